```python
import math
import jax, jax.numpy as jnp
from jax import lax
import numpy as np


D_MODEL = 1024
BATCH = 8
SEQ = 4096
DEPTH = 2
DEC_BATCH = 16
DEC_SEQ = 4096
PAST_LEN = 128

GRID_W = 64
QBLK = 128
N_MEM = 256
EPS = 1e-6
ROPE_THETA = 10000.0
A_HEADS = 8
A_KV_HEADS = 2
A_GROUP = A_HEADS // A_KV_HEADS
A_DIM = 64
B_HEADS = 4
B_DIM = 64
B_VDIM = 2 * B_DIM
A_Q = A_HEADS * A_DIM
A_KV = A_KV_HEADS * A_DIM
B_QK = B_HEADS * 2 * B_DIM
B_V = B_HEADS * B_VDIM
EVEN_IN = A_Q + 2 * A_KV + 2 * B_QK + B_V
EVEN_MIX = A_HEADS * A_DIM + B_HEADS * B_VDIM
C_HEADS = 16
C_Q_RANK = 384
C_KV_RANK = 256
C_NOPE = 64
C_ROPE = 32
C_VDIM = 64
ODD_IN = C_Q_RANK + C_KV_RANK + C_ROPE
C_MIX = C_HEADS * C_VDIM
X_HEADS = 4
X_DIM = D_MODEL // X_HEADS
D_FF = ((-(-8 * D_MODEL // 3) + 255) // 256) * 256
N_EVEN = (DEPTH + 1) // 2
N_ODD = DEPTH // 2

kernel_name = 'hybrid_gqa_diff_mla_encoder'


def rms_norm(x, g):
    xf = x.astype(jnp.float32)
    y = xf * lax.rsqrt(jnp.mean(xf * xf, axis=-1, keepdims=True) + EPS)
    return (y * g.astype(jnp.float32)).astype(x.dtype)


def rope(x, ang):
    extra = x.ndim - 3
    c = jnp.cos(ang).reshape(ang.shape[0], *([1] * extra), -1).astype(x.dtype)
    s = jnp.sin(ang).reshape(ang.shape[0], *([1] * extra), -1).astype(x.dtype)
    x2 = x.reshape(*x.shape[:-1], -1, 2)
    x0, x1 = x2[..., 0], x2[..., 1]
    return jnp.stack([x0 * c - x1 * s, x0 * s + x1 * c], axis=-1).reshape(x.shape)


def rope_freqs(n_pairs):
    return ROPE_THETA ** (-jnp.arange(n_pairs, dtype=jnp.float32) / n_pairs)


def axial_angles(S):
    rows = S // GRID_W
    r = jnp.repeat(jnp.arange(rows, dtype=jnp.float32), GRID_W)
    c = jnp.tile(jnp.arange(GRID_W, dtype=jnp.float32), rows)
    f = rope_freqs(A_DIM // 4)
    return jnp.concatenate([r[:, None] * f, c[:, None] * f], axis=-1)


def linear_angles(S, dim):
    t = jnp.arange(S, dtype=jnp.float32)
    return t[:, None] * rope_freqs(dim // 2)


def alibi_slopes(n):
    return jnp.asarray(2.0 ** (-8.0 * np.arange(1, n + 1) / n), dtype=jnp.float32)


def query_block_sweep(fn, qs, S):
    nb = S // QBLK
    xs = tuple(jnp.swapaxes(q.reshape(q.shape[0], nb, QBLK, *q.shape[2:]), 0, 1) for q in qs)
    starts = jnp.arange(nb, dtype=jnp.int32) * QBLK
    out = lax.map(lambda a: fn(*a[0], a[1]), (xs, starts))
    return jnp.swapaxes(out, 0, 1).reshape(qs[0].shape[0], S, *out.shape[3:])


def gqa_axial(qa, ka, va, gq, gk, ang):
    B, S = qa.shape[:2]
    q = rope(rms_norm(qa.reshape(B, S, A_KV_HEADS, A_GROUP, A_DIM), gq), ang)
    k = rope(rms_norm(ka.reshape(B, S, A_KV_HEADS, A_DIM), gk), ang)
    v = va.reshape(B, S, A_KV_HEADS, A_DIM)
    scale = A_DIM ** -0.5

    def blk(qb, start):
        s = jnp.einsum('bqhgd,bkhd->bhgqk', qb, k).astype(jnp.float32) * scale
        p = jax.nn.softmax(s, axis=-1).astype(v.dtype)
        return jnp.einsum('bhgqk,bkhd->bqhgd', p, v)

    o = query_block_sweep(blk, (q,), S)
    return o.reshape(B, S, A_HEADS * A_DIM)


def diff_attn(qb_in, kb_in, vb_in, lq1, lk1, lq2, lk2, g_sub, lam_init):
    B, S = qb_in.shape[:2]
    q = qb_in.reshape(B, S, B_HEADS, 2, B_DIM)
    k = kb_in.reshape(B, S, B_HEADS, 2, B_DIM)
    v = vb_in.reshape(B, S, B_HEADS, B_VDIM)
    f32 = jnp.float32
    lam = (jnp.exp(jnp.sum(lq1.astype(f32) * lk1.astype(f32)))
           - jnp.exp(jnp.sum(lq2.astype(f32) * lk2.astype(f32))) + lam_init)
    slopes = alibi_slopes(B_HEADS)
    kpos = jnp.arange(S, dtype=f32)
    scale = B_DIM ** -0.5

    def blk(qblk, start):
        s = jnp.einsum('bqhcd,bkhcd->bhcqk', qblk, k).astype(f32) * scale
        qpos = start.astype(f32) + jnp.arange(QBLK, dtype=f32)
        bias = -slopes[:, None, None] * jnp.abs(qpos[:, None] - kpos[None, :])
        p = jax.nn.softmax(s + bias[None, :, None], axis=-1)
        a = (p[:, :, 0] - lam * p[:, :, 1]).astype(v.dtype)
        return jnp.einsum('bhqk,bkhe->bqhe', a, v)

    o = query_block_sweep(blk, (q,), S)
    o = rms_norm(o, g_sub) * (1.0 - lam_init)
    return o.reshape(B, S, B_V)


def mla(h, w_in, g_q, g_kv, w_uq, w_ukv, ang):
    B, S = h.shape[:2]
    a = h @ w_in
    cq, ckv, kr = jnp.split(a, [C_Q_RANK, C_Q_RANK + C_KV_RANK], axis=-1)
    q = (rms_norm(cq, g_q) @ w_uq).reshape(B, S, C_HEADS, C_NOPE + C_ROPE)
    qn, qr = q[..., :C_NOPE], rope(q[..., C_NOPE:], ang)
    kv = (rms_norm(ckv, g_kv) @ w_ukv).reshape(B, S, C_HEADS, C_NOPE + C_VDIM)
    kn, v = kv[..., :C_NOPE], kv[..., C_NOPE:]
    kr = rope(kr, ang)
    scale = (C_NOPE + C_ROPE) ** -0.5

    def blk(qnb, qrb, start):
        s = (jnp.einsum('bqhd,bkhd->bhqk', qnb, kn)
             + jnp.einsum('bqhd,bkd->bhqk', qrb, kr)).astype(jnp.float32) * scale
        p = jax.nn.softmax(s, axis=-1).astype(v.dtype)
        return jnp.einsum('bhqk,bkhd->bqhd', p, v)

    o = query_block_sweep(blk, (qn, qr), S)
    return o.reshape(B, S, C_MIX)


def memory_cross_attn(h, mem_n, w_q, w_kv, w_o):
    B, S = h.shape[:2]
    M = mem_n.shape[1]
    q = (h @ w_q).reshape(B, S, X_HEADS, X_DIM)
    kv = (mem_n @ w_kv).reshape(B, M, 2, X_HEADS, X_DIM)
    s = jnp.einsum('bqhd,bkhd->bhqk', q, kv[:, :, 0]).astype(jnp.float32) * (X_DIM ** -0.5)
    p = jax.nn.softmax(s, axis=-1).astype(h.dtype)
    o = jnp.einsum('bhqk,bkhd->bqhd', p, kv[:, :, 1]).reshape(B, S, X_HEADS * X_DIM)
    return o @ w_o


def swiglu(h, w_gu, w_down):
    g, u = jnp.split(h @ w_gu, 2, axis=-1)
    return (jax.nn.silu(g) * u) @ w_down


def trunk(x, mem, norm_mix, e_w_in, e_q_norm, e_k_norm, e_lam_q1, e_lam_k1, e_lam_q2,
          e_lam_k2, e_subln, e_w_out, o_w_in, o_q_norm, o_kv_norm, o_w_uq, o_w_ukv,
          o_w_out, norm_cross, norm_mem, w_cq, w_ckv, w_co, norm_ffn, w_gu, w_down,
          final_norm):
    S = x.shape[1]
    ang_axial = axial_angles(S)
    ang_lin = linear_angles(S, C_ROPE)
    splits = [A_Q, A_Q + A_KV, A_Q + 2 * A_KV, A_Q + 2 * A_KV + B_QK, A_Q + 2 * A_KV + 2 * B_QK]
    for layer in range(DEPTH):
        h = rms_norm(x, norm_mix[layer])
        if layer % 2 == 0:
            e = layer // 2
            qa, ka, va, qb, kb, vb = jnp.split(h @ e_w_in[e], splits, axis=-1)
            oa = gqa_axial(qa, ka, va, e_q_norm[e], e_k_norm[e], ang_axial)
            lam_init = 0.8 - 0.6 * math.exp(-0.3 * layer)
            ob = diff_attn(qb, kb, vb, e_lam_q1[e], e_lam_k1[e], e_lam_q2[e], e_lam_k2[e],
                           e_subln[e], lam_init)
            x = x + jnp.concatenate([oa, ob], axis=-1) @ e_w_out[e]
        else:
            o = layer // 2
            x = x + mla(h, o_w_in[o], o_q_norm[o], o_kv_norm[o], o_w_uq[o], o_w_ukv[o],
                        ang_lin) @ o_w_out[o]
        x = x + memory_cross_attn(rms_norm(x, norm_cross[layer]), rms_norm(mem, norm_mem[layer]),
                                  w_cq[layer], w_ckv[layer], w_co[layer])
        x = x + swiglu(rms_norm(x, norm_ffn[layer]), w_gu[layer], w_down[layer])
    return rms_norm(x, final_norm)


def setup_inputs(seed: int = 0) -> dict:
    key = jax.random.key(seed)
    ks = jax.random.split(key, 32)
    f32 = jnp.float32

    def w(k, shape, fan_in):
        return jax.random.normal(k, shape, f32) * (fan_in ** -0.5)

    def gain(k, shape):
        return 1.0 + 0.02 * jax.random.normal(k, shape, f32)

    def small(k, shape):
        return 0.1 * jax.random.normal(k, shape, f32)

    D = D_MODEL
    return {
        'x_prompt': jax.random.normal(ks[0], (BATCH, SEQ, D), f32),
        'x_sample': jax.random.normal(ks[1], (DEC_BATCH, DEC_SEQ, D), f32),
        'mem_prompt': jax.random.normal(ks[2], (BATCH, N_MEM, D), f32),
        'mem_sample': jax.random.normal(ks[3], (DEC_BATCH, N_MEM, D), f32),
        'norm_mix': gain(ks[4], (DEPTH, D)),
        'e_w_in': w(ks[5], (N_EVEN, D, EVEN_IN), D),
        'e_q_norm': gain(ks[6], (N_EVEN, A_DIM)),
        'e_k_norm': gain(ks[7], (N_EVEN, A_DIM)),
        'e_lam_q1': small(ks[8], (N_EVEN, B_DIM)),
        'e_lam_k1': small(ks[9], (N_EVEN, B_DIM)),
        'e_lam_q2': small(ks[10], (N_EVEN, B_DIM)),
        'e_lam_k2': small(ks[11], (N_EVEN, B_DIM)),
        'e_subln': gain(ks[12], (N_EVEN, B_VDIM)),
        'e_w_out': w(ks[13], (N_EVEN, EVEN_MIX, D), EVEN_MIX),
        'o_w_in': w(ks[14], (N_ODD, D, ODD_IN), D),
        'o_q_norm': gain(ks[15], (N_ODD, C_Q_RANK)),
        'o_kv_norm': gain(ks[16], (N_ODD, C_KV_RANK)),
        'o_w_uq': w(ks[17], (N_ODD, C_Q_RANK, C_HEADS * (C_NOPE + C_ROPE)), C_Q_RANK),
        'o_w_ukv': w(ks[18], (N_ODD, C_KV_RANK, C_HEADS * (C_NOPE + C_VDIM)), C_KV_RANK),
        'o_w_out': w(ks[19], (N_ODD, C_MIX, D), C_MIX),
        'norm_cross': gain(ks[20], (DEPTH, D)),
        'norm_mem': gain(ks[21], (DEPTH, D)),
        'w_cq': w(ks[22], (DEPTH, D, X_HEADS * X_DIM), D),
        'w_ckv': w(ks[23], (DEPTH, D, 2 * X_HEADS * X_DIM), D),
        'w_co': w(ks[24], (DEPTH, X_HEADS * X_DIM, D), X_HEADS * X_DIM),
        'norm_ffn': gain(ks[25], (DEPTH, D)),
        'w_gu': w(ks[26], (DEPTH, D, 2 * D_FF), D),
        'w_down': w(ks[27], (DEPTH, D_FF, D), D_FF),
        'final_norm': gain(ks[28], (D,)),
    }


def reference(x_prompt, x_sample, mem_prompt, mem_sample, norm_mix, e_w_in, e_q_norm,
              e_k_norm, e_lam_q1, e_lam_k1, e_lam_q2, e_lam_k2, e_subln, e_w_out, o_w_in,
              o_q_norm, o_kv_norm, o_w_uq, o_w_ukv, o_w_out, norm_cross, norm_mem, w_cq,
              w_ckv, w_co, norm_ffn, w_gu, w_down, final_norm):
    y_prompt = trunk(x_prompt, mem_prompt, norm_mix, e_w_in, e_q_norm, e_k_norm, e_lam_q1,
                     e_lam_k1, e_lam_q2, e_lam_k2, e_subln, e_w_out, o_w_in, o_q_norm,
                     o_kv_norm, o_w_uq, o_w_ukv, o_w_out, norm_cross, norm_mem, w_cq, w_ckv,
                     w_co, norm_ffn, w_gu, w_down, final_norm)
    y_sample = trunk(x_sample, mem_sample, norm_mix, e_w_in, e_q_norm, e_k_norm, e_lam_q1,
                     e_lam_k1, e_lam_q2, e_lam_k2, e_subln, e_w_out, o_w_in, o_q_norm,
                     o_kv_norm, o_w_uq, o_w_ukv, o_w_out, norm_cross, norm_mem, w_cq, w_ckv,
                     w_co, norm_ffn, w_gu, w_down, final_norm)
    return (y_prompt, y_sample)
```

```python
import functools
import math

import jax
import jax.numpy as jnp
import numpy as np
from jax import lax
from jax.experimental import pallas as pl
from jax.experimental.pallas import tpu as pltpu

F32 = jnp.float32
BF16 = jnp.bfloat16

D_MODEL = 1024
GRID_W = 64
EPS = 1e-6
ROPE_THETA = 10000.0
A_HEADS, A_KV_HEADS, A_DIM = 8, 2, 64
A_GROUP = A_HEADS // A_KV_HEADS
B_HEADS, B_DIM = 4, 64
B_VDIM = 2 * B_DIM
A_Q = A_HEADS * A_DIM
A_KV = A_KV_HEADS * A_DIM
B_QK = B_HEADS * 2 * B_DIM
B_V = B_HEADS * B_VDIM
C_HEADS, C_Q_RANK, C_KV_RANK, C_NOPE, C_ROPE, C_VDIM = 16, 384, 256, 64, 32, 64
C_PAD = 128
X_HEADS = 4
X_DIM = D_MODEL // X_HEADS
D_FF = ((-(-8 * D_MODEL // 3) + 255) // 256) * 256
FF_CHUNK = 256
NEG_BIG = -1e30
LOG2E = math.log2(math.e)

TOKEN_TILE = 256
Q_TILE = 256
KEY_CHUNK = 512
VMEM_LIMIT = 48 * 1024 * 1024

_NT = (((1,), (1,)), ((), ()))


def _params(n_parallel, n_arbitrary=0):
    return pltpu.CompilerParams(
        dimension_semantics=("parallel",) * n_parallel + ("arbitrary",) * n_arbitrary,
        vmem_limit_bytes=VMEM_LIMIT)


def _rms_rows(x, g):
    ms = jnp.mean(x * x, axis=-1, keepdims=True)
    return (x * lax.rsqrt(ms + EPS)) * g


def _rms_cols(x, g):
    ms = jnp.mean(x * x, axis=0, keepdims=True)
    return (x * lax.rsqrt(ms + EPS)) * g


def _const_spec(shape):
    nd = len(shape)
    return pl.BlockSpec(shape, lambda *_: (0,) * nd)


def _even_in_kernel(x_ref, g_ref, wt_ref, wkb_ref, gq_ref, gk_ref, cos_ref, sin_ref,
                    qaT_ref, ka_ref, vaT_ref, qbT_ref, kb_ref, vbT_ref):
    half = A_DIM // 2
    xn = _rms_rows(x_ref[0], g_ref[...]).astype(BF16)
    yT = lax.dot_general(wt_ref[...], xn, _NT, preferred_element_type=F32)
    cos = cos_ref[...]
    sin = sin_ref[...]

    def norm_rope(xh, g, scale):
        y = _rms_cols(xh, g)
        e, o = y[:half], y[half:]
        return jnp.concatenate([e * cos - o * sin, e * sin + o * cos], axis=0) * scale

    scale_a = A_DIM ** -0.5
    for h in range(A_HEADS):
        qh = norm_rope(yT[h * A_DIM:(h + 1) * A_DIM], gq_ref[...], scale_a)
        qaT_ref[0, h * A_DIM:(h + 1) * A_DIM, :] = qh.astype(BF16)
    kT = jnp.concatenate(
        [norm_rope(yT[A_Q + h * A_DIM:A_Q + (h + 1) * A_DIM], gk_ref[...], 1.0)
         for h in range(A_KV_HEADS)], axis=0)
    ka_ref[0] = kT.T.astype(BF16)
    r0 = A_Q + A_KV
    vaT_ref[0] = yT[r0:r0 + A_KV].astype(BF16)
    r0 += A_KV
    qbT_ref[0] = (yT[r0:r0 + B_QK] * (B_DIM ** -0.5)).astype(BF16)
    r0 += B_QK
    vbT_ref[0] = yT[r0:r0 + B_V].astype(BF16)
    kb_ref[0] = jnp.dot(xn, wkb_ref[...], preferred_element_type=F32).astype(BF16)


def _even_in(x, g, wt, wkb, gq, gk, cosT, sinT):
    B, S, D = x.shape
    tm = min(TOKEN_TILE, S)
    rows = wt.shape[0]
    return pl.pallas_call(
        _even_in_kernel,
        grid=(B, S // tm),
        in_specs=[
            pl.BlockSpec((1, tm, D), lambda b, i: (b, i, 0)),
            _const_spec((1, D)),
            _const_spec((rows, D)),
            _const_spec((D, B_QK)),
            _const_spec((A_DIM, 1)),
            _const_spec((A_DIM, 1)),
            pl.BlockSpec((A_DIM // 2, tm), lambda b, i: (0, i)),
            pl.BlockSpec((A_DIM // 2, tm), lambda b, i: (0, i)),
        ],
        out_specs=[
            pl.BlockSpec((1, A_Q, tm), lambda b, i: (b, 0, i)),
            pl.BlockSpec((1, tm, A_KV), lambda b, i: (b, i, 0)),
            pl.BlockSpec((1, A_KV, tm), lambda b, i: (b, 0, i)),
            pl.BlockSpec((1, B_QK, tm), lambda b, i: (b, 0, i)),
            pl.BlockSpec((1, tm, B_QK), lambda b, i: (b, i, 0)),
            pl.BlockSpec((1, B_V, tm), lambda b, i: (b, 0, i)),
        ],
        out_shape=[
            jax.ShapeDtypeStruct((B, A_Q, S), BF16),
            jax.ShapeDtypeStruct((B, S, A_KV), BF16),
            jax.ShapeDtypeStruct((B, A_KV, S), BF16),
            jax.ShapeDtypeStruct((B, B_QK, S), BF16),
            jax.ShapeDtypeStruct((B, S, B_QK), BF16),
            jax.ShapeDtypeStruct((B, B_V, S), BF16),
        ],
        compiler_params=_params(2),
        name="even_in",
    )(x, g, wt, wkb, gq, gk, cosT, sinT)


def _softmax_pv_loop(score_fn, value_fn, n_chunks, tq, dv, exp_scale):
    def body(c, carry):
        m, l, acc = carry
        sT = score_fn(c)
        m_new = jnp.maximum(m, jnp.max(sT, axis=0, keepdims=True))
        alpha = jnp.exp2((m - m_new) * exp_scale)
        p = jnp.exp2((sT - m_new) * exp_scale)
        l = alpha * l + jnp.sum(p, axis=0, keepdims=True)
        acc = alpha * acc + jnp.dot(value_fn(c), p.astype(BF16), preferred_element_type=F32)
        return m_new, l, acc

    init = (jnp.full((1, tq), NEG_BIG, F32), jnp.zeros((1, tq), F32), jnp.zeros((dv, tq), F32))
    _, l, acc = lax.fori_loop(0, n_chunks, body, init)
    return acc, l


def _chunk_start(c, tkc):
    return pl.multiple_of(c * tkc, tkc)


def _gqa_kernel(qT_ref, k_ref, vT_ref, o_ref, *, tkc):
    g = pl.program_id(1)
    S = k_ref.shape[1]
    tq = qT_ref.shape[2]
    outs = []
    for j in range(A_GROUP):
        q = qT_ref[0, j * A_DIM:(j + 1) * A_DIM, :]
        zero = jnp.zeros_like(q)
        q_pad = jnp.concatenate([jnp.where(g == 0, q, zero), jnp.where(g == 1, q, zero)], axis=0)

        def score(c, q_pad=q_pad):
            k_c = k_ref[0, pl.ds(_chunk_start(c, tkc), tkc), :]
            return jnp.dot(k_c, q_pad, preferred_element_type=F32)

        def value(c):
            return vT_ref[0, :, pl.ds(_chunk_start(c, tkc), tkc)]

        acc, l = _softmax_pv_loop(score, value, S // tkc, tq, A_DIM, LOG2E)
        outs.append(acc / l)
    o_ref[0] = jnp.concatenate(outs, axis=0).T.astype(o_ref.dtype)


def _gqa(qaT, ka, vaT):
    B, _, S = qaT.shape
    tq = min(Q_TILE, S)
    tkc = min(KEY_CHUNK, S)
    gw = A_GROUP * A_DIM
    return pl.pallas_call(
        functools.partial(_gqa_kernel, tkc=tkc),
        grid=(B, A_KV_HEADS, S // tq),
        in_specs=[
            pl.BlockSpec((1, gw, tq), lambda b, g, i: (b, g, i)),
            pl.BlockSpec((1, S, A_KV), lambda b, g, i: (b, 0, 0)),
            pl.BlockSpec((1, A_DIM, S), lambda b, g, i: (b, g, 0)),
        ],
        out_specs=pl.BlockSpec((1, tq, gw), lambda b, g, i: (b, i, g)),
        out_shape=jax.ShapeDtypeStruct((B, S, A_Q), BF16),
        compiler_params=_params(3),
        name="gqa_attn",
    )(qaT, ka, vaT)


def _diff_kernel(slopes_ref, lq1_ref, lk1_ref, lq2_ref, lk2_ref, gsub_ref,
                 qT_ref, k_ref, vT_ref, o_ref, *, tkc, lam_init):
    h = pl.program_id(1)
    i = pl.program_id(2)
    S = k_ref.shape[1]
    tq = qT_ref.shape[2]
    slope = slopes_ref[h]
    q = qT_ref[0]
    rows = lax.broadcasted_iota(jnp.int32, q.shape, 0)
    zero = jnp.zeros_like(q)
    q_pads = (jnp.where(rows < B_DIM, q, zero), jnp.where(rows >= B_DIM, q, zero))
    dmat = (lax.broadcasted_iota(jnp.int32, (tkc, tq), 1)
            - lax.broadcasted_iota(jnp.int32, (tkc, tq), 0)).astype(F32)
    q_start = i * tq

    def value(c):
        return vT_ref[0, :, pl.ds(_chunk_start(c, tkc), tkc)]

    parts = []
    for comp in range(2):
        def score(c, q_pad=q_pads[comp]):
            start = _chunk_start(c, tkc)
            k_c = k_ref[0, pl.ds(start, tkc), :]
            off = (q_start - start).astype(F32)
            bias = -slope * jnp.abs(dmat + off)
            return jnp.dot(k_c, q_pad, preferred_element_type=F32) + bias

        acc, l = _softmax_pv_loop(score, value, S // tkc, tq, B_VDIM, LOG2E)
        parts.append(acc / l)

    lam = (jnp.exp(jnp.sum(lq1_ref[...] * lk1_ref[...], axis=-1, keepdims=True))
           - jnp.exp(jnp.sum(lq2_ref[...] * lk2_ref[...], axis=-1, keepdims=True)) + lam_init)
    o = parts[0] - lam * parts[1]
    o = _rms_cols(o, gsub_ref[...]) * (1.0 - lam_init)
    o_ref[0] = o.T.astype(o_ref.dtype)


def _diff(slopes, lq1, lk1, lq2, lk2, gsub, qbT, kb, vbT, lam_init):
    B, _, S = qbT.shape
    tq = min(Q_TILE, S)
    tkc = min(KEY_CHUNK, S)
    return pl.pallas_call(
        functools.partial(_diff_kernel, tkc=tkc, lam_init=lam_init),
        grid=(B, B_HEADS, S // tq),
        in_specs=[
            pl.BlockSpec(memory_space=pltpu.SMEM),
            _const_spec((1, B_DIM)), _const_spec((1, B_DIM)),
            _const_spec((1, B_DIM)), _const_spec((1, B_DIM)),
            _const_spec((B_VDIM, 1)),
            pl.BlockSpec((1, 2 * B_DIM, tq), lambda b, h, i: (b, h, i)),
            pl.BlockSpec((1, S, 2 * B_DIM), lambda b, h, i: (b, 0, h)),
            pl.BlockSpec((1, B_VDIM, S), lambda b, h, i: (b, h, 0)),
        ],
        out_specs=pl.BlockSpec((1, tq, B_VDIM), lambda b, h, i: (b, i, h)),
        out_shape=jax.ShapeDtypeStruct((B, S, B_V), BF16),
        compiler_params=_params(3),
        name="diff_attn",
    )(slopes, lq1, lk1, lq2, lk2, gsub, qbT, kb, vbT)


MLA_HEADS_PER_STEP = 2


def _mla_kernel(qT_ref, k_ref, vT_ref, o_ref, *, tkc):
    S = k_ref.shape[1]
    tq = qT_ref.shape[2]
    exp_scale = (C_NOPE + C_ROPE) ** -0.5 * LOG2E
    outs = []
    for j in range(MLA_HEADS_PER_STEP):
        q_pad = qT_ref[0, j * C_PAD:(j + 1) * C_PAD, :]

        def score(c, q_pad=q_pad, j=j):
            k_c = k_ref[0, pl.ds(_chunk_start(c, tkc), tkc), j * C_PAD:(j + 1) * C_PAD]
            return jnp.dot(k_c, q_pad, preferred_element_type=F32)

        def value(c, j=j):
            return vT_ref[0, j * C_VDIM:(j + 1) * C_VDIM, pl.ds(_chunk_start(c, tkc), tkc)]

        acc, l = _softmax_pv_loop(score, value, S // tkc, tq, C_VDIM, exp_scale)
        outs.append(acc / l)
    o_ref[0] = jnp.concatenate(outs, axis=0).T.astype(o_ref.dtype)


def _mla(qT, k, vT):
    B, _, S = qT.shape
    tq = min(Q_TILE, S)
    tkc = min(KEY_CHUNK, S)
    hp = MLA_HEADS_PER_STEP
    return pl.pallas_call(
        functools.partial(_mla_kernel, tkc=tkc),
        grid=(B, C_HEADS // hp, S // tq),
        in_specs=[
            pl.BlockSpec((1, hp * C_PAD, tq), lambda b, h, i: (b, h, i)),
            pl.BlockSpec((1, S, hp * C_PAD), lambda b, h, i: (b, 0, h)),
            pl.BlockSpec((1, hp * C_VDIM, S), lambda b, h, i: (b, h, 0)),
        ],
        out_specs=pl.BlockSpec((1, tq, hp * C_VDIM), lambda b, h, i: (b, i, h)),
        out_shape=jax.ShapeDtypeStruct((B, S, C_HEADS * C_VDIM), BF16),
        compiler_params=_params(3),
        name="mla_attn",
    )(qT, k, vT)


def _odd_in_kernel(x_ref, g_ref, win_ref, gq_ref, gkv_ref, wuqT_ref, wkn_ref, wvT_ref,
                   cosq_ref, sinq_ref, cosk_ref, sink_ref, qT_ref, k_ref, vT_ref):
    hr = C_ROPE // 2
    xn = _rms_rows(x_ref[0], g_ref[...]).astype(BF16)
    a = jnp.dot(xn, win_ref[...], preferred_element_type=F32)
    cqn = _rms_rows(a[:, :C_Q_RANK], gq_ref[...]).astype(BF16)
    ckvn = _rms_rows(a[:, C_Q_RANK:C_Q_RANK + C_KV_RANK], gkv_ref[...]).astype(BF16)

    qT = lax.dot_general(wuqT_ref[...], cqn, _NT, preferred_element_type=F32)
    cq, sq = cosq_ref[...], sinq_ref[...]
    for h in range(C_HEADS):
        r = h * C_PAD
        e = qT[r + C_NOPE:r + C_NOPE + hr]
        o = qT[r + C_NOPE + hr:r + C_NOPE + C_ROPE]
        head = jnp.concatenate(
            [qT[r:r + C_NOPE], e * cq - o * sq, e * sq + o * cq, qT[r + C_NOPE + C_ROPE:r + C_PAD]],
            axis=0)
        qT_ref[0, r:r + C_PAD, :] = head.astype(BF16)

    kblk = a[:, C_Q_RANK + C_KV_RANK:]
    t = kblk * cosk_ref[...] + kblk * sink_ref[...]
    lane = lax.broadcasted_iota(jnp.int32, t.shape, 1)
    kr = jnp.where(lane < C_ROPE, t + pltpu.roll(t, C_PAD - C_ROPE, axis=1), 0.0)
    kr = pltpu.roll(kr, C_NOPE, axis=1)

    kn = jnp.dot(ckvn, wkn_ref[...], preferred_element_type=F32)
    for h in range(C_HEADS):
        k_ref[0, :, h * C_PAD:(h + 1) * C_PAD] = (kn[:, h * C_PAD:(h + 1) * C_PAD] + kr).astype(BF16)

    vT_ref[0] = lax.dot_general(wvT_ref[...], ckvn, _NT, preferred_element_type=F32).astype(BF16)


def _odd_in(x, g, win, gq, gkv, wuqT, wkn, wvT, cosq, sinq, cosk, sink):
    B, S, D = x.shape
    tm = min(TOKEN_TILE, S)
    hr = C_ROPE // 2
    return pl.pallas_call(
        _odd_in_kernel,
        grid=(B, S // tm),
        in_specs=[
            pl.BlockSpec((1, tm, D), lambda b, i: (b, i, 0)),
            _const_spec((1, D)),
            _const_spec(win.shape),
            _const_spec((1, C_Q_RANK)),
            _const_spec((1, C_KV_RANK)),
            _const_spec(wuqT.shape),
            _const_spec(wkn.shape),
            _const_spec(wvT.shape),
            pl.BlockSpec((hr, tm), lambda b, i: (0, i)),
            pl.BlockSpec((hr, tm), lambda b, i: (0, i)),
            pl.BlockSpec((tm, C_PAD), lambda b, i: (i, 0)),
            pl.BlockSpec((tm, C_PAD), lambda b, i: (i, 0)),
        ],
        out_specs=[
            pl.BlockSpec((1, C_HEADS * C_PAD, tm), lambda b, i: (b, 0, i)),
            pl.BlockSpec((1, tm, C_HEADS * C_PAD), lambda b, i: (b, i, 0)),
            pl.BlockSpec((1, C_HEADS * C_VDIM, tm), lambda b, i: (b, 0, i)),
        ],
        out_shape=[
            jax.ShapeDtypeStruct((B, C_HEADS * C_PAD, S), BF16),
            jax.ShapeDtypeStruct((B, S, C_HEADS * C_PAD), BF16),
            jax.ShapeDtypeStruct((B, C_HEADS * C_VDIM, S), BF16),
        ],
        compiler_params=_params(2),
        name="odd_in",
    )(x, g, win, gq, gkv, wuqT, wkn, wvT, cosq, sinq, cosk, sink)


def _memkv_kernel(mem_ref, g_ref, w_ref, kv_ref):
    mn = _rms_rows(mem_ref[0], g_ref[...]).astype(BF16)
    kv_ref[0] = jnp.dot(mn, w_ref[...], preferred_element_type=F32).astype(BF16)


def _memkv(mem, g, w):
    B, M, D = mem.shape
    N = w.shape[1]
    return pl.pallas_call(
        _memkv_kernel,
        grid=(B,),
        in_specs=[pl.BlockSpec((1, M, D), lambda b: (b, 0, 0)), _const_spec((1, D)), _const_spec((D, N))],
        out_specs=pl.BlockSpec((1, M, N), lambda b: (b, 0, 0)),
        out_shape=jax.ShapeDtypeStruct((B, M, N), BF16),
        compiler_params=_params(1),
        name="mem_kv",
    )(mem, g, w)


def _post_mix_kernel(*refs, n_mix):
    x_ref = refs[0]
    o_refs = refs[1:1 + n_mix]
    w_refs = refs[1 + n_mix:1 + 2 * n_mix]
    gc_ref, wq_ref, kv_ref, wo_ref, out_ref = refs[1 + 2 * n_mix:]
    x = x_ref[0]
    for o_ref, w_ref in zip(o_refs, w_refs):
        x = x + jnp.dot(o_ref[0], w_ref[...], preferred_element_type=F32)

    hc = _rms_rows(x, gc_ref[...]).astype(BF16)
    q = (jnp.dot(hc, wq_ref[...], preferred_element_type=F32) * (X_DIM ** -0.5)).astype(BF16)
    heads = []
    for h in range(X_HEADS):
        k_h = kv_ref[0, :, h * X_DIM:(h + 1) * X_DIM]
        v_h = kv_ref[0, :, D_MODEL + h * X_DIM:D_MODEL + (h + 1) * X_DIM]
        s = lax.dot_general(q[:, h * X_DIM:(h + 1) * X_DIM], k_h, _NT, preferred_element_type=F32)
        p = jnp.exp(s - jnp.max(s, axis=-1, keepdims=True))
        l = jnp.sum(p, axis=-1, keepdims=True)
        heads.append((jnp.dot(p.astype(BF16), v_h, preferred_element_type=F32) / l).astype(BF16))
    o = jnp.concatenate(heads, axis=-1)
    out_ref[0] = x + jnp.dot(o, wo_ref[...], preferred_element_type=F32)


def _post_mix(x, mixes, weights, gc, wq, kv, wo):
    B, S, D = x.shape
    tm = min(TOKEN_TILE, S)
    n = len(mixes)
    M = kv.shape[1]
    tok = lambda b, i: (b, i, 0)
    return pl.pallas_call(
        functools.partial(_post_mix_kernel, n_mix=n),
        grid=(B, S // tm),
        in_specs=([pl.BlockSpec((1, tm, D), tok)]
                  + [pl.BlockSpec((1, tm, m.shape[2]), tok) for m in mixes]
                  + [_const_spec(w.shape) for w in weights]
                  + [_const_spec((1, D)), _const_spec((D, D)),
                     pl.BlockSpec((1, M, 2 * D), lambda b, i: (b, 0, 0)),
                     _const_spec((D, D))]),
        out_specs=pl.BlockSpec((1, tm, D), tok),
        out_shape=jax.ShapeDtypeStruct((B, S, D), F32),
        compiler_params=_params(2),
        name="post_mix",
    )(x, *mixes, *weights, gc, wq, kv, wo)


def _ffn_kernel(x_ref, g_ref, wgu_ref, wd_ref, gf_ref, out_ref, *, final_norm):
    x = x_ref[0]
    xn = _rms_rows(x, g_ref[...]).astype(BF16)
    acc = x
    for c in range(D_FF // FF_CHUNK):
        lo = c * FF_CHUNK
        gate = jnp.dot(xn, wgu_ref[:, lo:lo + FF_CHUNK], preferred_element_type=F32)
        up = jnp.dot(xn, wgu_ref[:, D_FF + lo:D_FF + lo + FF_CHUNK], preferred_element_type=F32)
        hidden = (jax.nn.silu(gate) * up).astype(BF16)
        acc = acc + jnp.dot(hidden, wd_ref[lo:lo + FF_CHUNK, :], preferred_element_type=F32)
    if final_norm:
        acc = _rms_rows(acc, gf_ref[...])
    out_ref[0] = acc


def _ffn(x, g, wgu, wd, gf, final_norm):
    B, S, D = x.shape
    tm = min(TOKEN_TILE, S)
    tok = lambda b, i: (b, i, 0)
    return pl.pallas_call(
        functools.partial(_ffn_kernel, final_norm=final_norm),
        grid=(B, S // tm),
        in_specs=[pl.BlockSpec((1, tm, D), tok), _const_spec((1, D)),
                  _const_spec(wgu.shape), _const_spec(wd.shape), _const_spec((1, D))],
        out_specs=pl.BlockSpec((1, tm, D), tok),
        out_shape=jax.ShapeDtypeStruct((B, S, D), F32),
        compiler_params=_params(2),
        name="ffn",
    )(x, g, wgu, wd, gf)


def _rope_freqs(n_pairs):
    return ROPE_THETA ** (-jnp.arange(n_pairs, dtype=F32) / n_pairs)


def _axial_angles(S):
    rows = S // GRID_W
    r = jnp.repeat(jnp.arange(rows, dtype=F32), GRID_W)
    c = jnp.tile(jnp.arange(GRID_W, dtype=F32), rows)
    f = _rope_freqs(A_DIM // 4)
    return jnp.concatenate([r[:, None] * f, c[:, None] * f], axis=-1)


def _linear_angles(S, dim):
    t = jnp.arange(S, dtype=F32)
    return t[:, None] * _rope_freqs(dim // 2)


def _deinterleave(n):
    return np.concatenate([np.arange(0, n, 2), np.arange(1, n, 2)])


def _prep_even(w_in, gq, gk):
    perm = _deinterleave(A_DIM)
    o = 0
    w_qa = w_in[:, o:o + A_Q].reshape(D_MODEL, A_HEADS, A_DIM)[:, :, perm].reshape(D_MODEL, A_Q)
    o += A_Q
    w_ka = w_in[:, o:o + A_KV].reshape(D_MODEL, A_KV_HEADS, A_DIM)[:, :, perm].reshape(D_MODEL, A_KV)
    o += A_KV
    w_va = w_in[:, o:o + A_KV]
    o += A_KV
    w_qb = w_in[:, o:o + B_QK]
    o += B_QK
    w_kb = w_in[:, o:o + B_QK]
    o += B_QK
    w_vb = w_in[:, o:o + B_V]
    wt = jnp.concatenate([w_qa, w_ka, w_va, w_qb, w_vb], axis=1).T.astype(BF16)
    return wt, w_kb.astype(BF16), gq[perm].reshape(A_DIM, 1), gk[perm].reshape(A_DIM, 1)


def _prep_odd(w_in, w_uq, w_ukv):
    hr = C_ROPE // 2
    perm = _deinterleave(C_ROPE)
    w_kr = w_in[:, C_Q_RANK + C_KV_RANK:][:, perm]
    w_kr_rot = jnp.concatenate([-w_kr[:, hr:], w_kr[:, :hr]], axis=1)
    win = jnp.concatenate(
        [w_in[:, :C_Q_RANK + C_KV_RANK], w_kr, w_kr_rot,
         jnp.zeros((D_MODEL, C_PAD - 2 * C_ROPE), F32)], axis=1).astype(BF16)
    wq = w_uq.reshape(C_Q_RANK, C_HEADS, C_NOPE + C_ROPE)
    wq = jnp.concatenate(
        [wq[:, :, :C_NOPE], wq[:, :, C_NOPE:][:, :, perm],
         jnp.zeros((C_Q_RANK, C_HEADS, C_PAD - C_NOPE - C_ROPE), F32)], axis=2)
    wuqT = wq.reshape(C_Q_RANK, C_HEADS * C_PAD).T.astype(BF16)
    wkv = w_ukv.reshape(C_KV_RANK, C_HEADS, C_NOPE + C_VDIM)
    wkn = jnp.concatenate(
        [wkv[:, :, :C_NOPE], jnp.zeros((C_KV_RANK, C_HEADS, C_PAD - C_NOPE), F32)], axis=2)
    wkn = wkn.reshape(C_KV_RANK, C_HEADS * C_PAD).astype(BF16)
    wvT = wkv[:, :, C_NOPE:].reshape(C_KV_RANK, C_HEADS * C_VDIM).T.astype(BF16)
    return win, wuqT, wkn, wvT


def _trunk(x, mem, p):
    B, S, D = x.shape
    depth = p['norm_mix'].shape[0]
    row = lambda v: v.reshape(1, -1)

    ang_a = _axial_angles(S)
    cos_a, sin_a = jnp.cos(ang_a).T, jnp.sin(ang_a).T
    ang_l = _linear_angles(S, C_ROPE)
    cos_l, sin_l = jnp.cos(ang_l), jnp.sin(ang_l)
    zpad = jnp.zeros((S, C_PAD - 2 * C_ROPE), F32)
    zrope = jnp.zeros((S, C_ROPE), F32)
    cos_k = jnp.concatenate([cos_l, cos_l, zrope, zpad], axis=1)
    sin_k = jnp.concatenate([zrope, sin_l, sin_l, zpad], axis=1)
    slopes = jnp.asarray(2.0 ** (-8.0 * np.arange(1, B_HEADS + 1) / B_HEADS), dtype=F32)

    for layer in range(depth):
        if layer % 2 == 0:
            e = layer // 2
            wt, wkb, gq, gk = _prep_even(p['e_w_in'][e], p['e_q_norm'][e], p['e_k_norm'][e])
            qaT, ka, vaT, qbT, kb, vbT = _even_in(
                x, row(p['norm_mix'][layer]), wt, wkb, gq, gk, cos_a, sin_a)
            oa = _gqa(qaT, ka, vaT)
            lam_init = 0.8 - 0.6 * math.exp(-0.3 * layer)
            ob = _diff(slopes, row(p['e_lam_q1'][e]), row(p['e_lam_k1'][e]),
                       row(p['e_lam_q2'][e]), row(p['e_lam_k2'][e]),
                       p['e_subln'][e].reshape(B_VDIM, 1), qbT, kb, vbT, lam_init)
            w_out = p['e_w_out'][e].astype(BF16)
            mixes, weights = [oa, ob], [w_out[:A_Q], w_out[A_Q:]]
        else:
            o = layer // 2
            win, wuqT, wkn, wvT = _prep_odd(p['o_w_in'][o], p['o_w_uq'][o], p['o_w_ukv'][o])
            qT, k, vT = _odd_in(x, row(p['norm_mix'][layer]), win, row(p['o_q_norm'][o]),
                                row(p['o_kv_norm'][o]), wuqT, wkn, wvT,
                                cos_l.T, sin_l.T, cos_k, sin_k)
            mixes, weights = [_mla(qT, k, vT)], [p['o_w_out'][o].astype(BF16)]
        kv = _memkv(mem, row(p['norm_mem'][layer]), p['w_ckv'][layer].astype(BF16))
        x = _post_mix(x, mixes, weights, row(p['norm_cross'][layer]),
                      p['w_cq'][layer].astype(BF16), kv, p['w_co'][layer].astype(BF16))
        x = _ffn(x, row(p['norm_ffn'][layer]), p['w_gu'][layer].astype(BF16),
                 p['w_down'][layer].astype(BF16), row(p['final_norm']),
                 final_norm=(layer == depth - 1))
    return x


def kernel(x_prompt, x_sample, mem_prompt, mem_sample, norm_mix, e_w_in, e_q_norm, e_k_norm, e_lam_q1, e_lam_k1, e_lam_q2, e_lam_k2, e_subln, e_w_out, o_w_in, o_q_norm, o_kv_norm, o_w_uq, o_w_ukv, o_w_out, norm_cross, norm_mem, w_cq, w_ckv, w_co, norm_ffn, w_gu, w_down, final_norm):
    p = dict(norm_mix=norm_mix, e_w_in=e_w_in, e_q_norm=e_q_norm, e_k_norm=e_k_norm,
             e_lam_q1=e_lam_q1, e_lam_k1=e_lam_k1, e_lam_q2=e_lam_q2, e_lam_k2=e_lam_k2,
             e_subln=e_subln, e_w_out=e_w_out, o_w_in=o_w_in, o_q_norm=o_q_norm,
             o_kv_norm=o_kv_norm, o_w_uq=o_w_uq, o_w_ukv=o_w_ukv, o_w_out=o_w_out,
             norm_cross=norm_cross, norm_mem=norm_mem, w_cq=w_cq, w_ckv=w_ckv, w_co=w_co,
             norm_ffn=norm_ffn, w_gu=w_gu, w_down=w_down, final_norm=final_norm)
    return (_trunk(x_prompt, mem_prompt, p), _trunk(x_sample, mem_sample, p))
```

```python
import functools
import math

import jax
import jax.numpy as jnp
import numpy as np
from jax import lax
from jax.experimental import pallas as pl
from jax.experimental.pallas import tpu as pltpu

F32 = jnp.float32
BF16 = jnp.bfloat16

D_MODEL = 1024
GRID_W = 64
EPS = 1e-6
ROPE_THETA = 10000.0
A_HEADS, A_KV_HEADS, A_DIM = 8, 2, 64
A_GROUP = A_HEADS // A_KV_HEADS
B_HEADS, B_DIM = 4, 64
B_VDIM = 2 * B_DIM
A_Q = A_HEADS * A_DIM
A_KV = A_KV_HEADS * A_DIM
B_QK = B_HEADS * 2 * B_DIM
B_V = B_HEADS * B_VDIM
C_HEADS, C_Q_RANK, C_KV_RANK, C_NOPE, C_ROPE, C_VDIM = 16, 384, 256, 64, 32, 64
C_PAD = 128
X_HEADS = 4
X_DIM = D_MODEL // X_HEADS
D_FF = ((-(-8 * D_MODEL // 3) + 255) // 256) * 256
FF_CHUNK = 256
NEG_BIG = -1e30
LOG2E = math.log2(math.e)

TOKEN_TILE = 256
Q_TILE = 256
KEY_CHUNK = 512
VMEM_LIMIT = 48 * 1024 * 1024

_NT = (((1,), (1,)), ((), ()))


def _params(n_parallel, n_arbitrary=0):
    return pltpu.CompilerParams(
        dimension_semantics=("parallel",) * n_parallel + ("arbitrary",) * n_arbitrary,
        vmem_limit_bytes=VMEM_LIMIT)


def _rms_rows(x, g):
    ms = jnp.mean(x * x, axis=-1, keepdims=True)
    return (x * lax.rsqrt(ms + EPS)) * g


def _rms_cols(x, g):
    ms = jnp.mean(x * x, axis=0, keepdims=True)
    return (x * lax.rsqrt(ms + EPS)) * g


def _const_spec(shape):
    nd = len(shape)
    return pl.BlockSpec(shape, lambda *_: (0,) * nd)


def _even_in_kernel(x_ref, g_ref, wt_ref, wkb_ref, gq_ref, gk_ref, cos_ref, sin_ref,
                    qaT_ref, ka_ref, vaT_ref, qbT_ref, kb_ref, vbT_ref):
    half = A_DIM // 2
    xn = _rms_rows(x_ref[0], g_ref[...]).astype(BF16)
    yT = lax.dot_general(wt_ref[...], xn, _NT, preferred_element_type=F32)
    cos = cos_ref[...]
    sin = sin_ref[...]

    def norm_rope(xh, g, scale):
        y = _rms_cols(xh, g)
        e, o = y[:half], y[half:]
        return jnp.concatenate([e * cos - o * sin, e * sin + o * cos], axis=0) * scale

    scale_a = A_DIM ** -0.5 * LOG2E
    for h in range(A_HEADS):
        qh = norm_rope(yT[h * A_DIM:(h + 1) * A_DIM], gq_ref[...], scale_a)
        qaT_ref[0, h * A_DIM:(h + 1) * A_DIM, :] = qh.astype(BF16)
    kT = jnp.concatenate(
        [norm_rope(yT[A_Q + h * A_DIM:A_Q + (h + 1) * A_DIM], gk_ref[...], 1.0)
         for h in range(A_KV_HEADS)], axis=0)
    ka_ref[0] = kT.T.astype(BF16)
    r0 = A_Q + A_KV
    vaT_ref[0] = yT[r0:r0 + A_KV].astype(BF16)
    r0 += A_KV
    qbT_ref[0] = (yT[r0:r0 + B_QK] * (B_DIM ** -0.5 * LOG2E)).astype(BF16)
    r0 += B_QK
    vbT_ref[0] = yT[r0:r0 + B_V].astype(BF16)
    kb_ref[0] = jnp.dot(xn, wkb_ref[...], preferred_element_type=F32).astype(BF16)


def _even_in(x, g, wt, wkb, gq, gk, cosT, sinT):
    B, S, D = x.shape
    tm = min(TOKEN_TILE, S)
    rows = wt.shape[0]
    return pl.pallas_call(
        _even_in_kernel,
        grid=(B, S // tm),
        in_specs=[
            pl.BlockSpec((1, tm, D), lambda b, i: (b, i, 0)),
            _const_spec((1, D)),
            _const_spec((rows, D)),
            _const_spec((D, B_QK)),
            _const_spec((A_DIM, 1)),
            _const_spec((A_DIM, 1)),
            pl.BlockSpec((A_DIM // 2, tm), lambda b, i: (0, i)),
            pl.BlockSpec((A_DIM // 2, tm), lambda b, i: (0, i)),
        ],
        out_specs=[
            pl.BlockSpec((1, A_Q, tm), lambda b, i: (b, 0, i)),
            pl.BlockSpec((1, tm, A_KV), lambda b, i: (b, i, 0)),
            pl.BlockSpec((1, A_KV, tm), lambda b, i: (b, 0, i)),
            pl.BlockSpec((1, B_QK, tm), lambda b, i: (b, 0, i)),
            pl.BlockSpec((1, tm, B_QK), lambda b, i: (b, i, 0)),
            pl.BlockSpec((1, B_V, tm), lambda b, i: (b, 0, i)),
        ],
        out_shape=[
            jax.ShapeDtypeStruct((B, A_Q, S), BF16),
            jax.ShapeDtypeStruct((B, S, A_KV), BF16),
            jax.ShapeDtypeStruct((B, A_KV, S), BF16),
            jax.ShapeDtypeStruct((B, B_QK, S), BF16),
            jax.ShapeDtypeStruct((B, S, B_QK), BF16),
            jax.ShapeDtypeStruct((B, B_V, S), BF16),
        ],
        compiler_params=_params(2),
        name="even_in",
    )(x, g, wt, wkb, gq, gk, cosT, sinT)


ONES_ROWS = 16


def _online_update(sT, v_ext, m, acc):
    m_new = jnp.maximum(m, jnp.max(sT, axis=0, keepdims=True))
    alpha = jnp.exp2(m - m_new)
    p = jnp.exp2(sT - m_new).astype(BF16)
    acc = alpha * acc + jnp.dot(v_ext, p, preferred_element_type=F32)
    return m_new, acc


def _chain_init(n, dv, tq):
    return tuple((jnp.full((1, tq), NEG_BIG, F32), jnp.zeros((dv + ONES_ROWS, tq), F32))
                 for _ in range(n))


def _chain_out(acc, dv):
    return acc[:dv] / acc[dv:dv + 1]


def _with_ones(vT_c):
    return jnp.concatenate([vT_c, jnp.ones((ONES_ROWS, vT_c.shape[1]), vT_c.dtype)], axis=0)


def _chunk_start(c, tkc):
    return c * tkc if isinstance(c, int) else pl.multiple_of(c * tkc, tkc)


def _key_chunk(S):
    return min(KEY_CHUNK, S // 2)


def _attend(n_chunks, n_chains, dv, tq, prep, score, value, s_bufs):
    assert n_chunks >= 2 and n_chunks % 2 == 0

    def stage(c_cur, slot, maxes, chains, c_next):
        ctx = None if c_next is None else prep(c_next)
        new_maxes, new_chains = [], []
        for j in range(n_chains):
            if ctx is not None:
                s_next = score(ctx, j)
                s_bufs[1 - slot][j][...] = s_next
                new_maxes.append(jnp.max(s_next, axis=0, keepdims=True))
            m, acc = chains[j]
            m_new = jnp.maximum(m, maxes[j])
            alpha = jnp.exp2(m - m_new)
            p = jnp.exp2(s_bufs[slot][j][...] - m_new).astype(BF16)
            acc = alpha * acc + jnp.dot(value(c_cur, j), p, preferred_element_type=F32)
            new_chains.append((m_new, acc))
        return tuple(new_maxes), tuple(new_chains)

    ctx0 = prep(0)
    maxes = []
    for j in range(n_chains):
        s0 = score(ctx0, j)
        s_bufs[0][j][...] = s0
        maxes.append(jnp.max(s0, axis=0, keepdims=True))

    def body(i, carry):
        c = 2 * i
        carry = stage(c, 0, *carry, c + 1)
        return stage(c + 1, 1, *carry, c + 2)

    carry = lax.fori_loop(0, n_chunks // 2 - 1, body, (tuple(maxes), _chain_init(n_chains, dv, tq)))
    carry = stage(n_chunks - 2, 0, *carry, n_chunks - 1)
    _, chains = stage(n_chunks - 1, 1, *carry, None)
    return [_chain_out(acc, dv) for _, acc in chains]


def _score_scratch(n_chains, tkc, tq):
    return [pltpu.VMEM((tkc, tq), F32) for _ in range(2 * n_chains)]


def _split_scratch(refs, n_chains):
    return [list(refs[:n_chains]), list(refs[n_chains:])]


def _gqa_kernel(qT_ref, k_ref, vT_ref, o_ref, *s_refs, tkc):
    g = pl.program_id(1)
    S = k_ref.shape[1]
    tq = qT_ref.shape[2]
    q_pads = []
    for j in range(A_GROUP):
        q = qT_ref[0, j * A_DIM:(j + 1) * A_DIM, :]
        zero = jnp.zeros_like(q)
        q_pads.append(jnp.concatenate(
            [jnp.where(g == 0, q, zero), jnp.where(g == 1, q, zero)], axis=0))

    def prep(c):
        return k_ref[0, pl.ds(_chunk_start(c, tkc), tkc), :]

    def score(k_c, j):
        return jnp.dot(k_c, q_pads[j], preferred_element_type=F32)

    def value(c, j):
        return _with_ones(vT_ref[0, :, pl.ds(_chunk_start(c, tkc), tkc)])

    outs = _attend(S // tkc, A_GROUP, A_DIM, tq, prep, score, value,
                   _split_scratch(s_refs, A_GROUP))
    o_ref[0] = jnp.concatenate(outs, axis=0).T.astype(o_ref.dtype)


def _gqa(qaT, ka, vaT):
    B, _, S = qaT.shape
    tq = min(Q_TILE, S)
    tkc = _key_chunk(S)
    gw = A_GROUP * A_DIM
    return pl.pallas_call(
        functools.partial(_gqa_kernel, tkc=tkc),
        grid=(B, A_KV_HEADS, S // tq),
        in_specs=[
            pl.BlockSpec((1, gw, tq), lambda b, g, i: (b, g, i)),
            pl.BlockSpec((1, S, A_KV), lambda b, g, i: (b, 0, 0)),
            pl.BlockSpec((1, A_DIM, S), lambda b, g, i: (b, g, 0)),
        ],
        out_specs=pl.BlockSpec((1, tq, gw), lambda b, g, i: (b, i, g)),
        out_shape=jax.ShapeDtypeStruct((B, S, A_Q), BF16),
        scratch_shapes=_score_scratch(A_GROUP, tkc, tq),
        compiler_params=_params(3),
        name="gqa_attn",
    )(qaT, ka, vaT)


DIFF_HEADS_PER_STEP = 2


def _diff_kernel(slopes_ref, lq1_ref, lk1_ref, lq2_ref, lk2_ref, gsub_ref,
                 qT_ref, k_ref, vT_ref, o_ref, *s_refs, tkc, lam_init):
    hp = pl.program_id(1)
    i = pl.program_id(2)
    S = k_ref.shape[1]
    tq = qT_ref.shape[2]
    pair = 2 * B_DIM
    neg_slopes = [-(slopes_ref[DIFF_HEADS_PER_STEP * hp + hh] * LOG2E)
                  for hh in range(DIFF_HEADS_PER_STEP)]
    q_pads = []
    for hh in range(DIFF_HEADS_PER_STEP):
        q = qT_ref[0, hh * pair:(hh + 1) * pair, :]
        rows = lax.broadcasted_iota(jnp.int32, q.shape, 0)
        zero = jnp.zeros_like(q)
        q_pads += [jnp.where(rows < B_DIM, q, zero), jnp.where(rows >= B_DIM, q, zero)]
    dmat = (lax.broadcasted_iota(jnp.int32, (tkc, tq), 1)
            - lax.broadcasted_iota(jnp.int32, (tkc, tq), 0)).astype(F32)
    q_start = i * tq

    def prep(c):
        start = _chunk_start(c, tkc)
        dist = jnp.abs(dmat + (q_start - start).astype(F32))
        return start, [dist * ns for ns in neg_slopes]

    def score(ctx, n):
        start, biases = ctx
        hh = n // 2
        k_c = k_ref[0, pl.ds(start, tkc), hh * pair:(hh + 1) * pair]
        return jnp.dot(k_c, q_pads[n], preferred_element_type=F32) + biases[hh]

    def value(c, n):
        hh = n // 2
        return _with_ones(vT_ref[0, hh * B_VDIM:(hh + 1) * B_VDIM, pl.ds(_chunk_start(c, tkc), tkc)])

    n_chains = 2 * DIFF_HEADS_PER_STEP
    parts = _attend(S // tkc, n_chains, B_VDIM, tq, prep, score, value,
                    _split_scratch(s_refs, n_chains))

    lam = (jnp.exp(jnp.sum(lq1_ref[...] * lk1_ref[...], axis=-1, keepdims=True))
           - jnp.exp(jnp.sum(lq2_ref[...] * lk2_ref[...], axis=-1, keepdims=True)) + lam_init)
    outs = []
    for hh in range(DIFF_HEADS_PER_STEP):
        o = parts[2 * hh] - lam * parts[2 * hh + 1]
        outs.append(_rms_cols(o, gsub_ref[...]) * (1.0 - lam_init))
    o_ref[0] = jnp.concatenate(outs, axis=0).T.astype(o_ref.dtype)


def _diff(slopes, lq1, lk1, lq2, lk2, gsub, qbT, kb, vbT, lam_init):
    B, _, S = qbT.shape
    tq = min(Q_TILE, S)
    tkc = _key_chunk(S)
    hs = DIFF_HEADS_PER_STEP
    return pl.pallas_call(
        functools.partial(_diff_kernel, tkc=tkc, lam_init=lam_init),
        grid=(B, B_HEADS // hs, S // tq),
        in_specs=[
            pl.BlockSpec(memory_space=pltpu.SMEM),
            _const_spec((1, B_DIM)), _const_spec((1, B_DIM)),
            _const_spec((1, B_DIM)), _const_spec((1, B_DIM)),
            _const_spec((B_VDIM, 1)),
            pl.BlockSpec((1, hs * 2 * B_DIM, tq), lambda b, h, i: (b, h, i)),
            pl.BlockSpec((1, S, hs * 2 * B_DIM), lambda b, h, i: (b, 0, h)),
            pl.BlockSpec((1, hs * B_VDIM, S), lambda b, h, i: (b, h, 0)),
        ],
        out_specs=pl.BlockSpec((1, tq, hs * B_VDIM), lambda b, h, i: (b, i, h)),
        out_shape=jax.ShapeDtypeStruct((B, S, B_V), BF16),
        scratch_shapes=_score_scratch(2 * hs, tkc, tq),
        compiler_params=_params(3),
        name="diff_attn",
    )(slopes, lq1, lk1, lq2, lk2, gsub, qbT, kb, vbT)


MLA_HEADS_PER_STEP = 4


def _mla_kernel(qT_ref, k_ref, vT_ref, o_ref, *s_refs, tkc):
    S = k_ref.shape[1]
    tq = qT_ref.shape[2]
    nh = MLA_HEADS_PER_STEP
    q_pads = [qT_ref[0, j * C_PAD:(j + 1) * C_PAD, :] for j in range(nh)]

    def prep(c):
        return _chunk_start(c, tkc)

    def score(start, j):
        k_c = k_ref[0, pl.ds(start, tkc), j * C_PAD:(j + 1) * C_PAD]
        return jnp.dot(k_c, q_pads[j], preferred_element_type=F32)

    def value(c, j):
        return _with_ones(vT_ref[0, j * C_VDIM:(j + 1) * C_VDIM, pl.ds(_chunk_start(c, tkc), tkc)])

    outs = _attend(S // tkc, nh, C_VDIM, tq, prep, score, value, _split_scratch(s_refs, nh))
    o_ref[0] = jnp.concatenate(outs, axis=0).T.astype(o_ref.dtype)


def _mla(qT, k, vT):
    B, _, S = qT.shape
    tq = min(Q_TILE, S)
    tkc = _key_chunk(S)
    hp = MLA_HEADS_PER_STEP
    return pl.pallas_call(
        functools.partial(_mla_kernel, tkc=tkc),
        grid=(B, C_HEADS // hp, S // tq),
        in_specs=[
            pl.BlockSpec((1, hp * C_PAD, tq), lambda b, h, i: (b, h, i)),
            pl.BlockSpec((1, S, hp * C_PAD), lambda b, h, i: (b, 0, h)),
            pl.BlockSpec((1, hp * C_VDIM, S), lambda b, h, i: (b, h, 0)),
        ],
        out_specs=pl.BlockSpec((1, tq, hp * C_VDIM), lambda b, h, i: (b, i, h)),
        out_shape=jax.ShapeDtypeStruct((B, S, C_HEADS * C_VDIM), BF16),
        scratch_shapes=_score_scratch(hp, tkc, tq),
        compiler_params=_params(3),
        name="mla_attn",
    )(qT, k, vT)


def _odd_in_kernel(x_ref, g_ref, win_ref, gq_ref, gkv_ref, wuqT_ref, wkn_ref, wvT_ref,
                   cosq_ref, sinq_ref, cosk_ref, sink_ref, qT_ref, k_ref, vT_ref):
    hr = C_ROPE // 2
    xn = _rms_rows(x_ref[0], g_ref[...]).astype(BF16)
    a = jnp.dot(xn, win_ref[...], preferred_element_type=F32)
    cqn = _rms_rows(a[:, :C_Q_RANK], gq_ref[...]).astype(BF16)
    ckvn = _rms_rows(a[:, C_Q_RANK:C_Q_RANK + C_KV_RANK], gkv_ref[...]).astype(BF16)

    qT = lax.dot_general(wuqT_ref[...], cqn, _NT, preferred_element_type=F32)
    qT = qT * ((C_NOPE + C_ROPE) ** -0.5 * LOG2E)
    cq, sq = cosq_ref[...], sinq_ref[...]
    for h in range(C_HEADS):
        r = h * C_PAD
        e = qT[r + C_NOPE:r + C_NOPE + hr]
        o = qT[r + C_NOPE + hr:r + C_NOPE + C_ROPE]
        head = jnp.concatenate(
            [qT[r:r + C_NOPE], e * cq - o * sq, e * sq + o * cq, qT[r + C_NOPE + C_ROPE:r + C_PAD]],
            axis=0)
        qT_ref[0, r:r + C_PAD, :] = head.astype(BF16)

    kblk = a[:, C_Q_RANK + C_KV_RANK:]
    t = kblk * cosk_ref[...] + kblk * sink_ref[...]
    lane = lax.broadcasted_iota(jnp.int32, t.shape, 1)
    kr = jnp.where(lane < C_ROPE, t + pltpu.roll(t, C_PAD - C_ROPE, axis=1), 0.0)
    kr = pltpu.roll(kr, C_NOPE, axis=1)

    kn = jnp.dot(ckvn, wkn_ref[...], preferred_element_type=F32)
    for h in range(C_HEADS):
        k_ref[0, :, h * C_PAD:(h + 1) * C_PAD] = (kn[:, h * C_PAD:(h + 1) * C_PAD] + kr).astype(BF16)

    vT_ref[0] = lax.dot_general(wvT_ref[...], ckvn, _NT, preferred_element_type=F32).astype(BF16)


def _odd_in(x, g, win, gq, gkv, wuqT, wkn, wvT, cosq, sinq, cosk, sink):
    B, S, D = x.shape
    tm = min(TOKEN_TILE, S)
    hr = C_ROPE // 2
    return pl.pallas_call(
        _odd_in_kernel,
        grid=(B, S // tm),
        in_specs=[
            pl.BlockSpec((1, tm, D), lambda b, i: (b, i, 0)),
            _const_spec((1, D)),
            _const_spec(win.shape),
            _const_spec((1, C_Q_RANK)),
            _const_spec((1, C_KV_RANK)),
            _const_spec(wuqT.shape),
            _const_spec(wkn.shape),
            _const_spec(wvT.shape),
            pl.BlockSpec((hr, tm), lambda b, i: (0, i)),
            pl.BlockSpec((hr, tm), lambda b, i: (0, i)),
            pl.BlockSpec((tm, C_PAD), lambda b, i: (i, 0)),
            pl.BlockSpec((tm, C_PAD), lambda b, i: (i, 0)),
        ],
        out_specs=[
            pl.BlockSpec((1, C_HEADS * C_PAD, tm), lambda b, i: (b, 0, i)),
            pl.BlockSpec((1, tm, C_HEADS * C_PAD), lambda b, i: (b, i, 0)),
            pl.BlockSpec((1, C_HEADS * C_VDIM, tm), lambda b, i: (b, 0, i)),
        ],
        out_shape=[
            jax.ShapeDtypeStruct((B, C_HEADS * C_PAD, S), BF16),
            jax.ShapeDtypeStruct((B, S, C_HEADS * C_PAD), BF16),
            jax.ShapeDtypeStruct((B, C_HEADS * C_VDIM, S), BF16),
        ],
        compiler_params=_params(2),
        name="odd_in",
    )(x, g, win, gq, gkv, wuqT, wkn, wvT, cosq, sinq, cosk, sink)


def _memkv_kernel(mem_ref, g_ref, w_ref, kv_ref):
    mn = _rms_rows(mem_ref[0], g_ref[...]).astype(BF16)
    kv_ref[0] = jnp.dot(mn, w_ref[...], preferred_element_type=F32).astype(BF16)


def _memkv(mem, g, w):
    B, M, D = mem.shape
    N = w.shape[1]
    return pl.pallas_call(
        _memkv_kernel,
        grid=(B,),
        in_specs=[pl.BlockSpec((1, M, D), lambda b: (b, 0, 0)), _const_spec((1, D)), _const_spec((D, N))],
        out_specs=pl.BlockSpec((1, M, N), lambda b: (b, 0, 0)),
        out_shape=jax.ShapeDtypeStruct((B, M, N), BF16),
        compiler_params=_params(1),
        name="mem_kv",
    )(mem, g, w)


def _post_mix_kernel(*refs, n_mix):
    x_ref = refs[0]
    o_refs = refs[1:1 + n_mix]
    w_refs = refs[1 + n_mix:1 + 2 * n_mix]
    gc_ref, wq_ref, kv_ref, wo_ref, out_ref = refs[1 + 2 * n_mix:]
    x = x_ref[0]
    for o_ref, w_ref in zip(o_refs, w_refs):
        x = x + jnp.dot(o_ref[0], w_ref[...], preferred_element_type=F32)

    hc = _rms_rows(x, gc_ref[...]).astype(BF16)
    q = (jnp.dot(hc, wq_ref[...], preferred_element_type=F32) * (X_DIM ** -0.5)).astype(BF16)
    heads = []
    for h in range(X_HEADS):
        k_h = kv_ref[0, :, h * X_DIM:(h + 1) * X_DIM]
        v_h = kv_ref[0, :, D_MODEL + h * X_DIM:D_MODEL + (h + 1) * X_DIM]
        s = lax.dot_general(q[:, h * X_DIM:(h + 1) * X_DIM], k_h, _NT, preferred_element_type=F32)
        p = jnp.exp(s - jnp.max(s, axis=-1, keepdims=True))
        l = jnp.sum(p, axis=-1, keepdims=True)
        heads.append((jnp.dot(p.astype(BF16), v_h, preferred_element_type=F32) / l).astype(BF16))
    o = jnp.concatenate(heads, axis=-1)
    out_ref[0] = x + jnp.dot(o, wo_ref[...], preferred_element_type=F32)


def _post_mix(x, mixes, weights, gc, wq, kv, wo):
    B, S, D = x.shape
    tm = min(TOKEN_TILE, S)
    n = len(mixes)
    M = kv.shape[1]
    tok = lambda b, i: (b, i, 0)
    return pl.pallas_call(
        functools.partial(_post_mix_kernel, n_mix=n),
        grid=(B, S // tm),
        in_specs=([pl.BlockSpec((1, tm, D), tok)]
                  + [pl.BlockSpec((1, tm, m.shape[2]), tok) for m in mixes]
                  + [_const_spec(w.shape) for w in weights]
                  + [_const_spec((1, D)), _const_spec((D, D)),
                     pl.BlockSpec((1, M, 2 * D), lambda b, i: (b, 0, 0)),
                     _const_spec((D, D))]),
        out_specs=pl.BlockSpec((1, tm, D), tok),
        out_shape=jax.ShapeDtypeStruct((B, S, D), F32),
        compiler_params=_params(2),
        name="post_mix",
    )(x, *mixes, *weights, gc, wq, kv, wo)


def _ffn_kernel(x_ref, g_ref, wgu_ref, wd_ref, gf_ref, out_ref, *, final_norm):
    x = x_ref[0]
    xn = _rms_rows(x, g_ref[...]).astype(BF16)
    acc = x
    for c in range(D_FF // FF_CHUNK):
        lo = c * FF_CHUNK
        gate = jnp.dot(xn, wgu_ref[:, lo:lo + FF_CHUNK], preferred_element_type=F32)
        up = jnp.dot(xn, wgu_ref[:, D_FF + lo:D_FF + lo + FF_CHUNK], preferred_element_type=F32)
        hidden = (jax.nn.silu(gate) * up).astype(BF16)
        acc = acc + jnp.dot(hidden, wd_ref[lo:lo + FF_CHUNK, :], preferred_element_type=F32)
    if final_norm:
        acc = _rms_rows(acc, gf_ref[...])
    out_ref[0] = acc


def _ffn(x, g, wgu, wd, gf, final_norm):
    B, S, D = x.shape
    tm = min(TOKEN_TILE, S)
    tok = lambda b, i: (b, i, 0)
    return pl.pallas_call(
        functools.partial(_ffn_kernel, final_norm=final_norm),
        grid=(B, S // tm),
        in_specs=[pl.BlockSpec((1, tm, D), tok), _const_spec((1, D)),
                  _const_spec(wgu.shape), _const_spec(wd.shape), _const_spec((1, D))],
        out_specs=pl.BlockSpec((1, tm, D), tok),
        out_shape=jax.ShapeDtypeStruct((B, S, D), F32),
        compiler_params=_params(2),
        name="ffn",
    )(x, g, wgu, wd, gf)


def _rope_freqs(n_pairs):
    return ROPE_THETA ** (-jnp.arange(n_pairs, dtype=F32) / n_pairs)


def _axial_angles(S):
    rows = S // GRID_W
    r = jnp.repeat(jnp.arange(rows, dtype=F32), GRID_W)
    c = jnp.tile(jnp.arange(GRID_W, dtype=F32), rows)
    f = _rope_freqs(A_DIM // 4)
    return jnp.concatenate([r[:, None] * f, c[:, None] * f], axis=-1)


def _linear_angles(S, dim):
    t = jnp.arange(S, dtype=F32)
    return t[:, None] * _rope_freqs(dim // 2)


def _deinterleave(n):
    return np.concatenate([np.arange(0, n, 2), np.arange(1, n, 2)])


def _prep_even(w_in, gq, gk):
    perm = _deinterleave(A_DIM)
    o = 0
    w_qa = w_in[:, o:o + A_Q].reshape(D_MODEL, A_HEADS, A_DIM)[:, :, perm].reshape(D_MODEL, A_Q)
    o += A_Q
    w_ka = w_in[:, o:o + A_KV].reshape(D_MODEL, A_KV_HEADS, A_DIM)[:, :, perm].reshape(D_MODEL, A_KV)
    o += A_KV
    w_va = w_in[:, o:o + A_KV]
    o += A_KV
    w_qb = w_in[:, o:o + B_QK]
    o += B_QK
    w_kb = w_in[:, o:o + B_QK]
    o += B_QK
    w_vb = w_in[:, o:o + B_V]
    wt = jnp.concatenate([w_qa, w_ka, w_va, w_qb, w_vb], axis=1).T.astype(BF16)
    return wt, w_kb.astype(BF16), gq[perm].reshape(A_DIM, 1), gk[perm].reshape(A_DIM, 1)


def _prep_odd(w_in, w_uq, w_ukv):
    hr = C_ROPE // 2
    perm = _deinterleave(C_ROPE)
    w_kr = w_in[:, C_Q_RANK + C_KV_RANK:][:, perm]
    w_kr_rot = jnp.concatenate([-w_kr[:, hr:], w_kr[:, :hr]], axis=1)
    win = jnp.concatenate(
        [w_in[:, :C_Q_RANK + C_KV_RANK], w_kr, w_kr_rot,
         jnp.zeros((D_MODEL, C_PAD - 2 * C_ROPE), F32)], axis=1).astype(BF16)
    wq = w_uq.reshape(C_Q_RANK, C_HEADS, C_NOPE + C_ROPE)
    wq = jnp.concatenate(
        [wq[:, :, :C_NOPE], wq[:, :, C_NOPE:][:, :, perm],
         jnp.zeros((C_Q_RANK, C_HEADS, C_PAD - C_NOPE - C_ROPE), F32)], axis=2)
    wuqT = wq.reshape(C_Q_RANK, C_HEADS * C_PAD).T.astype(BF16)
    wkv = w_ukv.reshape(C_KV_RANK, C_HEADS, C_NOPE + C_VDIM)
    wkn = jnp.concatenate(
        [wkv[:, :, :C_NOPE], jnp.zeros((C_KV_RANK, C_HEADS, C_PAD - C_NOPE), F32)], axis=2)
    wkn = wkn.reshape(C_KV_RANK, C_HEADS * C_PAD).astype(BF16)
    wvT = wkv[:, :, C_NOPE:].reshape(C_KV_RANK, C_HEADS * C_VDIM).T.astype(BF16)
    return win, wuqT, wkn, wvT


def _trunk(x, mem, p):
    B, S, D = x.shape
    depth = p['norm_mix'].shape[0]
    row = lambda v: v.reshape(1, -1)

    ang_a = _axial_angles(S)
    cos_a, sin_a = jnp.cos(ang_a).T, jnp.sin(ang_a).T
    ang_l = _linear_angles(S, C_ROPE)
    cos_l, sin_l = jnp.cos(ang_l), jnp.sin(ang_l)
    zpad = jnp.zeros((S, C_PAD - 2 * C_ROPE), F32)
    zrope = jnp.zeros((S, C_ROPE), F32)
    cos_k = jnp.concatenate([cos_l, cos_l, zrope, zpad], axis=1)
    sin_k = jnp.concatenate([zrope, sin_l, sin_l, zpad], axis=1)
    slopes = jnp.asarray(2.0 ** (-8.0 * np.arange(1, B_HEADS + 1) / B_HEADS), dtype=F32)

    for layer in range(depth):
        if layer % 2 == 0:
            e = layer // 2
            wt, wkb, gq, gk = _prep_even(p['e_w_in'][e], p['e_q_norm'][e], p['e_k_norm'][e])
            qaT, ka, vaT, qbT, kb, vbT = _even_in(
                x, row(p['norm_mix'][layer]), wt, wkb, gq, gk, cos_a, sin_a)
            oa = _gqa(qaT, ka, vaT)
            lam_init = 0.8 - 0.6 * math.exp(-0.3 * layer)
            ob = _diff(slopes, row(p['e_lam_q1'][e]), row(p['e_lam_k1'][e]),
                       row(p['e_lam_q2'][e]), row(p['e_lam_k2'][e]),
                       p['e_subln'][e].reshape(B_VDIM, 1), qbT, kb, vbT, lam_init)
            w_out = p['e_w_out'][e].astype(BF16)
            mixes, weights = [oa, ob], [w_out[:A_Q], w_out[A_Q:]]
        else:
            o = layer // 2
            win, wuqT, wkn, wvT = _prep_odd(p['o_w_in'][o], p['o_w_uq'][o], p['o_w_ukv'][o])
            qT, k, vT = _odd_in(x, row(p['norm_mix'][layer]), win, row(p['o_q_norm'][o]),
                                row(p['o_kv_norm'][o]), wuqT, wkn, wvT,
                                cos_l.T, sin_l.T, cos_k, sin_k)
            mixes, weights = [_mla(qT, k, vT)], [p['o_w_out'][o].astype(BF16)]
        kv = _memkv(mem, row(p['norm_mem'][layer]), p['w_ckv'][layer].astype(BF16))
        x = _post_mix(x, mixes, weights, row(p['norm_cross'][layer]),
                      p['w_cq'][layer].astype(BF16), kv, p['w_co'][layer].astype(BF16))
        x = _ffn(x, row(p['norm_ffn'][layer]), p['w_gu'][layer].astype(BF16),
                 p['w_down'][layer].astype(BF16), row(p['final_norm']),
                 final_norm=(layer == depth - 1))
    return x


def kernel(x_prompt, x_sample, mem_prompt, mem_sample, norm_mix, e_w_in, e_q_norm, e_k_norm, e_lam_q1, e_lam_k1, e_lam_q2, e_lam_k2, e_subln, e_w_out, o_w_in, o_q_norm, o_kv_norm, o_w_uq, o_w_ukv, o_w_out, norm_cross, norm_mem, w_cq, w_ckv, w_co, norm_ffn, w_gu, w_down, final_norm):
    p = dict(norm_mix=norm_mix, e_w_in=e_w_in, e_q_norm=e_q_norm, e_k_norm=e_k_norm,
             e_lam_q1=e_lam_q1, e_lam_k1=e_lam_k1, e_lam_q2=e_lam_q2, e_lam_k2=e_lam_k2,
             e_subln=e_subln, e_w_out=e_w_out, o_w_in=o_w_in, o_q_norm=o_q_norm,
             o_kv_norm=o_kv_norm, o_w_uq=o_w_uq, o_w_ukv=o_w_ukv, o_w_out=o_w_out,
             norm_cross=norm_cross, norm_mem=norm_mem, w_cq=w_cq, w_ckv=w_ckv, w_co=w_co,
             norm_ffn=norm_ffn, w_gu=w_gu, w_down=w_down, final_norm=final_norm)
    return (_trunk(x_prompt, mem_prompt, p), _trunk(x_sample, mem_sample, p))
```

```python
import functools
import math

import jax
import jax.numpy as jnp
import numpy as np
from jax import lax
from jax.experimental import pallas as pl
from jax.experimental.pallas import tpu as pltpu

F32 = jnp.float32
BF16 = jnp.bfloat16

D_MODEL = 1024
GRID_W = 64
EPS = 1e-6
ROPE_THETA = 10000.0
A_HEADS, A_KV_HEADS, A_DIM = 8, 2, 64
A_GROUP = A_HEADS // A_KV_HEADS
B_HEADS, B_DIM = 4, 64
B_VDIM = 2 * B_DIM
A_Q = A_HEADS * A_DIM
A_KV = A_KV_HEADS * A_DIM
B_QK = B_HEADS * 2 * B_DIM
B_V = B_HEADS * B_VDIM
C_HEADS, C_Q_RANK, C_KV_RANK, C_NOPE, C_ROPE, C_VDIM = 16, 384, 256, 64, 32, 64
C_PAD = 128
X_HEADS = 4
X_DIM = D_MODEL // X_HEADS
D_FF = ((-(-8 * D_MODEL // 3) + 255) // 256) * 256
FF_CHUNK = 256
NEG_BIG = -1e30
LOG2E = math.log2(math.e)

TOKEN_TILE = 512
WIDE_TOKEN_TILE = 512
Q_TILE = 256
KEY_CHUNK = 512
VMEM_LIMIT = 48 * 1024 * 1024

_NT = (((1,), (1,)), ((), ()))


def _params(n_parallel, n_arbitrary=0):
    return pltpu.CompilerParams(
        dimension_semantics=("parallel",) * n_parallel + ("arbitrary",) * n_arbitrary,
        vmem_limit_bytes=VMEM_LIMIT)


def _rms_rows(x, g):
    ms = jnp.mean(x * x, axis=-1, keepdims=True)
    return (x * lax.rsqrt(ms + EPS)) * g


def _rms_cols(x, g):
    ms = jnp.mean(x * x, axis=0, keepdims=True)
    return (x * lax.rsqrt(ms + EPS)) * g


def _const_spec(shape):
    nd = len(shape)
    return pl.BlockSpec(shape, lambda *_: (0,) * nd)


def _even_in_kernel(x_ref, g_ref, wt_ref, wkb_ref, gq_ref, gk_ref, cos_ref, sin_ref,
                    qaT_ref, ka_ref, vaT_ref, qbT_ref, kb_ref, vbT_ref):
    half = A_DIM // 2
    xn = _rms_rows(x_ref[0], g_ref[...]).astype(BF16)
    yT = lax.dot_general(wt_ref[...], xn, _NT, preferred_element_type=F32)
    cos = cos_ref[...]
    sin = sin_ref[...]

    def norm_rope(xh, g, scale):
        y = _rms_cols(xh, g)
        e, o = y[:half], y[half:]
        return jnp.concatenate([e * cos - o * sin, e * sin + o * cos], axis=0) * scale

    scale_a = A_DIM ** -0.5 * LOG2E
    for h in range(A_HEADS):
        qh = norm_rope(yT[h * A_DIM:(h + 1) * A_DIM], gq_ref[...], scale_a)
        qaT_ref[0, h * A_DIM:(h + 1) * A_DIM, :] = qh.astype(BF16)
    kT = jnp.concatenate(
        [norm_rope(yT[A_Q + h * A_DIM:A_Q + (h + 1) * A_DIM], gk_ref[...], 1.0)
         for h in range(A_KV_HEADS)], axis=0)
    ka_ref[0] = kT.T.astype(BF16)
    r0 = A_Q + A_KV
    vaT_ref[0] = yT[r0:r0 + A_KV].astype(BF16)
    r0 += A_KV
    qbT_ref[0] = (yT[r0:r0 + B_QK] * (B_DIM ** -0.5 * LOG2E)).astype(BF16)
    r0 += B_QK
    vbT_ref[0] = yT[r0:r0 + B_V].astype(BF16)
    kb_ref[0] = jnp.dot(xn, wkb_ref[...], preferred_element_type=F32).astype(BF16)


def _even_in(x, g, wt, wkb, gq, gk, cosT, sinT):
    B, S, D = x.shape
    tm = min(TOKEN_TILE, S)
    rows = wt.shape[0]
    return pl.pallas_call(
        _even_in_kernel,
        grid=(B, S // tm),
        in_specs=[
            pl.BlockSpec((1, tm, D), lambda b, i: (b, i, 0)),
            _const_spec((1, D)),
            _const_spec((rows, D)),
            _const_spec((D, B_QK)),
            _const_spec((A_DIM, 1)),
            _const_spec((A_DIM, 1)),
            pl.BlockSpec((A_DIM // 2, tm), lambda b, i: (0, i)),
            pl.BlockSpec((A_DIM // 2, tm), lambda b, i: (0, i)),
        ],
        out_specs=[
            pl.BlockSpec((1, A_Q, tm), lambda b, i: (b, 0, i)),
            pl.BlockSpec((1, tm, A_KV), lambda b, i: (b, i, 0)),
            pl.BlockSpec((1, A_KV, tm), lambda b, i: (b, 0, i)),
            pl.BlockSpec((1, B_QK, tm), lambda b, i: (b, 0, i)),
            pl.BlockSpec((1, tm, B_QK), lambda b, i: (b, i, 0)),
            pl.BlockSpec((1, B_V, tm), lambda b, i: (b, 0, i)),
        ],
        out_shape=[
            jax.ShapeDtypeStruct((B, A_Q, S), BF16),
            jax.ShapeDtypeStruct((B, S, A_KV), BF16),
            jax.ShapeDtypeStruct((B, A_KV, S), BF16),
            jax.ShapeDtypeStruct((B, B_QK, S), BF16),
            jax.ShapeDtypeStruct((B, S, B_QK), BF16),
            jax.ShapeDtypeStruct((B, B_V, S), BF16),
        ],
        compiler_params=_params(2),
        name="even_in",
    )(x, g, wt, wkb, gq, gk, cosT, sinT)


ONES_ROWS = 16


def _chain_init(n, dv, tq):
    return tuple((jnp.full((1, tq), NEG_BIG, F32), jnp.zeros((dv + ONES_ROWS, tq), F32))
                 for _ in range(n))


def _chain_out(acc, dv):
    return acc[:dv] / acc[dv:dv + 1]


def _with_ones(vT_c):
    return jnp.concatenate([vT_c, jnp.ones((ONES_ROWS, vT_c.shape[1]), vT_c.dtype)], axis=0)


def _chunk_start(c, tkc):
    return c * tkc if isinstance(c, int) else pl.multiple_of(c * tkc, tkc)


def _key_chunk(S):
    return min(KEY_CHUNK, S // 2)


def _attend(n_chunks, n_chains, dv, tq, first_tile, prep, score, value, scratch):
    assert n_chunks % 2 == 0
    max_ref, s_bufs = scratch[0], (scratch[1:1 + n_chains], scratch[1 + n_chains:])

    def produce(ctx, j, nxt, slot):
        s = score(ctx, j, nxt)
        s_bufs[slot][j][...] = s
        return jnp.max(s, axis=0, keepdims=True)

    @pl.when(first_tile)
    def _():
        ctx0 = prep(0, False)
        for j in range(n_chains):
            max_ref[j] = produce(ctx0, j, False, 0)

    maxes = [max_ref[j] for j in range(n_chains)]
    chains = list(_chain_init(n_chains, dv, tq))
    for c in range(n_chunks):
        slot = c % 2
        nxt = c == n_chunks - 1
        ctx = prep(0 if nxt else c + 1, nxt)
        for j in range(n_chains):
            new_max = produce(ctx, j, nxt, 1 - slot)
            m, acc = chains[j]
            m_new = jnp.maximum(m, maxes[j])
            alpha = jnp.exp2(m - m_new)
            p = jnp.exp2(s_bufs[slot][j][...] - m_new).astype(BF16)
            acc = alpha * acc + jnp.dot(value(c, j), p, preferred_element_type=F32)
            chains[j] = (m_new, acc)
            maxes[j] = new_max
    for j in range(n_chains):
        max_ref[j] = maxes[j]
    return [_chain_out(acc, dv) for _, acc in chains]


def _attend_scratch(n_chains, tkc, tq):
    return ([pltpu.VMEM((n_chains, 1, tq), F32)]
            + [pltpu.VMEM((tkc, tq), F32) for _ in range(2 * n_chains)])


def _next_tile(n_tiles):
    return lambda i: jnp.minimum(i + 1, n_tiles - 1)


def _gqa_kernel(qT_ref, qT_next_ref, k_ref, vT_ref, o_ref, *scratch, tkc):
    g = pl.program_id(1)
    S = k_ref.shape[1]
    tq = qT_ref.shape[2]

    def padded_queries(ref):
        pads = []
        for j in range(A_GROUP):
            q = ref[0, j * A_DIM:(j + 1) * A_DIM, :]
            zero = jnp.zeros_like(q)
            pads.append(jnp.concatenate(
                [jnp.where(g == 0, q, zero), jnp.where(g == 1, q, zero)], axis=0))
        return pads

    q_pads = {False: padded_queries(qT_ref), True: padded_queries(qT_next_ref)}

    def prep(c, nxt):
        return k_ref[0, pl.ds(_chunk_start(c, tkc), tkc), :]

    def score(k_c, j, nxt):
        return jnp.dot(k_c, q_pads[nxt][j], preferred_element_type=F32)

    def value(c, j):
        return _with_ones(vT_ref[0, :, pl.ds(_chunk_start(c, tkc), tkc)])

    outs = _attend(S // tkc, A_GROUP, A_DIM, tq, pl.program_id(2) == 0,
                   prep, score, value, scratch)
    o_ref[0] = jnp.concatenate(outs, axis=0).T.astype(o_ref.dtype)


def _gqa(qaT, ka, vaT):
    B, _, S = qaT.shape
    tq = min(Q_TILE, S)
    tkc = _key_chunk(S)
    gw = A_GROUP * A_DIM
    nxt = _next_tile(S // tq)
    return pl.pallas_call(
        functools.partial(_gqa_kernel, tkc=tkc),
        grid=(B, A_KV_HEADS, S // tq),
        in_specs=[
            pl.BlockSpec((1, gw, tq), lambda b, g, i: (b, g, i)),
            pl.BlockSpec((1, gw, tq), lambda b, g, i: (b, g, nxt(i))),
            pl.BlockSpec((1, S, A_KV), lambda b, g, i: (b, 0, 0)),
            pl.BlockSpec((1, A_DIM, S), lambda b, g, i: (b, g, 0)),
        ],
        out_specs=pl.BlockSpec((1, tq, gw), lambda b, g, i: (b, i, g)),
        out_shape=jax.ShapeDtypeStruct((B, S, A_Q), BF16),
        scratch_shapes=_attend_scratch(A_GROUP, tkc, tq),
        compiler_params=_params(2, 1),
        name="gqa_attn",
    )(qaT, qaT, ka, vaT)


DIFF_HEADS_PER_STEP = 2


def _diff_kernel(slopes_ref, lq1_ref, lk1_ref, lq2_ref, lk2_ref, gsub_ref,
                 qT_ref, qT_next_ref, k_ref, vT_ref, o_ref, *scratch, tkc, lam_init):
    hp = pl.program_id(1)
    i = pl.program_id(2)
    S = k_ref.shape[1]
    tq = qT_ref.shape[2]
    pair = 2 * B_DIM
    neg_slopes = [-(slopes_ref[DIFF_HEADS_PER_STEP * hp + hh] * LOG2E)
                  for hh in range(DIFF_HEADS_PER_STEP)]

    def padded_queries(ref):
        pads = []
        for hh in range(DIFF_HEADS_PER_STEP):
            q = ref[0, hh * pair:(hh + 1) * pair, :]
            rows = lax.broadcasted_iota(jnp.int32, q.shape, 0)
            zero = jnp.zeros_like(q)
            pads += [jnp.where(rows < B_DIM, q, zero), jnp.where(rows >= B_DIM, q, zero)]
        return pads

    q_pads = {False: padded_queries(qT_ref), True: padded_queries(qT_next_ref)}
    dmat = (lax.broadcasted_iota(jnp.int32, (tkc, tq), 1)
            - lax.broadcasted_iota(jnp.int32, (tkc, tq), 0)).astype(F32)

    def prep(c, nxt):
        start = _chunk_start(c, tkc)
        q_start = (i + 1) * tq if nxt else i * tq
        dist = jnp.abs(dmat + (q_start - start).astype(F32))
        return start, [dist * ns for ns in neg_slopes]

    def score(ctx, n, nxt):
        start, biases = ctx
        hh = n // 2
        k_c = k_ref[0, pl.ds(start, tkc), hh * pair:(hh + 1) * pair]
        return jnp.dot(k_c, q_pads[nxt][n], preferred_element_type=F32) + biases[hh]

    def value(c, n):
        hh = n // 2
        return _with_ones(vT_ref[0, hh * B_VDIM:(hh + 1) * B_VDIM, pl.ds(_chunk_start(c, tkc), tkc)])

    n_chains = 2 * DIFF_HEADS_PER_STEP
    parts = _attend(S // tkc, n_chains, B_VDIM, tq, i == 0, prep, score, value, scratch)

    lam = (jnp.exp(jnp.sum(lq1_ref[...] * lk1_ref[...], axis=-1, keepdims=True))
           - jnp.exp(jnp.sum(lq2_ref[...] * lk2_ref[...], axis=-1, keepdims=True)) + lam_init)
    outs = []
    for hh in range(DIFF_HEADS_PER_STEP):
        o = parts[2 * hh] - lam * parts[2 * hh + 1]
        outs.append(_rms_cols(o, gsub_ref[...]) * (1.0 - lam_init))
    o_ref[0] = jnp.concatenate(outs, axis=0).T.astype(o_ref.dtype)


def _diff(slopes, lq1, lk1, lq2, lk2, gsub, qbT, kb, vbT, lam_init):
    B, _, S = qbT.shape
    tq = min(Q_TILE, S)
    tkc = _key_chunk(S)
    hs = DIFF_HEADS_PER_STEP
    nxt = _next_tile(S // tq)
    return pl.pallas_call(
        functools.partial(_diff_kernel, tkc=tkc, lam_init=lam_init),
        grid=(B, B_HEADS // hs, S // tq),
        in_specs=[
            pl.BlockSpec(memory_space=pltpu.SMEM),
            _const_spec((1, B_DIM)), _const_spec((1, B_DIM)),
            _const_spec((1, B_DIM)), _const_spec((1, B_DIM)),
            _const_spec((B_VDIM, 1)),
            pl.BlockSpec((1, hs * 2 * B_DIM, tq), lambda b, h, i: (b, h, i)),
            pl.BlockSpec((1, hs * 2 * B_DIM, tq), lambda b, h, i: (b, h, nxt(i))),
            pl.BlockSpec((1, S, hs * 2 * B_DIM), lambda b, h, i: (b, 0, h)),
            pl.BlockSpec((1, hs * B_VDIM, S), lambda b, h, i: (b, h, 0)),
        ],
        out_specs=pl.BlockSpec((1, tq, hs * B_VDIM), lambda b, h, i: (b, i, h)),
        out_shape=jax.ShapeDtypeStruct((B, S, B_V), BF16),
        scratch_shapes=_attend_scratch(2 * hs, tkc, tq),
        compiler_params=_params(2, 1),
        name="diff_attn",
    )(slopes, lq1, lk1, lq2, lk2, gsub, qbT, qbT, kb, vbT)


MLA_HEADS_PER_STEP = 4


def _mla_kernel(qT_ref, qT_next_ref, k_ref, vT_ref, o_ref, *scratch, tkc):
    S = k_ref.shape[1]
    tq = qT_ref.shape[2]
    nh = MLA_HEADS_PER_STEP
    q_refs = {False: qT_ref, True: qT_next_ref}

    def prep(c, nxt):
        return _chunk_start(c, tkc)

    def score(start, j, nxt):
        k_c = k_ref[0, pl.ds(start, tkc), j * C_PAD:(j + 1) * C_PAD]
        q_pad = q_refs[nxt][0, j * C_PAD:(j + 1) * C_PAD, :]
        return jnp.dot(k_c, q_pad, preferred_element_type=F32)

    def value(c, j):
        return _with_ones(vT_ref[0, j * C_VDIM:(j + 1) * C_VDIM, pl.ds(_chunk_start(c, tkc), tkc)])

    outs = _attend(S // tkc, nh, C_VDIM, tq, pl.program_id(2) == 0, prep, score, value, scratch)
    o_ref[0] = jnp.concatenate(outs, axis=0).T.astype(o_ref.dtype)


def _mla(qT, k, vT):
    B, _, S = qT.shape
    tq = min(Q_TILE, S)
    tkc = _key_chunk(S)
    hp = MLA_HEADS_PER_STEP
    nxt = _next_tile(S // tq)
    return pl.pallas_call(
        functools.partial(_mla_kernel, tkc=tkc),
        grid=(B, C_HEADS // hp, S // tq),
        in_specs=[
            pl.BlockSpec((1, hp * C_PAD, tq), lambda b, h, i: (b, h, i)),
            pl.BlockSpec((1, hp * C_PAD, tq), lambda b, h, i: (b, h, nxt(i))),
            pl.BlockSpec((1, S, hp * C_PAD), lambda b, h, i: (b, 0, h)),
            pl.BlockSpec((1, hp * C_VDIM, S), lambda b, h, i: (b, h, 0)),
        ],
        out_specs=pl.BlockSpec((1, tq, hp * C_VDIM), lambda b, h, i: (b, i, h)),
        out_shape=jax.ShapeDtypeStruct((B, S, C_HEADS * C_VDIM), BF16),
        scratch_shapes=_attend_scratch(hp, tkc, tq),
        compiler_params=_params(2, 1),
        name="mla_attn",
    )(qT, qT, k, vT)


def _odd_in_kernel(x_ref, g_ref, win_ref, gq_ref, gkv_ref, wuqT_ref, wkn_ref, wvT_ref,
                   cosq_ref, sinq_ref, cosk_ref, sink_ref, qT_ref, k_ref, vT_ref):
    hr = C_ROPE // 2
    xn = _rms_rows(x_ref[0], g_ref[...]).astype(BF16)
    a = jnp.dot(xn, win_ref[...], preferred_element_type=F32)
    cqn = _rms_rows(a[:, :C_Q_RANK], gq_ref[...]).astype(BF16)
    ckvn = _rms_rows(a[:, C_Q_RANK:C_Q_RANK + C_KV_RANK], gkv_ref[...]).astype(BF16)

    qT = lax.dot_general(wuqT_ref[...], cqn, _NT, preferred_element_type=F32)
    qT = qT * ((C_NOPE + C_ROPE) ** -0.5 * LOG2E)
    cq, sq = cosq_ref[...], sinq_ref[...]
    for h in range(C_HEADS):
        r = h * C_PAD
        e = qT[r + C_NOPE:r + C_NOPE + hr]
        o = qT[r + C_NOPE + hr:r + C_NOPE + C_ROPE]
        head = jnp.concatenate(
            [qT[r:r + C_NOPE], e * cq - o * sq, e * sq + o * cq, qT[r + C_NOPE + C_ROPE:r + C_PAD]],
            axis=0)
        qT_ref[0, r:r + C_PAD, :] = head.astype(BF16)

    kblk = a[:, C_Q_RANK + C_KV_RANK:]
    t = kblk * cosk_ref[...] + kblk * sink_ref[...]
    lane = lax.broadcasted_iota(jnp.int32, t.shape, 1)
    kr = jnp.where(lane < C_ROPE, t + pltpu.roll(t, C_PAD - C_ROPE, axis=1), 0.0)
    kr = pltpu.roll(kr, C_NOPE, axis=1)

    kn = jnp.dot(ckvn, wkn_ref[...], preferred_element_type=F32)
    for h in range(C_HEADS):
        k_ref[0, :, h * C_PAD:(h + 1) * C_PAD] = (kn[:, h * C_PAD:(h + 1) * C_PAD] + kr).astype(BF16)

    vT_ref[0] = lax.dot_general(wvT_ref[...], ckvn, _NT, preferred_element_type=F32).astype(BF16)


def _odd_in(x, g, win, gq, gkv, wuqT, wkn, wvT, cosq, sinq, cosk, sink):
    B, S, D = x.shape
    tm = min(TOKEN_TILE, S)
    hr = C_ROPE // 2
    return pl.pallas_call(
        _odd_in_kernel,
        grid=(B, S // tm),
        in_specs=[
            pl.BlockSpec((1, tm, D), lambda b, i: (b, i, 0)),
            _const_spec((1, D)),
            _const_spec(win.shape),
            _const_spec((1, C_Q_RANK)),
            _const_spec((1, C_KV_RANK)),
            _const_spec(wuqT.shape),
            _const_spec(wkn.shape),
            _const_spec(wvT.shape),
            pl.BlockSpec((hr, tm), lambda b, i: (0, i)),
            pl.BlockSpec((hr, tm), lambda b, i: (0, i)),
            pl.BlockSpec((tm, C_PAD), lambda b, i: (i, 0)),
            pl.BlockSpec((tm, C_PAD), lambda b, i: (i, 0)),
        ],
        out_specs=[
            pl.BlockSpec((1, C_HEADS * C_PAD, tm), lambda b, i: (b, 0, i)),
            pl.BlockSpec((1, tm, C_HEADS * C_PAD), lambda b, i: (b, i, 0)),
            pl.BlockSpec((1, C_HEADS * C_VDIM, tm), lambda b, i: (b, 0, i)),
        ],
        out_shape=[
            jax.ShapeDtypeStruct((B, C_HEADS * C_PAD, S), BF16),
            jax.ShapeDtypeStruct((B, S, C_HEADS * C_PAD), BF16),
            jax.ShapeDtypeStruct((B, C_HEADS * C_VDIM, S), BF16),
        ],
        compiler_params=_params(2),
        name="odd_in",
    )(x, g, win, gq, gkv, wuqT, wkn, wvT, cosq, sinq, cosk, sink)


def _memkv_kernel(mem_ref, g_ref, w_ref, kv_ref):
    mn = _rms_rows(mem_ref[0], g_ref[...]).astype(BF16)
    kv_ref[0] = jnp.dot(mn, w_ref[...], preferred_element_type=F32).astype(BF16)


def _memkv(mem, g, w):
    B, M, D = mem.shape
    N = w.shape[1]
    return pl.pallas_call(
        _memkv_kernel,
        grid=(B,),
        in_specs=[pl.BlockSpec((1, M, D), lambda b: (b, 0, 0)), _const_spec((1, D)), _const_spec((D, N))],
        out_specs=pl.BlockSpec((1, M, N), lambda b: (b, 0, 0)),
        out_shape=jax.ShapeDtypeStruct((B, M, N), BF16),
        compiler_params=_params(1),
        name="mem_kv",
    )(mem, g, w)


def _post_mix_kernel(*refs, n_mix):
    x_ref = refs[0]
    o_refs = refs[1:1 + n_mix]
    w_refs = refs[1 + n_mix:1 + 2 * n_mix]
    gc_ref, wq_ref, kv_ref, wo_ref, out_ref = refs[1 + 2 * n_mix:]
    x = x_ref[0]
    for o_ref, w_ref in zip(o_refs, w_refs):
        x = x + jnp.dot(o_ref[0], w_ref[...], preferred_element_type=F32)

    hc = _rms_rows(x, gc_ref[...]).astype(BF16)
    q = (jnp.dot(hc, wq_ref[...], preferred_element_type=F32) * (X_DIM ** -0.5)).astype(BF16)
    heads = []
    for h in range(X_HEADS):
        k_h = kv_ref[0, :, h * X_DIM:(h + 1) * X_DIM]
        v_h = kv_ref[0, :, D_MODEL + h * X_DIM:D_MODEL + (h + 1) * X_DIM]
        s = lax.dot_general(q[:, h * X_DIM:(h + 1) * X_DIM], k_h, _NT, preferred_element_type=F32)
        p = jnp.exp(s - jnp.max(s, axis=-1, keepdims=True))
        l = jnp.sum(p, axis=-1, keepdims=True)
        heads.append((jnp.dot(p.astype(BF16), v_h, preferred_element_type=F32) / l).astype(BF16))
    o = jnp.concatenate(heads, axis=-1)
    out_ref[0] = x + jnp.dot(o, wo_ref[...], preferred_element_type=F32)


def _post_mix(x, mixes, weights, gc, wq, kv, wo):
    B, S, D = x.shape
    tm = min(WIDE_TOKEN_TILE, S)
    n = len(mixes)
    M = kv.shape[1]
    tok = lambda b, i: (b, i, 0)
    return pl.pallas_call(
        functools.partial(_post_mix_kernel, n_mix=n),
        grid=(B, S // tm),
        in_specs=([pl.BlockSpec((1, tm, D), tok)]
                  + [pl.BlockSpec((1, tm, m.shape[2]), tok) for m in mixes]
                  + [_const_spec(w.shape) for w in weights]
                  + [_const_spec((1, D)), _const_spec((D, D)),
                     pl.BlockSpec((1, M, 2 * D), lambda b, i: (b, 0, 0)),
                     _const_spec((D, D))]),
        out_specs=pl.BlockSpec((1, tm, D), tok),
        out_shape=jax.ShapeDtypeStruct((B, S, D), F32),
        compiler_params=_params(2),
        name="post_mix",
    )(x, *mixes, *weights, gc, wq, kv, wo)


def _ffn_kernel(x_ref, g_ref, wgu_ref, wd_ref, gf_ref, out_ref, *, final_norm):
    x = x_ref[0]
    xn = _rms_rows(x, g_ref[...]).astype(BF16)
    acc = x
    for c in range(D_FF // FF_CHUNK):
        lo = c * FF_CHUNK
        gate = jnp.dot(xn, wgu_ref[:, lo:lo + FF_CHUNK], preferred_element_type=F32)
        up = jnp.dot(xn, wgu_ref[:, D_FF + lo:D_FF + lo + FF_CHUNK], preferred_element_type=F32)
        hidden = (jax.nn.silu(gate) * up).astype(BF16)
        acc = acc + jnp.dot(hidden, wd_ref[lo:lo + FF_CHUNK, :], preferred_element_type=F32)
    if final_norm:
        acc = _rms_rows(acc, gf_ref[...])
    out_ref[0] = acc


def _ffn(x, g, wgu, wd, gf, final_norm):
    B, S, D = x.shape
    tm = min(WIDE_TOKEN_TILE, S)
    tok = lambda b, i: (b, i, 0)
    return pl.pallas_call(
        functools.partial(_ffn_kernel, final_norm=final_norm),
        grid=(B, S // tm),
        in_specs=[pl.BlockSpec((1, tm, D), tok), _const_spec((1, D)),
                  _const_spec(wgu.shape), _const_spec(wd.shape), _const_spec((1, D))],
        out_specs=pl.BlockSpec((1, tm, D), tok),
        out_shape=jax.ShapeDtypeStruct((B, S, D), F32),
        compiler_params=_params(2),
        name="ffn",
    )(x, g, wgu, wd, gf)


def _rope_freqs(n_pairs):
    return ROPE_THETA ** (-jnp.arange(n_pairs, dtype=F32) / n_pairs)


def _axial_angles(S):
    rows = S // GRID_W
    r = jnp.repeat(jnp.arange(rows, dtype=F32), GRID_W)
    c = jnp.tile(jnp.arange(GRID_W, dtype=F32), rows)
    f = _rope_freqs(A_DIM // 4)
    return jnp.concatenate([r[:, None] * f, c[:, None] * f], axis=-1)


def _linear_angles(S, dim):
    t = jnp.arange(S, dtype=F32)
    return t[:, None] * _rope_freqs(dim // 2)


def _deinterleave(n):
    return np.concatenate([np.arange(0, n, 2), np.arange(1, n, 2)])


def _prep_even(w_in, gq, gk):
    perm = _deinterleave(A_DIM)
    o = 0
    w_qa = w_in[:, o:o + A_Q].reshape(D_MODEL, A_HEADS, A_DIM)[:, :, perm].reshape(D_MODEL, A_Q)
    o += A_Q
    w_ka = w_in[:, o:o + A_KV].reshape(D_MODEL, A_KV_HEADS, A_DIM)[:, :, perm].reshape(D_MODEL, A_KV)
    o += A_KV
    w_va = w_in[:, o:o + A_KV]
    o += A_KV
    w_qb = w_in[:, o:o + B_QK]
    o += B_QK
    w_kb = w_in[:, o:o + B_QK]
    o += B_QK
    w_vb = w_in[:, o:o + B_V]
    wt = jnp.concatenate([w_qa, w_ka, w_va, w_qb, w_vb], axis=1).T.astype(BF16)
    return wt, w_kb.astype(BF16), gq[perm].reshape(A_DIM, 1), gk[perm].reshape(A_DIM, 1)


def _prep_odd(w_in, w_uq, w_ukv):
    hr = C_ROPE // 2
    perm = _deinterleave(C_ROPE)
    w_kr = w_in[:, C_Q_RANK + C_KV_RANK:][:, perm]
    w_kr_rot = jnp.concatenate([-w_kr[:, hr:], w_kr[:, :hr]], axis=1)
    win = jnp.concatenate(
        [w_in[:, :C_Q_RANK + C_KV_RANK], w_kr, w_kr_rot,
         jnp.zeros((D_MODEL, C_PAD - 2 * C_ROPE), F32)], axis=1).astype(BF16)
    wq = w_uq.reshape(C_Q_RANK, C_HEADS, C_NOPE + C_ROPE)
    wq = jnp.concatenate(
        [wq[:, :, :C_NOPE], wq[:, :, C_NOPE:][:, :, perm],
         jnp.zeros((C_Q_RANK, C_HEADS, C_PAD - C_NOPE - C_ROPE), F32)], axis=2)
    wuqT = wq.reshape(C_Q_RANK, C_HEADS * C_PAD).T.astype(BF16)
    wkv = w_ukv.reshape(C_KV_RANK, C_HEADS, C_NOPE + C_VDIM)
    wkn = jnp.concatenate(
        [wkv[:, :, :C_NOPE], jnp.zeros((C_KV_RANK, C_HEADS, C_PAD - C_NOPE), F32)], axis=2)
    wkn = wkn.reshape(C_KV_RANK, C_HEADS * C_PAD).astype(BF16)
    wvT = wkv[:, :, C_NOPE:].reshape(C_KV_RANK, C_HEADS * C_VDIM).T.astype(BF16)
    return win, wuqT, wkn, wvT


def _trunk(x, mem, p):
    B, S, D = x.shape
    depth = p['norm_mix'].shape[0]
    row = lambda v: v.reshape(1, -1)

    ang_a = _axial_angles(S)
    cos_a, sin_a = jnp.cos(ang_a).T, jnp.sin(ang_a).T
    ang_l = _linear_angles(S, C_ROPE)
    cos_l, sin_l = jnp.cos(ang_l), jnp.sin(ang_l)
    zpad = jnp.zeros((S, C_PAD - 2 * C_ROPE), F32)
    zrope = jnp.zeros((S, C_ROPE), F32)
    cos_k = jnp.concatenate([cos_l, cos_l, zrope, zpad], axis=1)
    sin_k = jnp.concatenate([zrope, sin_l, sin_l, zpad], axis=1)
    slopes = jnp.asarray(2.0 ** (-8.0 * np.arange(1, B_HEADS + 1) / B_HEADS), dtype=F32)

    for layer in range(depth):
        if layer % 2 == 0:
            e = layer // 2
            wt, wkb, gq, gk = _prep_even(p['e_w_in'][e], p['e_q_norm'][e], p['e_k_norm'][e])
            qaT, ka, vaT, qbT, kb, vbT = _even_in(
                x, row(p['norm_mix'][layer]), wt, wkb, gq, gk, cos_a, sin_a)
            oa = _gqa(qaT, ka, vaT)
            lam_init = 0.8 - 0.6 * math.exp(-0.3 * layer)
            ob = _diff(slopes, row(p['e_lam_q1'][e]), row(p['e_lam_k1'][e]),
                       row(p['e_lam_q2'][e]), row(p['e_lam_k2'][e]),
                       p['e_subln'][e].reshape(B_VDIM, 1), qbT, kb, vbT, lam_init)
            w_out = p['e_w_out'][e].astype(BF16)
            mixes, weights = [oa, ob], [w_out[:A_Q], w_out[A_Q:]]
        else:
            o = layer // 2
            win, wuqT, wkn, wvT = _prep_odd(p['o_w_in'][o], p['o_w_uq'][o], p['o_w_ukv'][o])
            qT, k, vT = _odd_in(x, row(p['norm_mix'][layer]), win, row(p['o_q_norm'][o]),
                                row(p['o_kv_norm'][o]), wuqT, wkn, wvT,
                                cos_l.T, sin_l.T, cos_k, sin_k)
            mixes, weights = [_mla(qT, k, vT)], [p['o_w_out'][o].astype(BF16)]
        kv = _memkv(mem, row(p['norm_mem'][layer]), p['w_ckv'][layer].astype(BF16))
        x = _post_mix(x, mixes, weights, row(p['norm_cross'][layer]),
                      p['w_cq'][layer].astype(BF16), kv, p['w_co'][layer].astype(BF16))
        x = _ffn(x, row(p['norm_ffn'][layer]), p['w_gu'][layer].astype(BF16),
                 p['w_down'][layer].astype(BF16), row(p['final_norm']),
                 final_norm=(layer == depth - 1))
    return x


def kernel(x_prompt, x_sample, mem_prompt, mem_sample, norm_mix, e_w_in, e_q_norm, e_k_norm, e_lam_q1, e_lam_k1, e_lam_q2, e_lam_k2, e_subln, e_w_out, o_w_in, o_q_norm, o_kv_norm, o_w_uq, o_w_ukv, o_w_out, norm_cross, norm_mem, w_cq, w_ckv, w_co, norm_ffn, w_gu, w_down, final_norm):
    p = dict(norm_mix=norm_mix, e_w_in=e_w_in, e_q_norm=e_q_norm, e_k_norm=e_k_norm,
             e_lam_q1=e_lam_q1, e_lam_k1=e_lam_k1, e_lam_q2=e_lam_q2, e_lam_k2=e_lam_k2,
             e_subln=e_subln, e_w_out=e_w_out, o_w_in=o_w_in, o_q_norm=o_q_norm,
             o_kv_norm=o_kv_norm, o_w_uq=o_w_uq, o_w_ukv=o_w_ukv, o_w_out=o_w_out,
             norm_cross=norm_cross, norm_mem=norm_mem, w_cq=w_cq, w_ckv=w_ckv, w_co=w_co,
             norm_ffn=norm_ffn, w_gu=w_gu, w_down=w_down, final_norm=final_norm)
    return (_trunk(x_prompt, mem_prompt, p), _trunk(x_sample, mem_sample, p))
```

```python
import functools
import math

import jax
import jax.numpy as jnp
import numpy as np
from jax import lax
from jax.experimental import pallas as pl
from jax.experimental.pallas import tpu as pltpu

F32 = jnp.float32
BF16 = jnp.bfloat16

D_MODEL = 1024
GRID_W = 64
EPS = 1e-6
ROPE_THETA = 10000.0
A_HEADS, A_KV_HEADS, A_DIM = 8, 2, 64
A_GROUP = A_HEADS // A_KV_HEADS
B_HEADS, B_DIM = 4, 64
B_VDIM = 2 * B_DIM
A_Q = A_HEADS * A_DIM
A_KV = A_KV_HEADS * A_DIM
B_QK = B_HEADS * 2 * B_DIM
B_V = B_HEADS * B_VDIM
C_HEADS, C_Q_RANK, C_KV_RANK, C_NOPE, C_ROPE, C_VDIM = 16, 384, 256, 64, 32, 64
C_PAD = 128
X_HEADS = 4
X_DIM = D_MODEL // X_HEADS
D_FF = ((-(-8 * D_MODEL // 3) + 255) // 256) * 256
FF_CHUNK = 256
NEG_BIG = -1e30
LOG2E = math.log2(math.e)

TOKEN_TILE = 512
WIDE_TOKEN_TILE = 512
Q_TILE = 256
KEY_CHUNK = 512
VMEM_LIMIT = 48 * 1024 * 1024

_NT = (((1,), (1,)), ((), ()))


def _params(n_parallel, n_arbitrary=0):
    return pltpu.CompilerParams(
        dimension_semantics=("parallel",) * n_parallel + ("arbitrary",) * n_arbitrary,
        vmem_limit_bytes=VMEM_LIMIT)


def _rms_rows(x, g):
    ms = jnp.mean(x * x, axis=-1, keepdims=True)
    return (x * lax.rsqrt(ms + EPS)) * g


def _rms_cols(x, g):
    ms = jnp.mean(x * x, axis=0, keepdims=True)
    return (x * lax.rsqrt(ms + EPS)) * g


def _const_spec(shape):
    nd = len(shape)
    return pl.BlockSpec(shape, lambda *_: (0,) * nd)


ONES_ROWS = 16


def _store_values(ref, vT, n_heads, dv):
    ext = dv + ONES_ROWS
    ones = jnp.ones((ONES_ROWS, vT.shape[1]), BF16)
    for h in range(n_heads):
        ref[0, h * ext:h * ext + dv, :] = vT[h * dv:(h + 1) * dv].astype(BF16)
        ref[0, h * ext + dv:(h + 1) * ext, :] = ones


def _value_rows(h, dv):
    ext = dv + ONES_ROWS
    return slice(h * ext, (h + 1) * ext)


def _even_in_kernel(x_ref, g_ref, wt_ref, wkb_ref, gq_ref, gk_ref, cos_ref, sin_ref,
                    qaT_ref, ka_ref, vaT_ref, qbT_ref, kb_ref, vbT_ref):
    half = A_DIM // 2
    xn = _rms_rows(x_ref[0], g_ref[...]).astype(BF16)
    yT = lax.dot_general(wt_ref[...], xn, _NT, preferred_element_type=F32)
    cos = cos_ref[...]
    sin = sin_ref[...]

    def norm_rope(xh, g, scale):
        y = _rms_cols(xh, g)
        e, o = y[:half], y[half:]
        return jnp.concatenate([e * cos - o * sin, e * sin + o * cos], axis=0) * scale

    scale_a = A_DIM ** -0.5 * LOG2E
    for h in range(A_HEADS):
        qh = norm_rope(yT[h * A_DIM:(h + 1) * A_DIM], gq_ref[...], scale_a)
        qaT_ref[0, h * A_DIM:(h + 1) * A_DIM, :] = qh.astype(BF16)
    kT = jnp.concatenate(
        [norm_rope(yT[A_Q + h * A_DIM:A_Q + (h + 1) * A_DIM], gk_ref[...], 1.0)
         for h in range(A_KV_HEADS)], axis=0)
    ka_ref[0] = kT.T.astype(BF16)
    r0 = A_Q + A_KV
    _store_values(vaT_ref, yT[r0:r0 + A_KV], A_KV_HEADS, A_DIM)
    r0 += A_KV
    qbT_ref[0] = (yT[r0:r0 + B_QK] * (B_DIM ** -0.5 * LOG2E)).astype(BF16)
    r0 += B_QK
    _store_values(vbT_ref, yT[r0:r0 + B_V], B_HEADS, B_VDIM)
    kb_ref[0] = jnp.dot(xn, wkb_ref[...], preferred_element_type=F32).astype(BF16)


def _even_in(x, g, wt, wkb, gq, gk, cosT, sinT):
    B, S, D = x.shape
    tm = min(TOKEN_TILE, S)
    rows = wt.shape[0]
    va_rows = A_KV_HEADS * (A_DIM + ONES_ROWS)
    vb_rows = B_HEADS * (B_VDIM + ONES_ROWS)
    return pl.pallas_call(
        _even_in_kernel,
        grid=(B, S // tm),
        in_specs=[
            pl.BlockSpec((1, tm, D), lambda b, i: (b, i, 0)),
            _const_spec((1, D)),
            _const_spec((rows, D)),
            _const_spec((D, B_QK)),
            _const_spec((A_DIM, 1)),
            _const_spec((A_DIM, 1)),
            pl.BlockSpec((A_DIM // 2, tm), lambda b, i: (0, i)),
            pl.BlockSpec((A_DIM // 2, tm), lambda b, i: (0, i)),
        ],
        out_specs=[
            pl.BlockSpec((1, A_Q, tm), lambda b, i: (b, 0, i)),
            pl.BlockSpec((1, tm, A_KV), lambda b, i: (b, i, 0)),
            pl.BlockSpec((1, va_rows, tm), lambda b, i: (b, 0, i)),
            pl.BlockSpec((1, B_QK, tm), lambda b, i: (b, 0, i)),
            pl.BlockSpec((1, tm, B_QK), lambda b, i: (b, i, 0)),
            pl.BlockSpec((1, vb_rows, tm), lambda b, i: (b, 0, i)),
        ],
        out_shape=[
            jax.ShapeDtypeStruct((B, A_Q, S), BF16),
            jax.ShapeDtypeStruct((B, S, A_KV), BF16),
            jax.ShapeDtypeStruct((B, va_rows, S), BF16),
            jax.ShapeDtypeStruct((B, B_QK, S), BF16),
            jax.ShapeDtypeStruct((B, S, B_QK), BF16),
            jax.ShapeDtypeStruct((B, vb_rows, S), BF16),
        ],
        compiler_params=_params(2),
        name="even_in",
    )(x, g, wt, wkb, gq, gk, cosT, sinT)


def _chain_init(n, dv, tq):
    return tuple((jnp.full((1, tq), NEG_BIG, F32), jnp.zeros((dv + ONES_ROWS, tq), F32))
                 for _ in range(n))


def _chain_out(acc, dv):
    return acc[:dv] / acc[dv:dv + 1]


def _chunk_start(c, tkc):
    return c * tkc if isinstance(c, int) else pl.multiple_of(c * tkc, tkc)


def _key_chunk(S):
    return min(KEY_CHUNK, S // 2)


def _attend(n_chunks, n_chains, dv, tq, first_tile, prep, score, value, scratch):
    assert n_chunks % 2 == 0
    max_ref, s_bufs = scratch[0], (scratch[1:1 + n_chains], scratch[1 + n_chains:])

    def produce(ctx, j, nxt, slot):
        s = score(ctx, j, nxt)
        s_bufs[slot][j][...] = s
        return jnp.max(s, axis=0, keepdims=True)

    @pl.when(first_tile)
    def _():
        ctx0 = prep(0, False)
        for j in range(n_chains):
            max_ref[j] = produce(ctx0, j, False, 0)

    maxes = [max_ref[j] for j in range(n_chains)]
    chains = list(_chain_init(n_chains, dv, tq))
    for c in range(n_chunks):
        slot = c % 2
        nxt = c == n_chunks - 1
        ctx = prep(0 if nxt else c + 1, nxt)
        for j in range(n_chains):
            new_max = produce(ctx, j, nxt, 1 - slot)
            m, acc = chains[j]
            m_new = jnp.maximum(m, maxes[j])
            alpha = jnp.exp2(m - m_new)
            p = jnp.exp2(s_bufs[slot][j][...] - m_new).astype(BF16)
            acc = alpha * acc + jnp.dot(value(c, j), p, preferred_element_type=F32)
            chains[j] = (m_new, acc)
            maxes[j] = new_max
    for j in range(n_chains):
        max_ref[j] = maxes[j]
    return [_chain_out(acc, dv) for _, acc in chains]


def _attend_scratch(n_chains, tkc, tq):
    return ([pltpu.VMEM((n_chains, 1, tq), F32)]
            + [pltpu.VMEM((tkc, tq), F32) for _ in range(2 * n_chains)])


def _next_tile(n_tiles):
    return lambda i: jnp.minimum(i + 1, n_tiles - 1)


def _gqa_kernel(qT_ref, qT_next_ref, k_ref, vT_ref, o_ref, *scratch, tkc):
    S = k_ref.shape[1]
    tq = qT_ref.shape[2]

    def padded_queries(ref):
        pads = []
        for h in range(A_HEADS):
            q = ref[0, h * A_DIM:(h + 1) * A_DIM, :]
            zero = jnp.zeros_like(q)
            pads.append(jnp.concatenate([q, zero] if h < A_GROUP else [zero, q], axis=0))
        return pads

    q_pads = {False: padded_queries(qT_ref), True: padded_queries(qT_next_ref)}

    def prep(c, nxt):
        return k_ref[0, pl.ds(_chunk_start(c, tkc), tkc), :]

    def score(k_c, h, nxt):
        return jnp.dot(k_c, q_pads[nxt][h], preferred_element_type=F32)

    def value(c, h):
        g = h // A_GROUP
        return vT_ref[0, _value_rows(g, A_DIM), pl.ds(_chunk_start(c, tkc), tkc)]

    outs = _attend(S // tkc, A_HEADS, A_DIM, tq, pl.program_id(1) == 0,
                   prep, score, value, scratch)
    o_ref[0] = jnp.concatenate(outs, axis=0).T.astype(o_ref.dtype)


def _gqa(qaT, ka, vaT):
    B, _, S = qaT.shape
    tq = min(Q_TILE, S)
    tkc = _key_chunk(S)
    nxt = _next_tile(S // tq)
    return pl.pallas_call(
        functools.partial(_gqa_kernel, tkc=tkc),
        grid=(B, S // tq),
        in_specs=[
            pl.BlockSpec((1, A_Q, tq), lambda b, i: (b, 0, i)),
            pl.BlockSpec((1, A_Q, tq), lambda b, i: (b, 0, nxt(i))),
            pl.BlockSpec((1, S, A_KV), lambda b, i: (b, 0, 0)),
            pl.BlockSpec((1, A_KV_HEADS * (A_DIM + ONES_ROWS), S), lambda b, i: (b, 0, 0)),
        ],
        out_specs=pl.BlockSpec((1, tq, A_Q), lambda b, i: (b, i, 0)),
        out_shape=jax.ShapeDtypeStruct((B, S, A_Q), BF16),
        scratch_shapes=_attend_scratch(A_HEADS, tkc, tq),
        compiler_params=_params(1, 1),
        name="gqa_attn",
    )(qaT, qaT, ka, vaT)


DIFF_HEADS_PER_STEP = 4


def _diff_kernel(slopes_ref, lq1_ref, lk1_ref, lq2_ref, lk2_ref, gsub_ref,
                 qT_ref, qT_next_ref, k_ref, vT_ref, o_ref, *scratch, tkc, lam_init):
    hp = pl.program_id(1)
    i = pl.program_id(2)
    S = k_ref.shape[1]
    tq = qT_ref.shape[2]
    pair = 2 * B_DIM
    neg_slopes = [-(slopes_ref[DIFF_HEADS_PER_STEP * hp + hh] * LOG2E)
                  for hh in range(DIFF_HEADS_PER_STEP)]

    def padded_queries(ref):
        pads = []
        for hh in range(DIFF_HEADS_PER_STEP):
            q = ref[0, hh * pair:(hh + 1) * pair, :]
            rows = lax.broadcasted_iota(jnp.int32, q.shape, 0)
            zero = jnp.zeros_like(q)
            pads += [jnp.where(rows < B_DIM, q, zero), jnp.where(rows >= B_DIM, q, zero)]
        return pads

    q_pads = {False: padded_queries(qT_ref), True: padded_queries(qT_next_ref)}
    dmat = (lax.broadcasted_iota(jnp.int32, (tkc, tq), 1)
            - lax.broadcasted_iota(jnp.int32, (tkc, tq), 0)).astype(F32)

    def prep(c, nxt):
        start = _chunk_start(c, tkc)
        q_start = (i + 1) * tq if nxt else i * tq
        dist = jnp.abs(dmat + (q_start - start).astype(F32))
        return start, [dist * ns for ns in neg_slopes]

    def score(ctx, n, nxt):
        start, biases = ctx
        hh = n // 2
        k_c = k_ref[0, pl.ds(start, tkc), hh * pair:(hh + 1) * pair]
        return jnp.dot(k_c, q_pads[nxt][n], preferred_element_type=F32) + biases[hh]

    def value(c, n):
        hh = n // 2
        return vT_ref[0, _value_rows(hh, B_VDIM), pl.ds(_chunk_start(c, tkc), tkc)]

    n_chains = 2 * DIFF_HEADS_PER_STEP
    parts = _attend(S // tkc, n_chains, B_VDIM, tq, i == 0, prep, score, value, scratch)

    lam = (jnp.exp(jnp.sum(lq1_ref[...] * lk1_ref[...], axis=-1, keepdims=True))
           - jnp.exp(jnp.sum(lq2_ref[...] * lk2_ref[...], axis=-1, keepdims=True)) + lam_init)
    outs = []
    for hh in range(DIFF_HEADS_PER_STEP):
        o = parts[2 * hh] - lam * parts[2 * hh + 1]
        outs.append(_rms_cols(o, gsub_ref[...]) * (1.0 - lam_init))
    o_ref[0] = jnp.concatenate(outs, axis=0).T.astype(o_ref.dtype)


def _diff(slopes, lq1, lk1, lq2, lk2, gsub, qbT, kb, vbT, lam_init):
    B, _, S = qbT.shape
    tq = min(Q_TILE, S)
    tkc = _key_chunk(S)
    hs = DIFF_HEADS_PER_STEP
    nxt = _next_tile(S // tq)
    return pl.pallas_call(
        functools.partial(_diff_kernel, tkc=tkc, lam_init=lam_init),
        grid=(B, B_HEADS // hs, S // tq),
        in_specs=[
            pl.BlockSpec(memory_space=pltpu.SMEM),
            _const_spec((1, B_DIM)), _const_spec((1, B_DIM)),
            _const_spec((1, B_DIM)), _const_spec((1, B_DIM)),
            _const_spec((B_VDIM, 1)),
            pl.BlockSpec((1, hs * 2 * B_DIM, tq), lambda b, h, i: (b, h, i)),
            pl.BlockSpec((1, hs * 2 * B_DIM, tq), lambda b, h, i: (b, h, nxt(i))),
            pl.BlockSpec((1, S, hs * 2 * B_DIM), lambda b, h, i: (b, 0, h)),
            pl.BlockSpec((1, hs * (B_VDIM + ONES_ROWS), S), lambda b, h, i: (b, h, 0)),
        ],
        out_specs=pl.BlockSpec((1, tq, hs * B_VDIM), lambda b, h, i: (b, i, h)),
        out_shape=jax.ShapeDtypeStruct((B, S, B_V), BF16),
        scratch_shapes=_attend_scratch(2 * hs, tkc, tq),
        compiler_params=_params(2, 1),
        name="diff_attn",
    )(slopes, lq1, lk1, lq2, lk2, gsub, qbT, qbT, kb, vbT)


MLA_HEADS_PER_STEP = 8


def _mla_kernel(qT_ref, qT_next_ref, k_ref, vT_ref, o_ref, *scratch, tkc):
    S = k_ref.shape[1]
    tq = qT_ref.shape[2]
    nh = MLA_HEADS_PER_STEP
    q_refs = {False: qT_ref, True: qT_next_ref}

    def prep(c, nxt):
        return _chunk_start(c, tkc)

    def score(start, j, nxt):
        k_c = k_ref[0, pl.ds(start, tkc), j * C_PAD:(j + 1) * C_PAD]
        q_pad = q_refs[nxt][0, j * C_PAD:(j + 1) * C_PAD, :]
        return jnp.dot(k_c, q_pad, preferred_element_type=F32)

    def value(c, j):
        return vT_ref[0, _value_rows(j, C_VDIM), pl.ds(_chunk_start(c, tkc), tkc)]

    outs = _attend(S // tkc, nh, C_VDIM, tq, pl.program_id(2) == 0, prep, score, value, scratch)
    o_ref[0] = jnp.concatenate(outs, axis=0).T.astype(o_ref.dtype)


def _mla(qT, k, vT):
    B, _, S = qT.shape
    tq = min(Q_TILE, S)
    tkc = _key_chunk(S)
    hp = MLA_HEADS_PER_STEP
    nxt = _next_tile(S // tq)
    return pl.pallas_call(
        functools.partial(_mla_kernel, tkc=tkc),
        grid=(B, C_HEADS // hp, S // tq),
        in_specs=[
            pl.BlockSpec((1, hp * C_PAD, tq), lambda b, h, i: (b, h, i)),
            pl.BlockSpec((1, hp * C_PAD, tq), lambda b, h, i: (b, h, nxt(i))),
            pl.BlockSpec((1, S, hp * C_PAD), lambda b, h, i: (b, 0, h)),
            pl.BlockSpec((1, hp * (C_VDIM + ONES_ROWS), S), lambda b, h, i: (b, h, 0)),
        ],
        out_specs=pl.BlockSpec((1, tq, hp * C_VDIM), lambda b, h, i: (b, i, h)),
        out_shape=jax.ShapeDtypeStruct((B, S, C_HEADS * C_VDIM), BF16),
        scratch_shapes=_attend_scratch(hp, tkc, tq),
        compiler_params=_params(2, 1),
        name="mla_attn",
    )(qT, qT, k, vT)


def _odd_in_kernel(x_ref, g_ref, win_ref, gq_ref, gkv_ref, wuqT_ref, wkn_ref, wvT_ref,
                   cosq_ref, sinq_ref, cosk_ref, sink_ref, qT_ref, k_ref, vT_ref):
    hr = C_ROPE // 2
    xn = _rms_rows(x_ref[0], g_ref[...]).astype(BF16)
    a = jnp.dot(xn, win_ref[...], preferred_element_type=F32)
    cqn = _rms_rows(a[:, :C_Q_RANK], gq_ref[...]).astype(BF16)
    ckvn = _rms_rows(a[:, C_Q_RANK:C_Q_RANK + C_KV_RANK], gkv_ref[...]).astype(BF16)

    qT = lax.dot_general(wuqT_ref[...], cqn, _NT, preferred_element_type=F32)
    qT = qT * ((C_NOPE + C_ROPE) ** -0.5 * LOG2E)
    cq, sq = cosq_ref[...], sinq_ref[...]
    for h in range(C_HEADS):
        r = h * C_PAD
        e = qT[r + C_NOPE:r + C_NOPE + hr]
        o = qT[r + C_NOPE + hr:r + C_NOPE + C_ROPE]
        head = jnp.concatenate(
            [qT[r:r + C_NOPE], e * cq - o * sq, e * sq + o * cq, qT[r + C_NOPE + C_ROPE:r + C_PAD]],
            axis=0)
        qT_ref[0, r:r + C_PAD, :] = head.astype(BF16)

    kblk = a[:, C_Q_RANK + C_KV_RANK:]
    t = kblk * cosk_ref[...] + kblk * sink_ref[...]
    lane = lax.broadcasted_iota(jnp.int32, t.shape, 1)
    kr = jnp.where(lane < C_ROPE, t + pltpu.roll(t, C_PAD - C_ROPE, axis=1), 0.0)
    kr = pltpu.roll(kr, C_NOPE, axis=1)

    kn = jnp.dot(ckvn, wkn_ref[...], preferred_element_type=F32)
    for h in range(C_HEADS):
        k_ref[0, :, h * C_PAD:(h + 1) * C_PAD] = (kn[:, h * C_PAD:(h + 1) * C_PAD] + kr).astype(BF16)

    _store_values(vT_ref, lax.dot_general(wvT_ref[...], ckvn, _NT, preferred_element_type=F32),
                  C_HEADS, C_VDIM)


def _odd_in(x, g, win, gq, gkv, wuqT, wkn, wvT, cosq, sinq, cosk, sink):
    B, S, D = x.shape
    tm = min(TOKEN_TILE, S)
    hr = C_ROPE // 2
    return pl.pallas_call(
        _odd_in_kernel,
        grid=(B, S // tm),
        in_specs=[
            pl.BlockSpec((1, tm, D), lambda b, i: (b, i, 0)),
            _const_spec((1, D)),
            _const_spec(win.shape),
            _const_spec((1, C_Q_RANK)),
            _const_spec((1, C_KV_RANK)),
            _const_spec(wuqT.shape),
            _const_spec(wkn.shape),
            _const_spec(wvT.shape),
            pl.BlockSpec((hr, tm), lambda b, i: (0, i)),
            pl.BlockSpec((hr, tm), lambda b, i: (0, i)),
            pl.BlockSpec((tm, C_PAD), lambda b, i: (i, 0)),
            pl.BlockSpec((tm, C_PAD), lambda b, i: (i, 0)),
        ],
        out_specs=[
            pl.BlockSpec((1, C_HEADS * C_PAD, tm), lambda b, i: (b, 0, i)),
            pl.BlockSpec((1, tm, C_HEADS * C_PAD), lambda b, i: (b, i, 0)),
            pl.BlockSpec((1, C_HEADS * (C_VDIM + ONES_ROWS), tm), lambda b, i: (b, 0, i)),
        ],
        out_shape=[
            jax.ShapeDtypeStruct((B, C_HEADS * C_PAD, S), BF16),
            jax.ShapeDtypeStruct((B, S, C_HEADS * C_PAD), BF16),
            jax.ShapeDtypeStruct((B, C_HEADS * (C_VDIM + ONES_ROWS), S), BF16),
        ],
        compiler_params=_params(2),
        name="odd_in",
    )(x, g, win, gq, gkv, wuqT, wkn, wvT, cosq, sinq, cosk, sink)


def _memkv_kernel(mem_ref, g_ref, w_ref, kv_ref):
    mn = _rms_rows(mem_ref[0], g_ref[...]).astype(BF16)
    kv_ref[0] = jnp.dot(mn, w_ref[...], preferred_element_type=F32).astype(BF16)


def _memkv(mem, g, w):
    B, M, D = mem.shape
    N = w.shape[1]
    return pl.pallas_call(
        _memkv_kernel,
        grid=(B,),
        in_specs=[pl.BlockSpec((1, M, D), lambda b: (b, 0, 0)), _const_spec((1, D)), _const_spec((D, N))],
        out_specs=pl.BlockSpec((1, M, N), lambda b: (b, 0, 0)),
        out_shape=jax.ShapeDtypeStruct((B, M, N), BF16),
        compiler_params=_params(1),
        name="mem_kv",
    )(mem, g, w)


def _post_mix_kernel(*refs, n_mix):
    x_ref = refs[0]
    o_refs = refs[1:1 + n_mix]
    w_refs = refs[1 + n_mix:1 + 2 * n_mix]
    gc_ref, wq_ref, kv_ref, wo_ref, out_ref = refs[1 + 2 * n_mix:]
    x = x_ref[0]
    for o_ref, w_ref in zip(o_refs, w_refs):
        x = x + jnp.dot(o_ref[0], w_ref[...], preferred_element_type=F32)

    hc = _rms_rows(x, gc_ref[...]).astype(BF16)
    q = (jnp.dot(hc, wq_ref[...], preferred_element_type=F32) * (X_DIM ** -0.5)).astype(BF16)
    heads = []
    for h in range(X_HEADS):
        k_h = kv_ref[0, :, h * X_DIM:(h + 1) * X_DIM]
        v_h = kv_ref[0, :, D_MODEL + h * X_DIM:D_MODEL + (h + 1) * X_DIM]
        s = lax.dot_general(q[:, h * X_DIM:(h + 1) * X_DIM], k_h, _NT, preferred_element_type=F32)
        p = jnp.exp(s - jnp.max(s, axis=-1, keepdims=True))
        l = jnp.sum(p, axis=-1, keepdims=True)
        heads.append((jnp.dot(p.astype(BF16), v_h, preferred_element_type=F32) / l).astype(BF16))
    o = jnp.concatenate(heads, axis=-1)
    out_ref[0] = x + jnp.dot(o, wo_ref[...], preferred_element_type=F32)


def _post_mix(x, mixes, weights, gc, wq, kv, wo):
    B, S, D = x.shape
    tm = min(WIDE_TOKEN_TILE, S)
    n = len(mixes)
    M = kv.shape[1]
    tok = lambda b, i: (b, i, 0)
    return pl.pallas_call(
        functools.partial(_post_mix_kernel, n_mix=n),
        grid=(B, S // tm),
        in_specs=([pl.BlockSpec((1, tm, D), tok)]
                  + [pl.BlockSpec((1, tm, m.shape[2]), tok) for m in mixes]
                  + [_const_spec(w.shape) for w in weights]
                  + [_const_spec((1, D)), _const_spec((D, D)),
                     pl.BlockSpec((1, M, 2 * D), lambda b, i: (b, 0, 0)),
                     _const_spec((D, D))]),
        out_specs=pl.BlockSpec((1, tm, D), tok),
        out_shape=jax.ShapeDtypeStruct((B, S, D), F32),
        compiler_params=_params(2),
        name="post_mix",
    )(x, *mixes, *weights, gc, wq, kv, wo)


def _ffn_kernel(x_ref, g_ref, wgu_ref, wd_ref, gf_ref, out_ref, *, final_norm):
    x = x_ref[0]
    xn = _rms_rows(x, g_ref[...]).astype(BF16)
    acc = x
    for c in range(D_FF // FF_CHUNK):
        lo = c * FF_CHUNK
        gate = jnp.dot(xn, wgu_ref[:, lo:lo + FF_CHUNK], preferred_element_type=F32)
        up = jnp.dot(xn, wgu_ref[:, D_FF + lo:D_FF + lo + FF_CHUNK], preferred_element_type=F32)
        hidden = (jax.nn.silu(gate) * up).astype(BF16)
        acc = acc + jnp.dot(hidden, wd_ref[lo:lo + FF_CHUNK, :], preferred_element_type=F32)
    if final_norm:
        acc = _rms_rows(acc, gf_ref[...])
    out_ref[0] = acc


def _ffn(x, g, wgu, wd, gf, final_norm):
    B, S, D = x.shape
    tm = min(WIDE_TOKEN_TILE, S)
    tok = lambda b, i: (b, i, 0)
    return pl.pallas_call(
        functools.partial(_ffn_kernel, final_norm=final_norm),
        grid=(B, S // tm),
        in_specs=[pl.BlockSpec((1, tm, D), tok), _const_spec((1, D)),
                  _const_spec(wgu.shape), _const_spec(wd.shape), _const_spec((1, D))],
        out_specs=pl.BlockSpec((1, tm, D), tok),
        out_shape=jax.ShapeDtypeStruct((B, S, D), F32),
        compiler_params=_params(2),
        name="ffn",
    )(x, g, wgu, wd, gf)


def _rope_freqs(n_pairs):
    return ROPE_THETA ** (-jnp.arange(n_pairs, dtype=F32) / n_pairs)


def _axial_angles(S):
    rows = S // GRID_W
    r = jnp.repeat(jnp.arange(rows, dtype=F32), GRID_W)
    c = jnp.tile(jnp.arange(GRID_W, dtype=F32), rows)
    f = _rope_freqs(A_DIM // 4)
    return jnp.concatenate([r[:, None] * f, c[:, None] * f], axis=-1)


def _linear_angles(S, dim):
    t = jnp.arange(S, dtype=F32)
    return t[:, None] * _rope_freqs(dim // 2)


def _deinterleave(n):
    return np.concatenate([np.arange(0, n, 2), np.arange(1, n, 2)])


def _prep_even(w_in, gq, gk):
    perm = _deinterleave(A_DIM)
    o = 0
    w_qa = w_in[:, o:o + A_Q].reshape(D_MODEL, A_HEADS, A_DIM)[:, :, perm].reshape(D_MODEL, A_Q)
    o += A_Q
    w_ka = w_in[:, o:o + A_KV].reshape(D_MODEL, A_KV_HEADS, A_DIM)[:, :, perm].reshape(D_MODEL, A_KV)
    o += A_KV
    w_va = w_in[:, o:o + A_KV]
    o += A_KV
    w_qb = w_in[:, o:o + B_QK]
    o += B_QK
    w_kb = w_in[:, o:o + B_QK]
    o += B_QK
    w_vb = w_in[:, o:o + B_V]
    wt = jnp.concatenate([w_qa, w_ka, w_va, w_qb, w_vb], axis=1).T.astype(BF16)
    return wt, w_kb.astype(BF16), gq[perm].reshape(A_DIM, 1), gk[perm].reshape(A_DIM, 1)


def _prep_odd(w_in, w_uq, w_ukv):
    hr = C_ROPE // 2
    perm = _deinterleave(C_ROPE)
    w_kr = w_in[:, C_Q_RANK + C_KV_RANK:][:, perm]
    w_kr_rot = jnp.concatenate([-w_kr[:, hr:], w_kr[:, :hr]], axis=1)
    win = jnp.concatenate(
        [w_in[:, :C_Q_RANK + C_KV_RANK], w_kr, w_kr_rot,
         jnp.zeros((D_MODEL, C_PAD - 2 * C_ROPE), F32)], axis=1).astype(BF16)
    wq = w_uq.reshape(C_Q_RANK, C_HEADS, C_NOPE + C_ROPE)
    wq = jnp.concatenate(
        [wq[:, :, :C_NOPE], wq[:, :, C_NOPE:][:, :, perm],
         jnp.zeros((C_Q_RANK, C_HEADS, C_PAD - C_NOPE - C_ROPE), F32)], axis=2)
    wuqT = wq.reshape(C_Q_RANK, C_HEADS * C_PAD).T.astype(BF16)
    wkv = w_ukv.reshape(C_KV_RANK, C_HEADS, C_NOPE + C_VDIM)
    wkn = jnp.concatenate(
        [wkv[:, :, :C_NOPE], jnp.zeros((C_KV_RANK, C_HEADS, C_PAD - C_NOPE), F32)], axis=2)
    wkn = wkn.reshape(C_KV_RANK, C_HEADS * C_PAD).astype(BF16)
    wvT = wkv[:, :, C_NOPE:].reshape(C_KV_RANK, C_HEADS * C_VDIM).T.astype(BF16)
    return win, wuqT, wkn, wvT


def _trunk(x, mem, p):
    B, S, D = x.shape
    depth = p['norm_mix'].shape[0]
    row = lambda v: v.reshape(1, -1)

    ang_a = _axial_angles(S)
    cos_a, sin_a = jnp.cos(ang_a).T, jnp.sin(ang_a).T
    ang_l = _linear_angles(S, C_ROPE)
    cos_l, sin_l = jnp.cos(ang_l), jnp.sin(ang_l)
    zpad = jnp.zeros((S, C_PAD - 2 * C_ROPE), F32)
    zrope = jnp.zeros((S, C_ROPE), F32)
    cos_k = jnp.concatenate([cos_l, cos_l, zrope, zpad], axis=1)
    sin_k = jnp.concatenate([zrope, sin_l, sin_l, zpad], axis=1)
    slopes = jnp.asarray(2.0 ** (-8.0 * np.arange(1, B_HEADS + 1) / B_HEADS), dtype=F32)

    for layer in range(depth):
        if layer % 2 == 0:
            e = layer // 2
            wt, wkb, gq, gk = _prep_even(p['e_w_in'][e], p['e_q_norm'][e], p['e_k_norm'][e])
            qaT, ka, vaT, qbT, kb, vbT = _even_in(
                x, row(p['norm_mix'][layer]), wt, wkb, gq, gk, cos_a, sin_a)
            oa = _gqa(qaT, ka, vaT)
            lam_init = 0.8 - 0.6 * math.exp(-0.3 * layer)
            ob = _diff(slopes, row(p['e_lam_q1'][e]), row(p['e_lam_k1'][e]),
                       row(p['e_lam_q2'][e]), row(p['e_lam_k2'][e]),
                       p['e_subln'][e].reshape(B_VDIM, 1), qbT, kb, vbT, lam_init)
            w_out = p['e_w_out'][e].astype(BF16)
            mixes, weights = [oa, ob], [w_out[:A_Q], w_out[A_Q:]]
        else:
            o = layer // 2
            win, wuqT, wkn, wvT = _prep_odd(p['o_w_in'][o], p['o_w_uq'][o], p['o_w_ukv'][o])
            qT, k, vT = _odd_in(x, row(p['norm_mix'][layer]), win, row(p['o_q_norm'][o]),
                                row(p['o_kv_norm'][o]), wuqT, wkn, wvT,
                                cos_l.T, sin_l.T, cos_k, sin_k)
            mixes, weights = [_mla(qT, k, vT)], [p['o_w_out'][o].astype(BF16)]
        kv = _memkv(mem, row(p['norm_mem'][layer]), p['w_ckv'][layer].astype(BF16))
        x = _post_mix(x, mixes, weights, row(p['norm_cross'][layer]),
                      p['w_cq'][layer].astype(BF16), kv, p['w_co'][layer].astype(BF16))
        x = _ffn(x, row(p['norm_ffn'][layer]), p['w_gu'][layer].astype(BF16),
                 p['w_down'][layer].astype(BF16), row(p['final_norm']),
                 final_norm=(layer == depth - 1))
    return x


def kernel(x_prompt, x_sample, mem_prompt, mem_sample, norm_mix, e_w_in, e_q_norm, e_k_norm, e_lam_q1, e_lam_k1, e_lam_q2, e_lam_k2, e_subln, e_w_out, o_w_in, o_q_norm, o_kv_norm, o_w_uq, o_w_ukv, o_w_out, norm_cross, norm_mem, w_cq, w_ckv, w_co, norm_ffn, w_gu, w_down, final_norm):
    p = dict(norm_mix=norm_mix, e_w_in=e_w_in, e_q_norm=e_q_norm, e_k_norm=e_k_norm,
             e_lam_q1=e_lam_q1, e_lam_k1=e_lam_k1, e_lam_q2=e_lam_q2, e_lam_k2=e_lam_k2,
             e_subln=e_subln, e_w_out=e_w_out, o_w_in=o_w_in, o_q_norm=o_q_norm,
             o_kv_norm=o_kv_norm, o_w_uq=o_w_uq, o_w_ukv=o_w_ukv, o_w_out=o_w_out,
             norm_cross=norm_cross, norm_mem=norm_mem, w_cq=w_cq, w_ckv=w_ckv, w_co=w_co,
             norm_ffn=norm_ffn, w_gu=w_gu, w_down=w_down, final_norm=final_norm)
    return (_trunk(x_prompt, mem_prompt, p), _trunk(x_sample, mem_sample, p))
```

```python
import functools
import math

import jax
import jax.numpy as jnp
import numpy as np
from jax import lax
from jax.experimental import pallas as pl
from jax.experimental.pallas import tpu as pltpu

F32 = jnp.float32
BF16 = jnp.bfloat16

D_MODEL = 1024
GRID_W = 64
EPS = 1e-6
ROPE_THETA = 10000.0
A_HEADS, A_KV_HEADS, A_DIM = 8, 2, 64
A_GROUP = A_HEADS // A_KV_HEADS
B_HEADS, B_DIM = 4, 64
B_VDIM = 2 * B_DIM
A_Q = A_HEADS * A_DIM
A_KV = A_KV_HEADS * A_DIM
B_QK = B_HEADS * 2 * B_DIM
B_V = B_HEADS * B_VDIM
C_HEADS, C_Q_RANK, C_KV_RANK, C_NOPE, C_ROPE, C_VDIM = 16, 384, 256, 64, 32, 64
C_PAD = 128
X_HEADS = 4
X_DIM = D_MODEL // X_HEADS
D_FF = ((-(-8 * D_MODEL // 3) + 255) // 256) * 256
FF_CHUNK = 256
NEG_BIG = -1e30
LOG2E = math.log2(math.e)

TOKEN_TILE = 512
WIDE_TOKEN_TILE = 512
Q_TILE = 256
KEY_CHUNK = 512
VMEM_LIMIT = 48 * 1024 * 1024

_NT = (((1,), (1,)), ((), ()))


def _params(n_parallel, n_arbitrary=0):
    return pltpu.CompilerParams(
        dimension_semantics=("parallel",) * n_parallel + ("arbitrary",) * n_arbitrary,
        vmem_limit_bytes=VMEM_LIMIT)


def _rms_rows(x, g):
    ms = jnp.mean(x * x, axis=-1, keepdims=True)
    return (x * lax.rsqrt(ms + EPS)) * g


def _rms_cols(x, g):
    ms = jnp.mean(x * x, axis=0, keepdims=True)
    return (x * lax.rsqrt(ms + EPS)) * g


def _const_spec(shape):
    nd = len(shape)
    return pl.BlockSpec(shape, lambda *_: (0,) * nd)


ONES_ROWS = 16


def _store_values(ref, vT, n_heads, dv):
    ext = dv + ONES_ROWS
    ones = jnp.ones((ONES_ROWS, vT.shape[1]), BF16)
    for h in range(n_heads):
        ref[0, h * ext:h * ext + dv, :] = vT[h * dv:(h + 1) * dv].astype(BF16)
        ref[0, h * ext + dv:(h + 1) * ext, :] = ones


def _value_rows(h, dv):
    ext = dv + ONES_ROWS
    return slice(h * ext, (h + 1) * ext)


def _even_in_kernel(x_ref, g_ref, wt_ref, wkb_ref, gq_ref, gk_ref, cos_ref, sin_ref,
                    qaT_ref, ka_ref, vaT_ref, qbT_ref, kb_ref, vbT_ref):
    half = A_DIM // 2
    xn = _rms_rows(x_ref[0], g_ref[...]).astype(BF16)
    yT = lax.dot_general(wt_ref[...], xn, _NT, preferred_element_type=F32)
    cos = cos_ref[...]
    sin = sin_ref[...]

    def norm_rope(xh, g, scale):
        y = _rms_cols(xh, g)
        e, o = y[:half], y[half:]
        return jnp.concatenate([e * cos - o * sin, e * sin + o * cos], axis=0) * scale

    scale_a = A_DIM ** -0.5 * LOG2E
    for h in range(A_HEADS):
        qh = norm_rope(yT[h * A_DIM:(h + 1) * A_DIM], gq_ref[...], scale_a)
        qaT_ref[0, h * A_DIM:(h + 1) * A_DIM, :] = qh.astype(BF16)
    kT = jnp.concatenate(
        [norm_rope(yT[A_Q + h * A_DIM:A_Q + (h + 1) * A_DIM], gk_ref[...], 1.0)
         for h in range(A_KV_HEADS)], axis=0)
    ka_ref[0] = kT.T.astype(BF16)
    r0 = A_Q + A_KV
    _store_values(vaT_ref, yT[r0:r0 + A_KV], A_KV_HEADS, A_DIM)
    r0 += A_KV
    qbT_ref[0] = (yT[r0:r0 + B_QK] * (B_DIM ** -0.5 * LOG2E)).astype(BF16)
    r0 += B_QK
    _store_values(vbT_ref, yT[r0:r0 + B_V], B_HEADS, B_VDIM)
    kb_ref[0] = jnp.dot(xn, wkb_ref[...], preferred_element_type=F32).astype(BF16)


def _even_in(x, g, wt, wkb, gq, gk, cosT, sinT):
    B, S, D = x.shape
    tm = min(TOKEN_TILE, S)
    rows = wt.shape[0]
    va_rows = A_KV_HEADS * (A_DIM + ONES_ROWS)
    vb_rows = B_HEADS * (B_VDIM + ONES_ROWS)
    return pl.pallas_call(
        _even_in_kernel,
        grid=(B, S // tm),
        in_specs=[
            pl.BlockSpec((1, tm, D), lambda b, i: (b, i, 0)),
            _const_spec((1, D)),
            _const_spec((rows, D)),
            _const_spec((D, B_QK)),
            _const_spec((A_DIM, 1)),
            _const_spec((A_DIM, 1)),
            pl.BlockSpec((A_DIM // 2, tm), lambda b, i: (0, i)),
            pl.BlockSpec((A_DIM // 2, tm), lambda b, i: (0, i)),
        ],
        out_specs=[
            pl.BlockSpec((1, A_Q, tm), lambda b, i: (b, 0, i)),
            pl.BlockSpec((1, tm, A_KV), lambda b, i: (b, i, 0)),
            pl.BlockSpec((1, va_rows, tm), lambda b, i: (b, 0, i)),
            pl.BlockSpec((1, B_QK, tm), lambda b, i: (b, 0, i)),
            pl.BlockSpec((1, tm, B_QK), lambda b, i: (b, i, 0)),
            pl.BlockSpec((1, vb_rows, tm), lambda b, i: (b, 0, i)),
        ],
        out_shape=[
            jax.ShapeDtypeStruct((B, A_Q, S), BF16),
            jax.ShapeDtypeStruct((B, S, A_KV), BF16),
            jax.ShapeDtypeStruct((B, va_rows, S), BF16),
            jax.ShapeDtypeStruct((B, B_QK, S), BF16),
            jax.ShapeDtypeStruct((B, S, B_QK), BF16),
            jax.ShapeDtypeStruct((B, vb_rows, S), BF16),
        ],
        compiler_params=_params(2),
        name="even_in",
    )(x, g, wt, wkb, gq, gk, cosT, sinT)


def _chain_init(n, dv, tq):
    return tuple((jnp.full((1, tq), NEG_BIG, F32), jnp.zeros((dv + ONES_ROWS, tq), F32))
                 for _ in range(n))


def _chain_out(acc, dv):
    return acc[:dv] / acc[dv:dv + 1]


def _chunk_start(c, tkc):
    return c * tkc if isinstance(c, int) else pl.multiple_of(c * tkc, tkc)


def _key_chunk(S):
    return min(KEY_CHUNK, S // 2)


def _attend(n_chunks, n_chains, dv, tq, first_tile, prep, score, value, scratch, offset=None):
    assert n_chunks % 2 == 0
    max_ref, s_bufs = scratch[0], (scratch[1:1 + n_chains], scratch[1 + n_chains:])

    def produce(ctx, j, nxt, slot):
        s = score(ctx, j, nxt)
        s_bufs[slot][j][...] = s
        top = jnp.max(s, axis=0, keepdims=True)
        r = None if offset is None else offset(ctx, j, nxt)
        return (top, None) if r is None else (top - r, r)

    @pl.when(first_tile)
    def _():
        ctx0 = prep(0, False)
        for j in range(n_chains):
            max_ref[j] = produce(ctx0, j, False, 0)[0]

    maxes = [(max_ref[j], None) for j in range(n_chains)]
    chains = list(_chain_init(n_chains, dv, tq))
    for c in range(n_chunks):
        slot = c % 2
        nxt = c == n_chunks - 1
        ctx = prep(0 if nxt else c + 1, nxt)
        for j in range(n_chains):
            produced = produce(ctx, j, nxt, 1 - slot)
            m, acc = chains[j]
            top, r = maxes[j]
            m_new = jnp.maximum(m, top)
            alpha = jnp.exp2(m - m_new)
            shift = m_new if r is None else m_new + r
            p = jnp.exp2(s_bufs[slot][j][...] - shift).astype(BF16)
            acc = alpha * acc + jnp.dot(value(c, j), p, preferred_element_type=F32)
            chains[j] = (m_new, acc)
            maxes[j] = produced
    for j in range(n_chains):
        max_ref[j] = maxes[j][0]
    return [_chain_out(acc, dv) for _, acc in chains]


def _attend_scratch(n_chains, tkc, tq):
    return ([pltpu.VMEM((n_chains, 1, tq), F32)]
            + [pltpu.VMEM((tkc, tq), F32) for _ in range(2 * n_chains)])


def _next_tile(n_tiles):
    return lambda i: jnp.minimum(i + 1, n_tiles - 1)


def _gqa_kernel(qT_ref, qT_next_ref, k_ref, vT_ref, o_ref, *scratch, tkc):
    S = k_ref.shape[1]
    tq = qT_ref.shape[2]

    def padded_queries(ref):
        pads = []
        for h in range(A_HEADS):
            q = ref[0, h * A_DIM:(h + 1) * A_DIM, :]
            zero = jnp.zeros_like(q)
            pads.append(jnp.concatenate([q, zero] if h < A_GROUP else [zero, q], axis=0))
        return pads

    q_pads = {False: padded_queries(qT_ref), True: padded_queries(qT_next_ref)}

    def prep(c, nxt):
        return k_ref[0, pl.ds(_chunk_start(c, tkc), tkc), :]

    def score(k_c, h, nxt):
        return jnp.dot(k_c, q_pads[nxt][h], preferred_element_type=F32)

    def value(c, h):
        g = h // A_GROUP
        return vT_ref[0, _value_rows(g, A_DIM), pl.ds(_chunk_start(c, tkc), tkc)]

    outs = _attend(S // tkc, A_HEADS, A_DIM, tq, pl.program_id(1) == 0,
                   prep, score, value, scratch)
    o_ref[0] = jnp.concatenate(outs, axis=0).T.astype(o_ref.dtype)


def _gqa(qaT, ka, vaT):
    B, _, S = qaT.shape
    tq = min(Q_TILE, S)
    tkc = _key_chunk(S)
    nxt = _next_tile(S // tq)
    return pl.pallas_call(
        functools.partial(_gqa_kernel, tkc=tkc),
        grid=(B, S // tq),
        in_specs=[
            pl.BlockSpec((1, A_Q, tq), lambda b, i: (b, 0, i)),
            pl.BlockSpec((1, A_Q, tq), lambda b, i: (b, 0, nxt(i))),
            pl.BlockSpec((1, S, A_KV), lambda b, i: (b, 0, 0)),
            pl.BlockSpec((1, A_KV_HEADS * (A_DIM + ONES_ROWS), S), lambda b, i: (b, 0, 0)),
        ],
        out_specs=pl.BlockSpec((1, tq, A_Q), lambda b, i: (b, i, 0)),
        out_shape=jax.ShapeDtypeStruct((B, S, A_Q), BF16),
        scratch_shapes=_attend_scratch(A_HEADS, tkc, tq),
        compiler_params=_params(1, 1),
        name="gqa_attn",
    )(qaT, qaT, ka, vaT)


DIFF_HEADS_PER_STEP = 4


ALIBI_PIECES = 3


def _diff_kernel(slopes_ref, lq1_ref, lk1_ref, lq2_ref, lk2_ref, gsub_ref,
                 qT_ref, qT_next_ref, k_ref, kpos_ref, vT_ref, o_ref, *scratch, tkc, lam_init):
    hp = pl.program_id(1)
    i = pl.program_id(2)
    S = k_ref.shape[1]
    tq = qT_ref.shape[2]
    n_chunks = S // tkc
    n_tiles = S // tq
    pair = 2 * B_DIM
    coefs = [slopes_ref[DIFF_HEADS_PER_STEP * hp + hh] * LOG2E for hh in range(DIFF_HEADS_PER_STEP)]

    def padded_queries(ref):
        pads = []
        for hh in range(DIFF_HEADS_PER_STEP):
            q = ref[0, hh * pair:(hh + 1) * pair, :]
            rows = lax.broadcasted_iota(jnp.int32, q.shape, 0)
            zero = jnp.zeros_like(q)
            pads += [jnp.where(rows < B_DIM, q, zero), jnp.where(rows >= B_DIM, q, zero)]
        return pads

    q_pads = {False: padded_queries(qT_ref), True: padded_queries(qT_next_ref)}
    dmat = (lax.broadcasted_iota(jnp.int32, (tkc, tq), 1)
            - lax.broadcasted_iota(jnp.int32, (tkc, tq), 0)).astype(F32)
    q_local = lax.broadcasted_iota(jnp.int32, (1, tq), 1).astype(F32)
    piece_rows = lax.broadcasted_iota(jnp.int32, (pair, tq), 0) < ALIBI_PIECES

    def chunk_of(c, tile):
        home = (tile * tq) // tkc
        return home, (home if c == 0 else lax.rem(home + c, n_chunks))

    def prep(c, nxt):
        tile = jnp.minimum(i + 1, n_tiles - 1) if nxt else i
        home, chunk = chunk_of(c, tile)
        start = pl.multiple_of(chunk * tkc, tkc)
        q_start = (tile * tq).astype(F32)
        if c == 0:
            dist = jnp.abs(dmat + (q_start - start.astype(F32)))
            return dict(start=start, biases=[dist * -cf for cf in coefs])
        sign = jnp.where(chunk < home, 1.0, -1.0)
        ext = jnp.where(piece_rows, sign, 0.0).astype(BF16)
        q_pos = q_start + q_local
        return dict(start=start, ext=ext, offsets=[(sign * cf) * q_pos for cf in coefs])

    def score(ctx, n, nxt):
        hh = n // 2
        k_c = k_ref[0, pl.ds(ctx["start"], tkc), hh * pair:(hh + 1) * pair]
        q_pad = q_pads[nxt][n]
        if "biases" in ctx:
            return jnp.dot(k_c, q_pad, preferred_element_type=F32) + ctx["biases"][hh]
        keys = jnp.concatenate([k_c, kpos_ref[hh, pl.ds(ctx["start"], tkc), :]], axis=1)
        queries = jnp.concatenate([q_pad, ctx["ext"]], axis=0)
        return jnp.dot(keys, queries, preferred_element_type=F32)

    def offset(ctx, n, nxt):
        return None if "biases" in ctx else ctx["offsets"][n // 2]

    def value(c, n):
        _, chunk = chunk_of(c, i)
        start = pl.multiple_of(chunk * tkc, tkc)
        return vT_ref[0, _value_rows(n // 2, B_VDIM), pl.ds(start, tkc)]

    n_chains = 2 * DIFF_HEADS_PER_STEP
    parts = _attend(n_chunks, n_chains, B_VDIM, tq, i == 0, prep, score, value, scratch, offset)

    lam = (jnp.exp(jnp.sum(lq1_ref[...] * lk1_ref[...], axis=-1, keepdims=True))
           - jnp.exp(jnp.sum(lq2_ref[...] * lk2_ref[...], axis=-1, keepdims=True)) + lam_init)
    outs = []
    for hh in range(DIFF_HEADS_PER_STEP):
        o = parts[2 * hh] - lam * parts[2 * hh + 1]
        outs.append(_rms_cols(o, gsub_ref[...]) * (1.0 - lam_init))
    o_ref[0] = jnp.concatenate(outs, axis=0).T.astype(o_ref.dtype)


def _alibi_key_table(slopes, S):
    a = (slopes * LOG2E)[:, None] * jnp.arange(S, dtype=F32)[None, :]
    pieces, rest = [], a
    for _ in range(ALIBI_PIECES):
        piece = lax.bitcast_convert_type(
            lax.bitcast_convert_type(rest, jnp.uint32) & jnp.uint32(0xFFFF0000), F32)
        pieces.append(piece.astype(BF16))
        rest = rest - piece
    table = jnp.stack(pieces, axis=-1)
    return jnp.pad(table, ((0, 0), (0, 0), (0, 2 * B_DIM - ALIBI_PIECES)))


def _diff(slopes, lq1, lk1, lq2, lk2, gsub, qbT, kb, vbT, lam_init):
    B, _, S = qbT.shape
    tq = min(Q_TILE, S)
    tkc = _key_chunk(S)
    assert tkc % tq == 0
    hs = DIFF_HEADS_PER_STEP
    nxt = _next_tile(S // tq)
    kpos = _alibi_key_table(slopes, S)
    return pl.pallas_call(
        functools.partial(_diff_kernel, tkc=tkc, lam_init=lam_init),
        grid=(B, B_HEADS // hs, S // tq),
        in_specs=[
            pl.BlockSpec(memory_space=pltpu.SMEM),
            _const_spec((1, B_DIM)), _const_spec((1, B_DIM)),
            _const_spec((1, B_DIM)), _const_spec((1, B_DIM)),
            _const_spec((B_VDIM, 1)),
            pl.BlockSpec((1, hs * 2 * B_DIM, tq), lambda b, h, i: (b, h, i)),
            pl.BlockSpec((1, hs * 2 * B_DIM, tq), lambda b, h, i: (b, h, nxt(i))),
            pl.BlockSpec((1, S, hs * 2 * B_DIM), lambda b, h, i: (b, 0, h)),
            pl.BlockSpec((hs, S, 2 * B_DIM), lambda b, h, i: (h, 0, 0)),
            pl.BlockSpec((1, hs * (B_VDIM + ONES_ROWS), S), lambda b, h, i: (b, h, 0)),
        ],
        out_specs=pl.BlockSpec((1, tq, hs * B_VDIM), lambda b, h, i: (b, i, h)),
        out_shape=jax.ShapeDtypeStruct((B, S, B_V), BF16),
        scratch_shapes=_attend_scratch(2 * hs, tkc, tq),
        compiler_params=_params(2, 1),
        name="diff_attn",
    )(slopes, lq1, lk1, lq2, lk2, gsub, qbT, qbT, kb, kpos, vbT)


MLA_HEADS_PER_STEP = 4


def _mla_kernel(qT_ref, qT_next_ref, k_ref, vT_ref, o_ref, *scratch, tkc):
    S = k_ref.shape[1]
    tq = qT_ref.shape[2]
    nh = MLA_HEADS_PER_STEP
    q_refs = {False: qT_ref, True: qT_next_ref}

    def prep(c, nxt):
        return _chunk_start(c, tkc)

    def score(start, j, nxt):
        k_c = k_ref[0, pl.ds(start, tkc), j * C_PAD:(j + 1) * C_PAD]
        q_pad = q_refs[nxt][0, j * C_PAD:(j + 1) * C_PAD, :]
        return jnp.dot(k_c, q_pad, preferred_element_type=F32)

    def value(c, j):
        return vT_ref[0, _value_rows(j, C_VDIM), pl.ds(_chunk_start(c, tkc), tkc)]

    outs = _attend(S // tkc, nh, C_VDIM, tq, pl.program_id(2) == 0, prep, score, value, scratch)
    o_ref[0] = jnp.concatenate(outs, axis=0).T.astype(o_ref.dtype)


def _mla(qT, k, vT):
    B, _, S = qT.shape
    tq = min(Q_TILE, S)
    tkc = _key_chunk(S)
    hp = MLA_HEADS_PER_STEP
    nxt = _next_tile(S // tq)
    return pl.pallas_call(
        functools.partial(_mla_kernel, tkc=tkc),
        grid=(B, C_HEADS // hp, S // tq),
        in_specs=[
            pl.BlockSpec((1, hp * C_PAD, tq), lambda b, h, i: (b, h, i)),
            pl.BlockSpec((1, hp * C_PAD, tq), lambda b, h, i: (b, h, nxt(i))),
            pl.BlockSpec((1, S, hp * C_PAD), lambda b, h, i: (b, 0, h)),
            pl.BlockSpec((1, hp * (C_VDIM + ONES_ROWS), S), lambda b, h, i: (b, h, 0)),
        ],
        out_specs=pl.BlockSpec((1, tq, hp * C_VDIM), lambda b, h, i: (b, i, h)),
        out_shape=jax.ShapeDtypeStruct((B, S, C_HEADS * C_VDIM), BF16),
        scratch_shapes=_attend_scratch(hp, tkc, tq),
        compiler_params=_params(2, 1),
        name="mla_attn",
    )(qT, qT, k, vT)


def _odd_in_kernel(x_ref, g_ref, win_ref, gq_ref, gkv_ref, wuqT_ref, wkn_ref, wvT_ref,
                   cosq_ref, sinq_ref, cosk_ref, sink_ref, qT_ref, k_ref, vT_ref):
    hr = C_ROPE // 2
    xn = _rms_rows(x_ref[0], g_ref[...]).astype(BF16)
    a = jnp.dot(xn, win_ref[...], preferred_element_type=F32)
    cqn = _rms_rows(a[:, :C_Q_RANK], gq_ref[...]).astype(BF16)
    ckvn = _rms_rows(a[:, C_Q_RANK:C_Q_RANK + C_KV_RANK], gkv_ref[...]).astype(BF16)

    qT = lax.dot_general(wuqT_ref[...], cqn, _NT, preferred_element_type=F32)
    qT = qT * ((C_NOPE + C_ROPE) ** -0.5 * LOG2E)
    cq, sq = cosq_ref[...], sinq_ref[...]
    for h in range(C_HEADS):
        r = h * C_PAD
        e = qT[r + C_NOPE:r + C_NOPE + hr]
        o = qT[r + C_NOPE + hr:r + C_NOPE + C_ROPE]
        head = jnp.concatenate(
            [qT[r:r + C_NOPE], e * cq - o * sq, e * sq + o * cq, qT[r + C_NOPE + C_ROPE:r + C_PAD]],
            axis=0)
        qT_ref[0, r:r + C_PAD, :] = head.astype(BF16)

    kblk = a[:, C_Q_RANK + C_KV_RANK:]
    t = kblk * cosk_ref[...] + kblk * sink_ref[...]
    lane = lax.broadcasted_iota(jnp.int32, t.shape, 1)
    kr = jnp.where(lane < C_ROPE, t + pltpu.roll(t, C_PAD - C_ROPE, axis=1), 0.0)
    kr = pltpu.roll(kr, C_NOPE, axis=1)

    kn = jnp.dot(ckvn, wkn_ref[...], preferred_element_type=F32)
    for h in range(C_HEADS):
        k_ref[0, :, h * C_PAD:(h + 1) * C_PAD] = (kn[:, h * C_PAD:(h + 1) * C_PAD] + kr).astype(BF16)

    _store_values(vT_ref, lax.dot_general(wvT_ref[...], ckvn, _NT, preferred_element_type=F32),
                  C_HEADS, C_VDIM)


def _odd_in(x, g, win, gq, gkv, wuqT, wkn, wvT, cosq, sinq, cosk, sink):
    B, S, D = x.shape
    tm = min(TOKEN_TILE, S)
    hr = C_ROPE // 2
    return pl.pallas_call(
        _odd_in_kernel,
        grid=(B, S // tm),
        in_specs=[
            pl.BlockSpec((1, tm, D), lambda b, i: (b, i, 0)),
            _const_spec((1, D)),
            _const_spec(win.shape),
            _const_spec((1, C_Q_RANK)),
            _const_spec((1, C_KV_RANK)),
            _const_spec(wuqT.shape),
            _const_spec(wkn.shape),
            _const_spec(wvT.shape),
            pl.BlockSpec((hr, tm), lambda b, i: (0, i)),
            pl.BlockSpec((hr, tm), lambda b, i: (0, i)),
            pl.BlockSpec((tm, C_PAD), lambda b, i: (i, 0)),
            pl.BlockSpec((tm, C_PAD), lambda b, i: (i, 0)),
        ],
        out_specs=[
            pl.BlockSpec((1, C_HEADS * C_PAD, tm), lambda b, i: (b, 0, i)),
            pl.BlockSpec((1, tm, C_HEADS * C_PAD), lambda b, i: (b, i, 0)),
            pl.BlockSpec((1, C_HEADS * (C_VDIM + ONES_ROWS), tm), lambda b, i: (b, 0, i)),
        ],
        out_shape=[
            jax.ShapeDtypeStruct((B, C_HEADS * C_PAD, S), BF16),
            jax.ShapeDtypeStruct((B, S, C_HEADS * C_PAD), BF16),
            jax.ShapeDtypeStruct((B, C_HEADS * (C_VDIM + ONES_ROWS), S), BF16),
        ],
        compiler_params=_params(2),
        name="odd_in",
    )(x, g, win, gq, gkv, wuqT, wkn, wvT, cosq, sinq, cosk, sink)


def _memkv_kernel(mem_ref, g_ref, w_ref, kv_ref):
    mn = _rms_rows(mem_ref[0], g_ref[...]).astype(BF16)
    kv_ref[0] = jnp.dot(mn, w_ref[...], preferred_element_type=F32).astype(BF16)


def _memkv(mem, g, w):
    B, M, D = mem.shape
    N = w.shape[1]
    return pl.pallas_call(
        _memkv_kernel,
        grid=(B,),
        in_specs=[pl.BlockSpec((1, M, D), lambda b: (b, 0, 0)), _const_spec((1, D)), _const_spec((D, N))],
        out_specs=pl.BlockSpec((1, M, N), lambda b: (b, 0, 0)),
        out_shape=jax.ShapeDtypeStruct((B, M, N), BF16),
        compiler_params=_params(1),
        name="mem_kv",
    )(mem, g, w)


def _post_mix_kernel(*refs, n_mix):
    x_ref = refs[0]
    o_refs = refs[1:1 + n_mix]
    w_refs = refs[1 + n_mix:1 + 2 * n_mix]
    gc_ref, wq_ref, kv_ref, wo_ref, out_ref = refs[1 + 2 * n_mix:]
    x = x_ref[0]
    for o_ref, w_ref in zip(o_refs, w_refs):
        x = x + jnp.dot(o_ref[0], w_ref[...], preferred_element_type=F32)

    hc = _rms_rows(x, gc_ref[...]).astype(BF16)
    q = (jnp.dot(hc, wq_ref[...], preferred_element_type=F32) * (X_DIM ** -0.5)).astype(BF16)
    heads = []
    for h in range(X_HEADS):
        k_h = kv_ref[0, :, h * X_DIM:(h + 1) * X_DIM]
        v_h = kv_ref[0, :, D_MODEL + h * X_DIM:D_MODEL + (h + 1) * X_DIM]
        s = lax.dot_general(q[:, h * X_DIM:(h + 1) * X_DIM], k_h, _NT, preferred_element_type=F32)
        p = jnp.exp(s - jnp.max(s, axis=-1, keepdims=True))
        l = jnp.sum(p, axis=-1, keepdims=True)
        heads.append((jnp.dot(p.astype(BF16), v_h, preferred_element_type=F32) / l).astype(BF16))
    o = jnp.concatenate(heads, axis=-1)
    out_ref[0] = x + jnp.dot(o, wo_ref[...], preferred_element_type=F32)


def _post_mix(x, mixes, weights, gc, wq, kv, wo):
    B, S, D = x.shape
    tm = min(WIDE_TOKEN_TILE, S)
    n = len(mixes)
    M = kv.shape[1]
    tok = lambda b, i: (b, i, 0)
    return pl.pallas_call(
        functools.partial(_post_mix_kernel, n_mix=n),
        grid=(B, S // tm),
        in_specs=([pl.BlockSpec((1, tm, D), tok)]
                  + [pl.BlockSpec((1, tm, m.shape[2]), tok) for m in mixes]
                  + [_const_spec(w.shape) for w in weights]
                  + [_const_spec((1, D)), _const_spec((D, D)),
                     pl.BlockSpec((1, M, 2 * D), lambda b, i: (b, 0, 0)),
                     _const_spec((D, D))]),
        out_specs=pl.BlockSpec((1, tm, D), tok),
        out_shape=jax.ShapeDtypeStruct((B, S, D), F32),
        compiler_params=_params(2),
        name="post_mix",
    )(x, *mixes, *weights, gc, wq, kv, wo)


def _ffn_kernel(x_ref, g_ref, wgu_ref, wd_ref, gf_ref, out_ref, *, final_norm):
    x = x_ref[0]
    xn = _rms_rows(x, g_ref[...]).astype(BF16)
    acc = x
    for c in range(D_FF // FF_CHUNK):
        lo = c * FF_CHUNK
        gate = jnp.dot(xn, wgu_ref[:, lo:lo + FF_CHUNK], preferred_element_type=F32)
        up = jnp.dot(xn, wgu_ref[:, D_FF + lo:D_FF + lo + FF_CHUNK], preferred_element_type=F32)
        hidden = (jax.nn.silu(gate) * up).astype(BF16)
        acc = acc + jnp.dot(hidden, wd_ref[lo:lo + FF_CHUNK, :], preferred_element_type=F32)
    if final_norm:
        acc = _rms_rows(acc, gf_ref[...])
    out_ref[0] = acc


def _ffn(x, g, wgu, wd, gf, final_norm):
    B, S, D = x.shape
    tm = min(WIDE_TOKEN_TILE, S)
    tok = lambda b, i: (b, i, 0)
    return pl.pallas_call(
        functools.partial(_ffn_kernel, final_norm=final_norm),
        grid=(B, S // tm),
        in_specs=[pl.BlockSpec((1, tm, D), tok), _const_spec((1, D)),
                  _const_spec(wgu.shape), _const_spec(wd.shape), _const_spec((1, D))],
        out_specs=pl.BlockSpec((1, tm, D), tok),
        out_shape=jax.ShapeDtypeStruct((B, S, D), F32),
        compiler_params=_params(2),
        name="ffn",
    )(x, g, wgu, wd, gf)


def _rope_freqs(n_pairs):
    return ROPE_THETA ** (-jnp.arange(n_pairs, dtype=F32) / n_pairs)


def _axial_angles(S):
    rows = S // GRID_W
    r = jnp.repeat(jnp.arange(rows, dtype=F32), GRID_W)
    c = jnp.tile(jnp.arange(GRID_W, dtype=F32), rows)
    f = _rope_freqs(A_DIM // 4)
    return jnp.concatenate([r[:, None] * f, c[:, None] * f], axis=-1)


def _linear_angles(S, dim):
    t = jnp.arange(S, dtype=F32)
    return t[:, None] * _rope_freqs(dim // 2)


def _deinterleave(n):
    return np.concatenate([np.arange(0, n, 2), np.arange(1, n, 2)])


def _prep_even(w_in, gq, gk):
    perm = _deinterleave(A_DIM)
    o = 0
    w_qa = w_in[:, o:o + A_Q].reshape(D_MODEL, A_HEADS, A_DIM)[:, :, perm].reshape(D_MODEL, A_Q)
    o += A_Q
    w_ka = w_in[:, o:o + A_KV].reshape(D_MODEL, A_KV_HEADS, A_DIM)[:, :, perm].reshape(D_MODEL, A_KV)
    o += A_KV
    w_va = w_in[:, o:o + A_KV]
    o += A_KV
    w_qb = w_in[:, o:o + B_QK]
    o += B_QK
    w_kb = w_in[:, o:o + B_QK]
    o += B_QK
    w_vb = w_in[:, o:o + B_V]
    wt = jnp.concatenate([w_qa, w_ka, w_va, w_qb, w_vb], axis=1).T.astype(BF16)
    return wt, w_kb.astype(BF16), gq[perm].reshape(A_DIM, 1), gk[perm].reshape(A_DIM, 1)


def _prep_odd(w_in, w_uq, w_ukv):
    hr = C_ROPE // 2
    perm = _deinterleave(C_ROPE)
    w_kr = w_in[:, C_Q_RANK + C_KV_RANK:][:, perm]
    w_kr_rot = jnp.concatenate([-w_kr[:, hr:], w_kr[:, :hr]], axis=1)
    win = jnp.concatenate(
        [w_in[:, :C_Q_RANK + C_KV_RANK], w_kr, w_kr_rot,
         jnp.zeros((D_MODEL, C_PAD - 2 * C_ROPE), F32)], axis=1).astype(BF16)
    wq = w_uq.reshape(C_Q_RANK, C_HEADS, C_NOPE + C_ROPE)
    wq = jnp.concatenate(
        [wq[:, :, :C_NOPE], wq[:, :, C_NOPE:][:, :, perm],
         jnp.zeros((C_Q_RANK, C_HEADS, C_PAD - C_NOPE - C_ROPE), F32)], axis=2)
    wuqT = wq.reshape(C_Q_RANK, C_HEADS * C_PAD).T.astype(BF16)
    wkv = w_ukv.reshape(C_KV_RANK, C_HEADS, C_NOPE + C_VDIM)
    wkn = jnp.concatenate(
        [wkv[:, :, :C_NOPE], jnp.zeros((C_KV_RANK, C_HEADS, C_PAD - C_NOPE), F32)], axis=2)
    wkn = wkn.reshape(C_KV_RANK, C_HEADS * C_PAD).astype(BF16)
    wvT = wkv[:, :, C_NOPE:].reshape(C_KV_RANK, C_HEADS * C_VDIM).T.astype(BF16)
    return win, wuqT, wkn, wvT


def _trunk(x, mem, p):
    B, S, D = x.shape
    depth = p['norm_mix'].shape[0]
    row = lambda v: v.reshape(1, -1)

    ang_a = _axial_angles(S)
    cos_a, sin_a = jnp.cos(ang_a).T, jnp.sin(ang_a).T
    ang_l = _linear_angles(S, C_ROPE)
    cos_l, sin_l = jnp.cos(ang_l), jnp.sin(ang_l)
    zpad = jnp.zeros((S, C_PAD - 2 * C_ROPE), F32)
    zrope = jnp.zeros((S, C_ROPE), F32)
    cos_k = jnp.concatenate([cos_l, cos_l, zrope, zpad], axis=1)
    sin_k = jnp.concatenate([zrope, sin_l, sin_l, zpad], axis=1)
    slopes = jnp.asarray(2.0 ** (-8.0 * np.arange(1, B_HEADS + 1) / B_HEADS), dtype=F32)

    for layer in range(depth):
        if layer % 2 == 0:
            e = layer // 2
            wt, wkb, gq, gk = _prep_even(p['e_w_in'][e], p['e_q_norm'][e], p['e_k_norm'][e])
            qaT, ka, vaT, qbT, kb, vbT = _even_in(
                x, row(p['norm_mix'][layer]), wt, wkb, gq, gk, cos_a, sin_a)
            oa = _gqa(qaT, ka, vaT)
            lam_init = 0.8 - 0.6 * math.exp(-0.3 * layer)
            ob = _diff(slopes, row(p['e_lam_q1'][e]), row(p['e_lam_k1'][e]),
                       row(p['e_lam_q2'][e]), row(p['e_lam_k2'][e]),
                       p['e_subln'][e].reshape(B_VDIM, 1), qbT, kb, vbT, lam_init)
            w_out = p['e_w_out'][e].astype(BF16)
            mixes, weights = [oa, ob], [w_out[:A_Q], w_out[A_Q:]]
        else:
            o = layer // 2
            win, wuqT, wkn, wvT = _prep_odd(p['o_w_in'][o], p['o_w_uq'][o], p['o_w_ukv'][o])
            qT, k, vT = _odd_in(x, row(p['norm_mix'][layer]), win, row(p['o_q_norm'][o]),
                                row(p['o_kv_norm'][o]), wuqT, wkn, wvT,
                                cos_l.T, sin_l.T, cos_k, sin_k)
            mixes, weights = [_mla(qT, k, vT)], [p['o_w_out'][o].astype(BF16)]
        kv = _memkv(mem, row(p['norm_mem'][layer]), p['w_ckv'][layer].astype(BF16))
        x = _post_mix(x, mixes, weights, row(p['norm_cross'][layer]),
                      p['w_cq'][layer].astype(BF16), kv, p['w_co'][layer].astype(BF16))
        x = _ffn(x, row(p['norm_ffn'][layer]), p['w_gu'][layer].astype(BF16),
                 p['w_down'][layer].astype(BF16), row(p['final_norm']),
                 final_norm=(layer == depth - 1))
    return x


def kernel(x_prompt, x_sample, mem_prompt, mem_sample, norm_mix, e_w_in, e_q_norm, e_k_norm, e_lam_q1, e_lam_k1, e_lam_q2, e_lam_k2, e_subln, e_w_out, o_w_in, o_q_norm, o_kv_norm, o_w_uq, o_w_ukv, o_w_out, norm_cross, norm_mem, w_cq, w_ckv, w_co, norm_ffn, w_gu, w_down, final_norm):
    p = dict(norm_mix=norm_mix, e_w_in=e_w_in, e_q_norm=e_q_norm, e_k_norm=e_k_norm,
             e_lam_q1=e_lam_q1, e_lam_k1=e_lam_k1, e_lam_q2=e_lam_q2, e_lam_k2=e_lam_k2,
             e_subln=e_subln, e_w_out=e_w_out, o_w_in=o_w_in, o_q_norm=o_q_norm,
             o_kv_norm=o_kv_norm, o_w_uq=o_w_uq, o_w_ukv=o_w_ukv, o_w_out=o_w_out,
             norm_cross=norm_cross, norm_mem=norm_mem, w_cq=w_cq, w_ckv=w_ckv, w_co=w_co,
             norm_ffn=norm_ffn, w_gu=w_gu, w_down=w_down, final_norm=final_norm)
    return (_trunk(x_prompt, mem_prompt, p), _trunk(x_sample, mem_sample, p))
```

```python
import functools
import math

import jax
import jax.numpy as jnp
import numpy as np
from jax import lax
from jax.experimental import pallas as pl
from jax.experimental.pallas import tpu as pltpu

F32 = jnp.float32
BF16 = jnp.bfloat16

D_MODEL = 1024
GRID_W = 64
EPS = 1e-6
ROPE_THETA = 10000.0
A_HEADS, A_KV_HEADS, A_DIM = 8, 2, 64
A_GROUP = A_HEADS // A_KV_HEADS
B_HEADS, B_DIM = 4, 64
B_VDIM = 2 * B_DIM
A_Q = A_HEADS * A_DIM
A_KV = A_KV_HEADS * A_DIM
B_QK = B_HEADS * 2 * B_DIM
B_V = B_HEADS * B_VDIM
C_HEADS, C_Q_RANK, C_KV_RANK, C_NOPE, C_ROPE, C_VDIM = 16, 384, 256, 64, 32, 64
C_PAD = 128
X_HEADS = 4
X_DIM = D_MODEL // X_HEADS
D_FF = ((-(-8 * D_MODEL // 3) + 255) // 256) * 256
FF_CHUNK = 256
NEG_BIG = -1e30
LOG2E = math.log2(math.e)

TOKEN_TILE = 512
WIDE_TOKEN_TILE = 512
Q_TILE = 256
KEY_CHUNK = 256
VMEM_LIMIT = 48 * 1024 * 1024

_NT = (((1,), (1,)), ((), ()))


def _params(n_parallel, n_arbitrary=0):
    return pltpu.CompilerParams(
        dimension_semantics=("parallel",) * n_parallel + ("arbitrary",) * n_arbitrary,
        vmem_limit_bytes=VMEM_LIMIT)


def _rms_rows(x, g):
    ms = jnp.mean(x * x, axis=-1, keepdims=True)
    return (x * lax.rsqrt(ms + EPS)) * g


def _rms_cols(x, g):
    ms = jnp.mean(x * x, axis=0, keepdims=True)
    return (x * lax.rsqrt(ms + EPS)) * g


def _const_spec(shape):
    nd = len(shape)
    return pl.BlockSpec(shape, lambda *_: (0,) * nd)


ONES_ROWS = 16


def _store_values(ref, vT, n_heads, dv):
    ext = dv + ONES_ROWS
    ones = jnp.ones((ONES_ROWS, vT.shape[1]), BF16)
    for h in range(n_heads):
        ref[0, h * ext:h * ext + dv, :] = vT[h * dv:(h + 1) * dv].astype(BF16)
        ref[0, h * ext + dv:(h + 1) * ext, :] = ones


def _value_rows(h, dv):
    ext = dv + ONES_ROWS
    return slice(h * ext, (h + 1) * ext)


def _even_in_kernel(x_ref, g_ref, wt_ref, wkb_ref, gq_ref, gk_ref, cos_ref, sin_ref,
                    qaT_ref, ka_ref, vaT_ref, qbT_ref, kb_ref, vbT_ref):
    half = A_DIM // 2
    xn = _rms_rows(x_ref[0], g_ref[...]).astype(BF16)
    yT = lax.dot_general(wt_ref[...], xn, _NT, preferred_element_type=F32)
    cos = cos_ref[...]
    sin = sin_ref[...]

    def norm_rope(xh, g, scale):
        y = _rms_cols(xh, g)
        e, o = y[:half], y[half:]
        return jnp.concatenate([e * cos - o * sin, e * sin + o * cos], axis=0) * scale

    scale_a = A_DIM ** -0.5 * LOG2E
    for h in range(A_HEADS):
        qh = norm_rope(yT[h * A_DIM:(h + 1) * A_DIM], gq_ref[...], scale_a)
        qaT_ref[0, h * A_DIM:(h + 1) * A_DIM, :] = qh.astype(BF16)
    kT = jnp.concatenate(
        [norm_rope(yT[A_Q + h * A_DIM:A_Q + (h + 1) * A_DIM], gk_ref[...], 1.0)
         for h in range(A_KV_HEADS)], axis=0)
    ka_ref[0] = kT.T.astype(BF16)
    r0 = A_Q + A_KV
    _store_values(vaT_ref, yT[r0:r0 + A_KV], A_KV_HEADS, A_DIM)
    r0 += A_KV
    qbT_ref[0] = (yT[r0:r0 + B_QK] * (B_DIM ** -0.5 * LOG2E)).astype(BF16)
    r0 += B_QK
    _store_values(vbT_ref, yT[r0:r0 + B_V], B_HEADS, B_VDIM)
    kb_ref[0] = jnp.dot(xn, wkb_ref[...], preferred_element_type=F32).astype(BF16)


def _even_in(x, g, wt, wkb, gq, gk, cosT, sinT):
    B, S, D = x.shape
    tm = min(TOKEN_TILE, S)
    rows = wt.shape[0]
    va_rows = A_KV_HEADS * (A_DIM + ONES_ROWS)
    vb_rows = B_HEADS * (B_VDIM + ONES_ROWS)
    return pl.pallas_call(
        _even_in_kernel,
        grid=(B, S // tm),
        in_specs=[
            pl.BlockSpec((1, tm, D), lambda b, i: (b, i, 0)),
            _const_spec((1, D)),
            _const_spec((rows, D)),
            _const_spec((D, B_QK)),
            _const_spec((A_DIM, 1)),
            _const_spec((A_DIM, 1)),
            pl.BlockSpec((A_DIM // 2, tm), lambda b, i: (0, i)),
            pl.BlockSpec((A_DIM // 2, tm), lambda b, i: (0, i)),
        ],
        out_specs=[
            pl.BlockSpec((1, A_Q, tm), lambda b, i: (b, 0, i)),
            pl.BlockSpec((1, tm, A_KV), lambda b, i: (b, i, 0)),
            pl.BlockSpec((1, va_rows, tm), lambda b, i: (b, 0, i)),
            pl.BlockSpec((1, B_QK, tm), lambda b, i: (b, 0, i)),
            pl.BlockSpec((1, tm, B_QK), lambda b, i: (b, i, 0)),
            pl.BlockSpec((1, vb_rows, tm), lambda b, i: (b, 0, i)),
        ],
        out_shape=[
            jax.ShapeDtypeStruct((B, A_Q, S), BF16),
            jax.ShapeDtypeStruct((B, S, A_KV), BF16),
            jax.ShapeDtypeStruct((B, va_rows, S), BF16),
            jax.ShapeDtypeStruct((B, B_QK, S), BF16),
            jax.ShapeDtypeStruct((B, S, B_QK), BF16),
            jax.ShapeDtypeStruct((B, vb_rows, S), BF16),
        ],
        compiler_params=_params(2),
        name="even_in",
    )(x, g, wt, wkb, gq, gk, cosT, sinT)


def _chain_init(n, dv, tq):
    return tuple((jnp.full((1, tq), NEG_BIG, F32), jnp.zeros((dv + ONES_ROWS, tq), F32))
                 for _ in range(n))


def _chain_out(acc, dv):
    return acc[:dv] / acc[dv:dv + 1]


def _chunk_start(c, tkc):
    return c * tkc if isinstance(c, int) else pl.multiple_of(c * tkc, tkc)


def _key_chunk(S):
    return min(KEY_CHUNK, S // 2)


def _attend(n_chunks, n_chains, dv, tq, first_tile, prep, score, value, scratch, offset=None):
    assert n_chunks % 2 == 0
    max_ref, s_bufs = scratch[0], (scratch[1:1 + n_chains], scratch[1 + n_chains:])

    def produce(ctx, j, nxt, slot):
        s = score(ctx, j, nxt)
        s_bufs[slot][j][...] = s
        top = jnp.max(s, axis=0, keepdims=True)
        r = None if offset is None else offset(ctx, j, nxt)
        return (top, None) if r is None else (top - r, r)

    @pl.when(first_tile)
    def _():
        ctx0 = prep(0, False)
        for j in range(n_chains):
            max_ref[j] = produce(ctx0, j, False, 0)[0]

    maxes = [(max_ref[j], None) for j in range(n_chains)]
    chains = list(_chain_init(n_chains, dv, tq))
    for c in range(n_chunks):
        slot = c % 2
        nxt = c == n_chunks - 1
        ctx = prep(0 if nxt else c + 1, nxt)
        for j in range(n_chains):
            produced = produce(ctx, j, nxt, 1 - slot)
            m, acc = chains[j]
            top, r = maxes[j]
            m_new = jnp.maximum(m, top)
            alpha = jnp.exp2(m - m_new)
            shift = m_new if r is None else m_new + r
            p = jnp.exp2(s_bufs[slot][j][...] - shift).astype(BF16)
            acc = alpha * acc + jnp.dot(value(c, j), p, preferred_element_type=F32)
            chains[j] = (m_new, acc)
            maxes[j] = produced
    for j in range(n_chains):
        max_ref[j] = maxes[j][0]
    return [_chain_out(acc, dv) for _, acc in chains]


def _attend_scratch(n_chains, tkc, tq):
    return ([pltpu.VMEM((n_chains, 1, tq), F32)]
            + [pltpu.VMEM((tkc, tq), F32) for _ in range(2 * n_chains)])


def _next_tile(n_tiles):
    return lambda i: jnp.minimum(i + 1, n_tiles - 1)


def _gqa_kernel(qT_ref, qT_next_ref, k_ref, vT_ref, o_ref, *scratch, tkc):
    S = k_ref.shape[1]
    tq = qT_ref.shape[2]

    def padded_queries(ref):
        pads = []
        for h in range(A_HEADS):
            q = ref[0, h * A_DIM:(h + 1) * A_DIM, :]
            zero = jnp.zeros_like(q)
            pads.append(jnp.concatenate([q, zero] if h < A_GROUP else [zero, q], axis=0))
        return pads

    q_pads = {False: padded_queries(qT_ref), True: padded_queries(qT_next_ref)}

    def prep(c, nxt):
        return k_ref[0, pl.ds(_chunk_start(c, tkc), tkc), :]

    def score(k_c, h, nxt):
        return jnp.dot(k_c, q_pads[nxt][h], preferred_element_type=F32)

    def value(c, h):
        g = h // A_GROUP
        return vT_ref[0, _value_rows(g, A_DIM), pl.ds(_chunk_start(c, tkc), tkc)]

    outs = _attend(S // tkc, A_HEADS, A_DIM, tq, pl.program_id(1) == 0,
                   prep, score, value, scratch)
    o_ref[0] = jnp.concatenate(outs, axis=0).T.astype(o_ref.dtype)


def _gqa(qaT, ka, vaT):
    B, _, S = qaT.shape
    tq = min(Q_TILE, S)
    tkc = _key_chunk(S)
    nxt = _next_tile(S // tq)
    return pl.pallas_call(
        functools.partial(_gqa_kernel, tkc=tkc),
        grid=(B, S // tq),
        in_specs=[
            pl.BlockSpec((1, A_Q, tq), lambda b, i: (b, 0, i)),
            pl.BlockSpec((1, A_Q, tq), lambda b, i: (b, 0, nxt(i))),
            pl.BlockSpec((1, S, A_KV), lambda b, i: (b, 0, 0)),
            pl.BlockSpec((1, A_KV_HEADS * (A_DIM + ONES_ROWS), S), lambda b, i: (b, 0, 0)),
        ],
        out_specs=pl.BlockSpec((1, tq, A_Q), lambda b, i: (b, i, 0)),
        out_shape=jax.ShapeDtypeStruct((B, S, A_Q), BF16),
        scratch_shapes=_attend_scratch(A_HEADS, tkc, tq),
        compiler_params=_params(1, 1),
        name="gqa_attn",
    )(qaT, qaT, ka, vaT)


DIFF_HEADS_PER_STEP = 4


ALIBI_PIECES = 3


def _diff_kernel(slopes_ref, lq1_ref, lk1_ref, lq2_ref, lk2_ref, gsub_ref,
                 qT_ref, qT_next_ref, k_ref, kpos_ref, vT_ref, o_ref, *scratch, tkc, lam_init):
    hp = pl.program_id(1)
    i = pl.program_id(2)
    S = k_ref.shape[1]
    tq = qT_ref.shape[2]
    n_chunks = S // tkc
    n_tiles = S // tq
    pair = 2 * B_DIM
    coefs = [slopes_ref[DIFF_HEADS_PER_STEP * hp + hh] * LOG2E for hh in range(DIFF_HEADS_PER_STEP)]

    def padded_queries(ref):
        pads = []
        for hh in range(DIFF_HEADS_PER_STEP):
            q = ref[0, hh * pair:(hh + 1) * pair, :]
            rows = lax.broadcasted_iota(jnp.int32, q.shape, 0)
            zero = jnp.zeros_like(q)
            pads += [jnp.where(rows < B_DIM, q, zero), jnp.where(rows >= B_DIM, q, zero)]
        return pads

    q_pads = {False: padded_queries(qT_ref), True: padded_queries(qT_next_ref)}
    dmat = (lax.broadcasted_iota(jnp.int32, (tkc, tq), 1)
            - lax.broadcasted_iota(jnp.int32, (tkc, tq), 0)).astype(F32)
    q_local = lax.broadcasted_iota(jnp.int32, (1, tq), 1).astype(F32)
    piece_rows = lax.broadcasted_iota(jnp.int32, (pair, tq), 0) < ALIBI_PIECES

    def chunk_of(c, tile):
        home = (tile * tq) // tkc
        return home, (home if c == 0 else lax.rem(home + c, n_chunks))

    def prep(c, nxt):
        tile = jnp.minimum(i + 1, n_tiles - 1) if nxt else i
        home, chunk = chunk_of(c, tile)
        start = pl.multiple_of(chunk * tkc, tkc)
        q_start = (tile * tq).astype(F32)
        if c == 0:
            dist = jnp.abs(dmat + (q_start - start.astype(F32)))
            return dict(start=start, biases=[dist * -cf for cf in coefs])
        sign = jnp.where(chunk < home, 1.0, -1.0)
        ext = jnp.where(piece_rows, sign, 0.0).astype(BF16)
        q_pos = q_start + q_local
        return dict(start=start, ext=ext, offsets=[(sign * cf) * q_pos for cf in coefs])

    def score(ctx, n, nxt):
        hh = n // 2
        k_c = k_ref[0, pl.ds(ctx["start"], tkc), hh * pair:(hh + 1) * pair]
        q_pad = q_pads[nxt][n]
        if "biases" in ctx:
            return jnp.dot(k_c, q_pad, preferred_element_type=F32) + ctx["biases"][hh]
        keys = jnp.concatenate([k_c, kpos_ref[hh, pl.ds(ctx["start"], tkc), :]], axis=1)
        queries = jnp.concatenate([q_pad, ctx["ext"]], axis=0)
        return jnp.dot(keys, queries, preferred_element_type=F32)

    def offset(ctx, n, nxt):
        return None if "biases" in ctx else ctx["offsets"][n // 2]

    def value(c, n):
        _, chunk = chunk_of(c, i)
        start = pl.multiple_of(chunk * tkc, tkc)
        return vT_ref[0, _value_rows(n // 2, B_VDIM), pl.ds(start, tkc)]

    n_chains = 2 * DIFF_HEADS_PER_STEP
    parts = _attend(n_chunks, n_chains, B_VDIM, tq, i == 0, prep, score, value, scratch, offset)

    lam = (jnp.exp(jnp.sum(lq1_ref[...] * lk1_ref[...], axis=-1, keepdims=True))
           - jnp.exp(jnp.sum(lq2_ref[...] * lk2_ref[...], axis=-1, keepdims=True)) + lam_init)
    outs = []
    for hh in range(DIFF_HEADS_PER_STEP):
        o = parts[2 * hh] - lam * parts[2 * hh + 1]
        outs.append(_rms_cols(o, gsub_ref[...]) * (1.0 - lam_init))
    o_ref[0] = jnp.concatenate(outs, axis=0).T.astype(o_ref.dtype)


def _alibi_key_table(slopes, S):
    a = (slopes * LOG2E)[:, None] * jnp.arange(S, dtype=F32)[None, :]
    pieces, rest = [], a
    for _ in range(ALIBI_PIECES):
        piece = lax.bitcast_convert_type(
            lax.bitcast_convert_type(rest, jnp.uint32) & jnp.uint32(0xFFFF0000), F32)
        pieces.append(piece.astype(BF16))
        rest = rest - piece
    table = jnp.stack(pieces, axis=-1)
    return jnp.pad(table, ((0, 0), (0, 0), (0, 2 * B_DIM - ALIBI_PIECES)))


def _diff(slopes, lq1, lk1, lq2, lk2, gsub, qbT, kb, vbT, lam_init):
    B, _, S = qbT.shape
    tq = min(Q_TILE, S)
    tkc = _key_chunk(S)
    assert tkc % tq == 0
    hs = DIFF_HEADS_PER_STEP
    nxt = _next_tile(S // tq)
    kpos = _alibi_key_table(slopes, S)
    return pl.pallas_call(
        functools.partial(_diff_kernel, tkc=tkc, lam_init=lam_init),
        grid=(B, B_HEADS // hs, S // tq),
        in_specs=[
            pl.BlockSpec(memory_space=pltpu.SMEM),
            _const_spec((1, B_DIM)), _const_spec((1, B_DIM)),
            _const_spec((1, B_DIM)), _const_spec((1, B_DIM)),
            _const_spec((B_VDIM, 1)),
            pl.BlockSpec((1, hs * 2 * B_DIM, tq), lambda b, h, i: (b, h, i)),
            pl.BlockSpec((1, hs * 2 * B_DIM, tq), lambda b, h, i: (b, h, nxt(i))),
            pl.BlockSpec((1, S, hs * 2 * B_DIM), lambda b, h, i: (b, 0, h)),
            pl.BlockSpec((hs, S, 2 * B_DIM), lambda b, h, i: (h, 0, 0)),
            pl.BlockSpec((1, hs * (B_VDIM + ONES_ROWS), S), lambda b, h, i: (b, h, 0)),
        ],
        out_specs=pl.BlockSpec((1, tq, hs * B_VDIM), lambda b, h, i: (b, i, h)),
        out_shape=jax.ShapeDtypeStruct((B, S, B_V), BF16),
        scratch_shapes=_attend_scratch(2 * hs, tkc, tq),
        compiler_params=_params(2, 1),
        name="diff_attn",
    )(slopes, lq1, lk1, lq2, lk2, gsub, qbT, qbT, kb, kpos, vbT)


MLA_HEADS_PER_STEP = 4


def _mla_kernel(qT_ref, qT_next_ref, k_ref, vT_ref, o_ref, *scratch, tkc):
    S = k_ref.shape[1]
    tq = qT_ref.shape[2]
    nh = MLA_HEADS_PER_STEP
    q_refs = {False: qT_ref, True: qT_next_ref}

    def prep(c, nxt):
        return _chunk_start(c, tkc)

    def score(start, j, nxt):
        k_c = k_ref[0, pl.ds(start, tkc), j * C_PAD:(j + 1) * C_PAD]
        q_pad = q_refs[nxt][0, j * C_PAD:(j + 1) * C_PAD, :]
        return jnp.dot(k_c, q_pad, preferred_element_type=F32)

    def value(c, j):
        return vT_ref[0, _value_rows(j, C_VDIM), pl.ds(_chunk_start(c, tkc), tkc)]

    outs = _attend(S // tkc, nh, C_VDIM, tq, pl.program_id(2) == 0, prep, score, value, scratch)
    o_ref[0] = jnp.concatenate(outs, axis=0).T.astype(o_ref.dtype)


def _mla(qT, k, vT):
    B, _, S = qT.shape
    tq = min(Q_TILE, S)
    tkc = _key_chunk(S)
    hp = MLA_HEADS_PER_STEP
    nxt = _next_tile(S // tq)
    return pl.pallas_call(
        functools.partial(_mla_kernel, tkc=tkc),
        grid=(B, C_HEADS // hp, S // tq),
        in_specs=[
            pl.BlockSpec((1, hp * C_PAD, tq), lambda b, h, i: (b, h, i)),
            pl.BlockSpec((1, hp * C_PAD, tq), lambda b, h, i: (b, h, nxt(i))),
            pl.BlockSpec((1, S, hp * C_PAD), lambda b, h, i: (b, 0, h)),
            pl.BlockSpec((1, hp * (C_VDIM + ONES_ROWS), S), lambda b, h, i: (b, h, 0)),
        ],
        out_specs=pl.BlockSpec((1, tq, hp * C_VDIM), lambda b, h, i: (b, i, h)),
        out_shape=jax.ShapeDtypeStruct((B, S, C_HEADS * C_VDIM), BF16),
        scratch_shapes=_attend_scratch(hp, tkc, tq),
        compiler_params=_params(2, 1),
        name="mla_attn",
    )(qT, qT, k, vT)


def _odd_in_kernel(x_ref, g_ref, win_ref, gq_ref, gkv_ref, wuqT_ref, wkn_ref, wvT_ref,
                   cosq_ref, sinq_ref, cosk_ref, sink_ref, qT_ref, k_ref, vT_ref):
    hr = C_ROPE // 2
    xn = _rms_rows(x_ref[0], g_ref[...]).astype(BF16)
    a = jnp.dot(xn, win_ref[...], preferred_element_type=F32)
    cqn = _rms_rows(a[:, :C_Q_RANK], gq_ref[...]).astype(BF16)
    ckvn = _rms_rows(a[:, C_Q_RANK:C_Q_RANK + C_KV_RANK], gkv_ref[...]).astype(BF16)

    qT = lax.dot_general(wuqT_ref[...], cqn, _NT, preferred_element_type=F32)
    qT = qT * ((C_NOPE + C_ROPE) ** -0.5 * LOG2E)
    cq, sq = cosq_ref[...], sinq_ref[...]
    for h in range(C_HEADS):
        r = h * C_PAD
        e = qT[r + C_NOPE:r + C_NOPE + hr]
        o = qT[r + C_NOPE + hr:r + C_NOPE + C_ROPE]
        head = jnp.concatenate(
            [qT[r:r + C_NOPE], e * cq - o * sq, e * sq + o * cq, qT[r + C_NOPE + C_ROPE:r + C_PAD]],
            axis=0)
        qT_ref[0, r:r + C_PAD, :] = head.astype(BF16)

    kblk = a[:, C_Q_RANK + C_KV_RANK:]
    t = kblk * cosk_ref[...] + kblk * sink_ref[...]
    lane = lax.broadcasted_iota(jnp.int32, t.shape, 1)
    kr = jnp.where(lane < C_ROPE, t + pltpu.roll(t, C_PAD - C_ROPE, axis=1), 0.0)
    kr = pltpu.roll(kr, C_NOPE, axis=1)

    kn = jnp.dot(ckvn, wkn_ref[...], preferred_element_type=F32)
    for h in range(C_HEADS):
        k_ref[0, :, h * C_PAD:(h + 1) * C_PAD] = (kn[:, h * C_PAD:(h + 1) * C_PAD] + kr).astype(BF16)

    _store_values(vT_ref, lax.dot_general(wvT_ref[...], ckvn, _NT, preferred_element_type=F32),
                  C_HEADS, C_VDIM)


def _odd_in(x, g, win, gq, gkv, wuqT, wkn, wvT, cosq, sinq, cosk, sink):
    B, S, D = x.shape
    tm = min(TOKEN_TILE, S)
    hr = C_ROPE // 2
    return pl.pallas_call(
        _odd_in_kernel,
        grid=(B, S // tm),
        in_specs=[
            pl.BlockSpec((1, tm, D), lambda b, i: (b, i, 0)),
            _const_spec((1, D)),
            _const_spec(win.shape),
            _const_spec((1, C_Q_RANK)),
            _const_spec((1, C_KV_RANK)),
            _const_spec(wuqT.shape),
            _const_spec(wkn.shape),
            _const_spec(wvT.shape),
            pl.BlockSpec((hr, tm), lambda b, i: (0, i)),
            pl.BlockSpec((hr, tm), lambda b, i: (0, i)),
            pl.BlockSpec((tm, C_PAD), lambda b, i: (i, 0)),
            pl.BlockSpec((tm, C_PAD), lambda b, i: (i, 0)),
        ],
        out_specs=[
            pl.BlockSpec((1, C_HEADS * C_PAD, tm), lambda b, i: (b, 0, i)),
            pl.BlockSpec((1, tm, C_HEADS * C_PAD), lambda b, i: (b, i, 0)),
            pl.BlockSpec((1, C_HEADS * (C_VDIM + ONES_ROWS), tm), lambda b, i: (b, 0, i)),
        ],
        out_shape=[
            jax.ShapeDtypeStruct((B, C_HEADS * C_PAD, S), BF16),
            jax.ShapeDtypeStruct((B, S, C_HEADS * C_PAD), BF16),
            jax.ShapeDtypeStruct((B, C_HEADS * (C_VDIM + ONES_ROWS), S), BF16),
        ],
        compiler_params=_params(2),
        name="odd_in",
    )(x, g, win, gq, gkv, wuqT, wkn, wvT, cosq, sinq, cosk, sink)


def _memkv_kernel(mem_ref, g_ref, w_ref, kv_ref):
    mn = _rms_rows(mem_ref[0], g_ref[...]).astype(BF16)
    kv_ref[0] = jnp.dot(mn, w_ref[...], preferred_element_type=F32).astype(BF16)


def _memkv(mem, g, w):
    B, M, D = mem.shape
    N = w.shape[1]
    return pl.pallas_call(
        _memkv_kernel,
        grid=(B,),
        in_specs=[pl.BlockSpec((1, M, D), lambda b: (b, 0, 0)), _const_spec((1, D)), _const_spec((D, N))],
        out_specs=pl.BlockSpec((1, M, N), lambda b: (b, 0, 0)),
        out_shape=jax.ShapeDtypeStruct((B, M, N), BF16),
        compiler_params=_params(1),
        name="mem_kv",
    )(mem, g, w)


def _post_mix_kernel(*refs, n_mix):
    x_ref = refs[0]
    o_refs = refs[1:1 + n_mix]
    w_refs = refs[1 + n_mix:1 + 2 * n_mix]
    gc_ref, wq_ref, kv_ref, wo_ref, out_ref = refs[1 + 2 * n_mix:]
    x = x_ref[0]
    for o_ref, w_ref in zip(o_refs, w_refs):
        x = x + jnp.dot(o_ref[0], w_ref[...], preferred_element_type=F32)

    hc = _rms_rows(x, gc_ref[...]).astype(BF16)
    q = (jnp.dot(hc, wq_ref[...], preferred_element_type=F32) * (X_DIM ** -0.5)).astype(BF16)
    heads = []
    for h in range(X_HEADS):
        k_h = kv_ref[0, :, h * X_DIM:(h + 1) * X_DIM]
        v_h = kv_ref[0, :, D_MODEL + h * X_DIM:D_MODEL + (h + 1) * X_DIM]
        s = lax.dot_general(q[:, h * X_DIM:(h + 1) * X_DIM], k_h, _NT, preferred_element_type=F32)
        p = jnp.exp(s - jnp.max(s, axis=-1, keepdims=True))
        l = jnp.sum(p, axis=-1, keepdims=True)
        heads.append((jnp.dot(p.astype(BF16), v_h, preferred_element_type=F32) / l).astype(BF16))
    o = jnp.concatenate(heads, axis=-1)
    out_ref[0] = x + jnp.dot(o, wo_ref[...], preferred_element_type=F32)


def _post_mix(x, mixes, weights, gc, wq, kv, wo):
    B, S, D = x.shape
    tm = min(WIDE_TOKEN_TILE, S)
    n = len(mixes)
    M = kv.shape[1]
    tok = lambda b, i: (b, i, 0)
    return pl.pallas_call(
        functools.partial(_post_mix_kernel, n_mix=n),
        grid=(B, S // tm),
        in_specs=([pl.BlockSpec((1, tm, D), tok)]
                  + [pl.BlockSpec((1, tm, m.shape[2]), tok) for m in mixes]
                  + [_const_spec(w.shape) for w in weights]
                  + [_const_spec((1, D)), _const_spec((D, D)),
                     pl.BlockSpec((1, M, 2 * D), lambda b, i: (b, 0, 0)),
                     _const_spec((D, D))]),
        out_specs=pl.BlockSpec((1, tm, D), tok),
        out_shape=jax.ShapeDtypeStruct((B, S, D), F32),
        compiler_params=_params(2),
        name="post_mix",
    )(x, *mixes, *weights, gc, wq, kv, wo)


def _ffn_kernel(x_ref, g_ref, wgu_ref, wd_ref, gf_ref, out_ref, *, final_norm):
    x = x_ref[0]
    xn = _rms_rows(x, g_ref[...]).astype(BF16)
    acc = x
    for c in range(D_FF // FF_CHUNK):
        lo = c * FF_CHUNK
        gate = jnp.dot(xn, wgu_ref[:, lo:lo + FF_CHUNK], preferred_element_type=F32)
        up = jnp.dot(xn, wgu_ref[:, D_FF + lo:D_FF + lo + FF_CHUNK], preferred_element_type=F32)
        hidden = (jax.nn.silu(gate) * up).astype(BF16)
        acc = acc + jnp.dot(hidden, wd_ref[lo:lo + FF_CHUNK, :], preferred_element_type=F32)
    if final_norm:
        acc = _rms_rows(acc, gf_ref[...])
    out_ref[0] = acc


def _ffn(x, g, wgu, wd, gf, final_norm):
    B, S, D = x.shape
    tm = min(WIDE_TOKEN_TILE, S)
    tok = lambda b, i: (b, i, 0)
    return pl.pallas_call(
        functools.partial(_ffn_kernel, final_norm=final_norm),
        grid=(B, S // tm),
        in_specs=[pl.BlockSpec((1, tm, D), tok), _const_spec((1, D)),
                  _const_spec(wgu.shape), _const_spec(wd.shape), _const_spec((1, D))],
        out_specs=pl.BlockSpec((1, tm, D), tok),
        out_shape=jax.ShapeDtypeStruct((B, S, D), F32),
        compiler_params=_params(2),
        name="ffn",
    )(x, g, wgu, wd, gf)


def _rope_freqs(n_pairs):
    return ROPE_THETA ** (-jnp.arange(n_pairs, dtype=F32) / n_pairs)


def _axial_angles(S):
    rows = S // GRID_W
    r = jnp.repeat(jnp.arange(rows, dtype=F32), GRID_W)
    c = jnp.tile(jnp.arange(GRID_W, dtype=F32), rows)
    f = _rope_freqs(A_DIM // 4)
    return jnp.concatenate([r[:, None] * f, c[:, None] * f], axis=-1)


def _linear_angles(S, dim):
    t = jnp.arange(S, dtype=F32)
    return t[:, None] * _rope_freqs(dim // 2)


def _deinterleave(n):
    return np.concatenate([np.arange(0, n, 2), np.arange(1, n, 2)])


def _prep_even(w_in, gq, gk):
    perm = _deinterleave(A_DIM)
    o = 0
    w_qa = w_in[:, o:o + A_Q].reshape(D_MODEL, A_HEADS, A_DIM)[:, :, perm].reshape(D_MODEL, A_Q)
    o += A_Q
    w_ka = w_in[:, o:o + A_KV].reshape(D_MODEL, A_KV_HEADS, A_DIM)[:, :, perm].reshape(D_MODEL, A_KV)
    o += A_KV
    w_va = w_in[:, o:o + A_KV]
    o += A_KV
    w_qb = w_in[:, o:o + B_QK]
    o += B_QK
    w_kb = w_in[:, o:o + B_QK]
    o += B_QK
    w_vb = w_in[:, o:o + B_V]
    wt = jnp.concatenate([w_qa, w_ka, w_va, w_qb, w_vb], axis=1).T.astype(BF16)
    return wt, w_kb.astype(BF16), gq[perm].reshape(A_DIM, 1), gk[perm].reshape(A_DIM, 1)


def _prep_odd(w_in, w_uq, w_ukv):
    hr = C_ROPE // 2
    perm = _deinterleave(C_ROPE)
    w_kr = w_in[:, C_Q_RANK + C_KV_RANK:][:, perm]
    w_kr_rot = jnp.concatenate([-w_kr[:, hr:], w_kr[:, :hr]], axis=1)
    win = jnp.concatenate(
        [w_in[:, :C_Q_RANK + C_KV_RANK], w_kr, w_kr_rot,
         jnp.zeros((D_MODEL, C_PAD - 2 * C_ROPE), F32)], axis=1).astype(BF16)
    wq = w_uq.reshape(C_Q_RANK, C_HEADS, C_NOPE + C_ROPE)
    wq = jnp.concatenate(
        [wq[:, :, :C_NOPE], wq[:, :, C_NOPE:][:, :, perm],
         jnp.zeros((C_Q_RANK, C_HEADS, C_PAD - C_NOPE - C_ROPE), F32)], axis=2)
    wuqT = wq.reshape(C_Q_RANK, C_HEADS * C_PAD).T.astype(BF16)
    wkv = w_ukv.reshape(C_KV_RANK, C_HEADS, C_NOPE + C_VDIM)
    wkn = jnp.concatenate(
        [wkv[:, :, :C_NOPE], jnp.zeros((C_KV_RANK, C_HEADS, C_PAD - C_NOPE), F32)], axis=2)
    wkn = wkn.reshape(C_KV_RANK, C_HEADS * C_PAD).astype(BF16)
    wvT = wkv[:, :, C_NOPE:].reshape(C_KV_RANK, C_HEADS * C_VDIM).T.astype(BF16)
    return win, wuqT, wkn, wvT


def _trunk(x, mem, p):
    B, S, D = x.shape
    depth = p['norm_mix'].shape[0]
    row = lambda v: v.reshape(1, -1)

    ang_a = _axial_angles(S)
    cos_a, sin_a = jnp.cos(ang_a).T, jnp.sin(ang_a).T
    ang_l = _linear_angles(S, C_ROPE)
    cos_l, sin_l = jnp.cos(ang_l), jnp.sin(ang_l)
    zpad = jnp.zeros((S, C_PAD - 2 * C_ROPE), F32)
    zrope = jnp.zeros((S, C_ROPE), F32)
    cos_k = jnp.concatenate([cos_l, cos_l, zrope, zpad], axis=1)
    sin_k = jnp.concatenate([zrope, sin_l, sin_l, zpad], axis=1)
    slopes = jnp.asarray(2.0 ** (-8.0 * np.arange(1, B_HEADS + 1) / B_HEADS), dtype=F32)

    for layer in range(depth):
        if layer % 2 == 0:
            e = layer // 2
            wt, wkb, gq, gk = _prep_even(p['e_w_in'][e], p['e_q_norm'][e], p['e_k_norm'][e])
            qaT, ka, vaT, qbT, kb, vbT = _even_in(
                x, row(p['norm_mix'][layer]), wt, wkb, gq, gk, cos_a, sin_a)
            oa = _gqa(qaT, ka, vaT)
            lam_init = 0.8 - 0.6 * math.exp(-0.3 * layer)
            ob = _diff(slopes, row(p['e_lam_q1'][e]), row(p['e_lam_k1'][e]),
                       row(p['e_lam_q2'][e]), row(p['e_lam_k2'][e]),
                       p['e_subln'][e].reshape(B_VDIM, 1), qbT, kb, vbT, lam_init)
            w_out = p['e_w_out'][e].astype(BF16)
            mixes, weights = [oa, ob], [w_out[:A_Q], w_out[A_Q:]]
        else:
            o = layer // 2
            win, wuqT, wkn, wvT = _prep_odd(p['o_w_in'][o], p['o_w_uq'][o], p['o_w_ukv'][o])
            qT, k, vT = _odd_in(x, row(p['norm_mix'][layer]), win, row(p['o_q_norm'][o]),
                                row(p['o_kv_norm'][o]), wuqT, wkn, wvT,
                                cos_l.T, sin_l.T, cos_k, sin_k)
            mixes, weights = [_mla(qT, k, vT)], [p['o_w_out'][o].astype(BF16)]
        kv = _memkv(mem, row(p['norm_mem'][layer]), p['w_ckv'][layer].astype(BF16))
        x = _post_mix(x, mixes, weights, row(p['norm_cross'][layer]),
                      p['w_cq'][layer].astype(BF16), kv, p['w_co'][layer].astype(BF16))
        x = _ffn(x, row(p['norm_ffn'][layer]), p['w_gu'][layer].astype(BF16),
                 p['w_down'][layer].astype(BF16), row(p['final_norm']),
                 final_norm=(layer == depth - 1))
    return x


def kernel(x_prompt, x_sample, mem_prompt, mem_sample, norm_mix, e_w_in, e_q_norm, e_k_norm, e_lam_q1, e_lam_k1, e_lam_q2, e_lam_k2, e_subln, e_w_out, o_w_in, o_q_norm, o_kv_norm, o_w_uq, o_w_ukv, o_w_out, norm_cross, norm_mem, w_cq, w_ckv, w_co, norm_ffn, w_gu, w_down, final_norm):
    p = dict(norm_mix=norm_mix, e_w_in=e_w_in, e_q_norm=e_q_norm, e_k_norm=e_k_norm,
             e_lam_q1=e_lam_q1, e_lam_k1=e_lam_k1, e_lam_q2=e_lam_q2, e_lam_k2=e_lam_k2,
             e_subln=e_subln, e_w_out=e_w_out, o_w_in=o_w_in, o_q_norm=o_q_norm,
             o_kv_norm=o_kv_norm, o_w_uq=o_w_uq, o_w_ukv=o_w_ukv, o_w_out=o_w_out,
             norm_cross=norm_cross, norm_mem=norm_mem, w_cq=w_cq, w_ckv=w_ckv, w_co=w_co,
             norm_ffn=norm_ffn, w_gu=w_gu, w_down=w_down, final_norm=final_norm)
    return (_trunk(x_prompt, mem_prompt, p), _trunk(x_sample, mem_sample, p))
```

```python
import functools
import math

import jax
import jax.numpy as jnp
import numpy as np
from jax import lax
from jax.experimental import pallas as pl
from jax.experimental.pallas import tpu as pltpu

F32 = jnp.float32
BF16 = jnp.bfloat16

D_MODEL = 1024
GRID_W = 64
EPS = 1e-6
ROPE_THETA = 10000.0
A_HEADS, A_KV_HEADS, A_DIM = 8, 2, 64
A_GROUP = A_HEADS // A_KV_HEADS
B_HEADS, B_DIM = 4, 64
B_VDIM = 2 * B_DIM
A_Q = A_HEADS * A_DIM
A_KV = A_KV_HEADS * A_DIM
B_QK = B_HEADS * 2 * B_DIM
B_V = B_HEADS * B_VDIM
C_HEADS, C_Q_RANK, C_KV_RANK, C_NOPE, C_ROPE, C_VDIM = 16, 384, 256, 64, 32, 64
C_PAD = 128
X_HEADS = 4
X_DIM = D_MODEL // X_HEADS
D_FF = ((-(-8 * D_MODEL // 3) + 255) // 256) * 256
FF_CHUNK = 256
NEG_BIG = -1e30
LOG2E = math.log2(math.e)

TOKEN_TILE = 512
WIDE_TOKEN_TILE = 512
Q_TILE = 256
KEY_CHUNK = 256
VMEM_LIMIT = 48 * 1024 * 1024

_NT = (((1,), (1,)), ((), ()))


def _params(n_parallel, n_arbitrary=0):
    return pltpu.CompilerParams(
        dimension_semantics=("parallel",) * n_parallel + ("arbitrary",) * n_arbitrary,
        vmem_limit_bytes=VMEM_LIMIT)


def _rms_rows(x, g):
    ms = jnp.mean(x * x, axis=-1, keepdims=True)
    return (x * lax.rsqrt(ms + EPS)) * g


def _rms_cols(x, g):
    ms = jnp.mean(x * x, axis=0, keepdims=True)
    return (x * lax.rsqrt(ms + EPS)) * g


def _const_spec(shape):
    nd = len(shape)
    return pl.BlockSpec(shape, lambda *_: (0,) * nd)


ONES_ROWS = 16


def _store_values(ref, vT, n_heads, dv):
    ext = dv + ONES_ROWS
    ones = jnp.ones((ONES_ROWS, vT.shape[1]), BF16)
    for h in range(n_heads):
        ref[0, h * ext:h * ext + dv, :] = vT[h * dv:(h + 1) * dv].astype(BF16)
        ref[0, h * ext + dv:(h + 1) * ext, :] = ones


def _value_rows(h, dv):
    ext = dv + ONES_ROWS
    return slice(h * ext, (h + 1) * ext)


def _even_in_kernel(x_ref, g_ref, wt_ref, wkb_ref, gq_ref, gk_ref, cos_ref, sin_ref,
                    qaT_ref, ka_ref, vaT_ref, qbT_ref, kb_ref, vbT_ref):
    half = A_DIM // 2
    xn = _rms_rows(x_ref[0], g_ref[...]).astype(BF16)
    yT = lax.dot_general(wt_ref[...], xn, _NT, preferred_element_type=F32)
    cos = cos_ref[...]
    sin = sin_ref[...]

    def norm_rope(xh, g, scale):
        y = _rms_cols(xh, g)
        e, o = y[:half], y[half:]
        return jnp.concatenate([e * cos - o * sin, e * sin + o * cos], axis=0) * scale

    scale_a = A_DIM ** -0.5 * LOG2E
    for h in range(A_HEADS):
        qh = norm_rope(yT[h * A_DIM:(h + 1) * A_DIM], gq_ref[...], scale_a)
        qaT_ref[0, h * A_DIM:(h + 1) * A_DIM, :] = qh.astype(BF16)
    kT = jnp.concatenate(
        [norm_rope(yT[A_Q + h * A_DIM:A_Q + (h + 1) * A_DIM], gk_ref[...], 1.0)
         for h in range(A_KV_HEADS)], axis=0)
    ka_ref[0] = kT.T.astype(BF16)
    r0 = A_Q + A_KV
    _store_values(vaT_ref, yT[r0:r0 + A_KV], A_KV_HEADS, A_DIM)
    r0 += A_KV
    qbT_ref[0] = (yT[r0:r0 + B_QK] * (B_DIM ** -0.5 * LOG2E)).astype(BF16)
    r0 += B_QK
    _store_values(vbT_ref, yT[r0:r0 + B_V], B_HEADS, B_VDIM)
    kb_ref[0] = jnp.dot(xn, wkb_ref[...], preferred_element_type=F32).astype(BF16)


def _even_in(x, g, wt, wkb, gq, gk, cosT, sinT):
    B, S, D = x.shape
    tm = min(TOKEN_TILE, S)
    rows = wt.shape[0]
    va_rows = A_KV_HEADS * (A_DIM + ONES_ROWS)
    vb_rows = B_HEADS * (B_VDIM + ONES_ROWS)
    return pl.pallas_call(
        _even_in_kernel,
        grid=(B, S // tm),
        in_specs=[
            pl.BlockSpec((1, tm, D), lambda b, i: (b, i, 0)),
            _const_spec((1, D)),
            _const_spec((rows, D)),
            _const_spec((D, B_QK)),
            _const_spec((A_DIM, 1)),
            _const_spec((A_DIM, 1)),
            pl.BlockSpec((A_DIM // 2, tm), lambda b, i: (0, i)),
            pl.BlockSpec((A_DIM // 2, tm), lambda b, i: (0, i)),
        ],
        out_specs=[
            pl.BlockSpec((1, A_Q, tm), lambda b, i: (b, 0, i)),
            pl.BlockSpec((1, tm, A_KV), lambda b, i: (b, i, 0)),
            pl.BlockSpec((1, va_rows, tm), lambda b, i: (b, 0, i)),
            pl.BlockSpec((1, B_QK, tm), lambda b, i: (b, 0, i)),
            pl.BlockSpec((1, tm, B_QK), lambda b, i: (b, i, 0)),
            pl.BlockSpec((1, vb_rows, tm), lambda b, i: (b, 0, i)),
        ],
        out_shape=[
            jax.ShapeDtypeStruct((B, A_Q, S), BF16),
            jax.ShapeDtypeStruct((B, S, A_KV), BF16),
            jax.ShapeDtypeStruct((B, va_rows, S), BF16),
            jax.ShapeDtypeStruct((B, B_QK, S), BF16),
            jax.ShapeDtypeStruct((B, S, B_QK), BF16),
            jax.ShapeDtypeStruct((B, vb_rows, S), BF16),
        ],
        compiler_params=_params(2),
        name="even_in",
    )(x, g, wt, wkb, gq, gk, cosT, sinT)


def _chain_init(n, dv, tq):
    return tuple((jnp.full((1, tq), NEG_BIG, F32), jnp.zeros((dv + ONES_ROWS, tq), F32))
                 for _ in range(n))


def _chain_out(acc, dv):
    return acc[:dv] / acc[dv:dv + 1]


def _chunk_start(c, tkc):
    return c * tkc if isinstance(c, int) else pl.multiple_of(c * tkc, tkc)


def _key_chunk(S):
    return min(KEY_CHUNK, S // 2)


def _attend(n_chunks, n_chains, dv, tq, first_tile, prep, score, value, scratch, offset=None):
    assert n_chunks % 2 == 0
    max_ref, s_bufs = scratch[0], (scratch[1:1 + n_chains], scratch[1 + n_chains:])

    def produce(ctx, j, nxt, slot):
        s = score(ctx, j, nxt)
        s_bufs[slot][j][...] = s
        top = jnp.max(s, axis=0, keepdims=True)
        r = None if offset is None else offset(ctx, j, nxt)
        return (top, None) if r is None else (top - r, r)

    @pl.when(first_tile)
    def _():
        ctx0 = prep(0, False)
        for j in range(n_chains):
            max_ref[j] = produce(ctx0, j, False, 0)[0]

    maxes = [(max_ref[j], None) for j in range(n_chains)]
    chains = list(_chain_init(n_chains, dv, tq))
    for c in range(n_chunks):
        slot = c % 2
        nxt = c == n_chunks - 1
        ctx = prep(0 if nxt else c + 1, nxt)
        for j in range(n_chains):
            produced = produce(ctx, j, nxt, 1 - slot)
            m, acc = chains[j]
            top, r = maxes[j]
            m_new = jnp.maximum(m, top)
            alpha = jnp.exp2(m - m_new)
            shift = m_new if r is None else m_new + r
            p = jnp.exp2(s_bufs[slot][j][...] - shift).astype(BF16)
            acc = alpha * acc + jnp.dot(value(c, j), p, preferred_element_type=F32)
            chains[j] = (m_new, acc)
            maxes[j] = produced
    for j in range(n_chains):
        max_ref[j] = maxes[j][0]
    return [_chain_out(acc, dv) for _, acc in chains]


def _attend_scratch(n_chains, tkc, tq):
    return ([pltpu.VMEM((n_chains, 1, tq), F32)]
            + [pltpu.VMEM((tkc, tq), F32) for _ in range(2 * n_chains)])


def _next_tile(n_tiles):
    return lambda i: jnp.minimum(i + 1, n_tiles - 1)


def _gqa_kernel(qT_ref, qT_next_ref, k_ref, vT_ref, o_ref, *scratch, tkc):
    S = k_ref.shape[1]
    tq = qT_ref.shape[2]

    def padded_queries(ref):
        pads = []
        for h in range(A_HEADS):
            q = ref[0, h * A_DIM:(h + 1) * A_DIM, :]
            zero = jnp.zeros_like(q)
            pads.append(jnp.concatenate([q, zero] if h < A_GROUP else [zero, q], axis=0))
        return pads

    q_pads = {False: padded_queries(qT_ref), True: padded_queries(qT_next_ref)}

    def prep(c, nxt):
        return k_ref[0, pl.ds(_chunk_start(c, tkc), tkc), :]

    def score(k_c, h, nxt):
        return jnp.dot(k_c, q_pads[nxt][h], preferred_element_type=F32)

    def value(c, h):
        g = h // A_GROUP
        return vT_ref[0, _value_rows(g, A_DIM), pl.ds(_chunk_start(c, tkc), tkc)]

    outs = _attend(S // tkc, A_HEADS, A_DIM, tq, pl.program_id(1) == 0,
                   prep, score, value, scratch)
    o_ref[0] = jnp.concatenate(outs, axis=0).T.astype(o_ref.dtype)


def _gqa(qaT, ka, vaT):
    B, _, S = qaT.shape
    tq = min(Q_TILE, S)
    tkc = _key_chunk(S)
    nxt = _next_tile(S // tq)
    return pl.pallas_call(
        functools.partial(_gqa_kernel, tkc=tkc),
        grid=(B, S // tq),
        in_specs=[
            pl.BlockSpec((1, A_Q, tq), lambda b, i: (b, 0, i)),
            pl.BlockSpec((1, A_Q, tq), lambda b, i: (b, 0, nxt(i))),
            pl.BlockSpec((1, S, A_KV), lambda b, i: (b, 0, 0)),
            pl.BlockSpec((1, A_KV_HEADS * (A_DIM + ONES_ROWS), S), lambda b, i: (b, 0, 0)),
        ],
        out_specs=pl.BlockSpec((1, tq, A_Q), lambda b, i: (b, i, 0)),
        out_shape=jax.ShapeDtypeStruct((B, S, A_Q), BF16),
        scratch_shapes=_attend_scratch(A_HEADS, tkc, tq),
        compiler_params=_params(1, 1),
        name="gqa_attn",
    )(qaT, qaT, ka, vaT)


DIFF_HEADS_PER_STEP = 4


ALIBI_PIECES = 3


def _diff_kernel(slopes_ref, lq1_ref, lk1_ref, lq2_ref, lk2_ref, gsub_ref,
                 qT_ref, qT_next_ref, k_ref, kpos_ref, vT_ref, o_ref, *scratch, tkc, lam_init):
    hp = pl.program_id(1)
    i = pl.program_id(2)
    S = k_ref.shape[1]
    tq = qT_ref.shape[2]
    n_chunks = S // tkc
    n_tiles = S // tq
    pair = 2 * B_DIM
    coefs = [slopes_ref[DIFF_HEADS_PER_STEP * hp + hh] * LOG2E for hh in range(DIFF_HEADS_PER_STEP)]

    def padded_queries(ref):
        pads = []
        for hh in range(DIFF_HEADS_PER_STEP):
            q = ref[0, hh * pair:(hh + 1) * pair, :]
            rows = lax.broadcasted_iota(jnp.int32, q.shape, 0)
            zero = jnp.zeros_like(q)
            pads += [jnp.where(rows < B_DIM, q, zero), jnp.where(rows >= B_DIM, q, zero)]
        return pads

    q_pads = {False: padded_queries(qT_ref), True: padded_queries(qT_next_ref)}
    dmat = (lax.broadcasted_iota(jnp.int32, (tkc, tq), 1)
            - lax.broadcasted_iota(jnp.int32, (tkc, tq), 0)).astype(F32)
    q_local = lax.broadcasted_iota(jnp.int32, (1, tq), 1).astype(F32)
    piece_rows = lax.broadcasted_iota(jnp.int32, (pair, tq), 0) < ALIBI_PIECES

    def chunk_of(c, tile):
        home = (tile * tq) // tkc
        return home, (home if c == 0 else lax.rem(home + c, n_chunks))

    def prep(c, nxt):
        tile = jnp.minimum(i + 1, n_tiles - 1) if nxt else i
        home, chunk = chunk_of(c, tile)
        start = pl.multiple_of(chunk * tkc, tkc)
        q_start = (tile * tq).astype(F32)
        if c == 0:
            dist = jnp.abs(dmat + (q_start - start.astype(F32)))
            return dict(start=start, biases=[dist * -cf for cf in coefs])
        sign = jnp.where(chunk < home, 1.0, -1.0)
        ext = jnp.where(piece_rows, sign, 0.0).astype(BF16)
        q_pos = q_start + q_local
        return dict(start=start, ext=ext, offsets=[(sign * cf) * q_pos for cf in coefs])

    def score(ctx, n, nxt):
        hh = n // 2
        k_c = k_ref[0, pl.ds(ctx["start"], tkc), hh * pair:(hh + 1) * pair]
        q_pad = q_pads[nxt][n]
        if "biases" in ctx:
            return jnp.dot(k_c, q_pad, preferred_element_type=F32) + ctx["biases"][hh]
        keys = jnp.concatenate([k_c, kpos_ref[hh, pl.ds(ctx["start"], tkc), :]], axis=1)
        queries = jnp.concatenate([q_pad, ctx["ext"]], axis=0)
        return jnp.dot(keys, queries, preferred_element_type=F32)

    def offset(ctx, n, nxt):
        return None if "biases" in ctx else ctx["offsets"][n // 2]

    def value(c, n):
        _, chunk = chunk_of(c, i)
        start = pl.multiple_of(chunk * tkc, tkc)
        return vT_ref[0, _value_rows(n // 2, B_VDIM), pl.ds(start, tkc)]

    n_chains = 2 * DIFF_HEADS_PER_STEP
    parts = _attend(n_chunks, n_chains, B_VDIM, tq, i == 0, prep, score, value, scratch, offset)

    lam = (jnp.exp(jnp.sum(lq1_ref[...] * lk1_ref[...], axis=-1, keepdims=True))
           - jnp.exp(jnp.sum(lq2_ref[...] * lk2_ref[...], axis=-1, keepdims=True)) + lam_init)
    outs = []
    for hh in range(DIFF_HEADS_PER_STEP):
        o = parts[2 * hh] - lam * parts[2 * hh + 1]
        outs.append(_rms_cols(o, gsub_ref[...]) * (1.0 - lam_init))
    o_ref[0] = jnp.concatenate(outs, axis=0).T.astype(o_ref.dtype)


def _alibi_key_table(slopes, S):
    a = (slopes * LOG2E)[:, None] * jnp.arange(S, dtype=F32)[None, :]
    pieces, rest = [], a
    for _ in range(ALIBI_PIECES):
        piece = lax.bitcast_convert_type(
            lax.bitcast_convert_type(rest, jnp.uint32) & jnp.uint32(0xFFFF0000), F32)
        pieces.append(piece.astype(BF16))
        rest = rest - piece
    table = jnp.stack(pieces, axis=-1)
    return jnp.pad(table, ((0, 0), (0, 0), (0, 2 * B_DIM - ALIBI_PIECES)))


def _diff(slopes, lq1, lk1, lq2, lk2, gsub, qbT, kb, vbT, lam_init):
    B, _, S = qbT.shape
    tq = min(Q_TILE, S)
    tkc = _key_chunk(S)
    assert tkc % tq == 0
    hs = DIFF_HEADS_PER_STEP
    nxt = _next_tile(S // tq)
    kpos = _alibi_key_table(slopes, S)
    return pl.pallas_call(
        functools.partial(_diff_kernel, tkc=tkc, lam_init=lam_init),
        grid=(B, B_HEADS // hs, S // tq),
        in_specs=[
            pl.BlockSpec(memory_space=pltpu.SMEM),
            _const_spec((1, B_DIM)), _const_spec((1, B_DIM)),
            _const_spec((1, B_DIM)), _const_spec((1, B_DIM)),
            _const_spec((B_VDIM, 1)),
            pl.BlockSpec((1, hs * 2 * B_DIM, tq), lambda b, h, i: (b, h, i)),
            pl.BlockSpec((1, hs * 2 * B_DIM, tq), lambda b, h, i: (b, h, nxt(i))),
            pl.BlockSpec((1, S, hs * 2 * B_DIM), lambda b, h, i: (b, 0, h)),
            pl.BlockSpec((hs, S, 2 * B_DIM), lambda b, h, i: (h, 0, 0)),
            pl.BlockSpec((1, hs * (B_VDIM + ONES_ROWS), S), lambda b, h, i: (b, h, 0)),
        ],
        out_specs=pl.BlockSpec((1, tq, hs * B_VDIM), lambda b, h, i: (b, i, h)),
        out_shape=jax.ShapeDtypeStruct((B, S, B_V), BF16),
        scratch_shapes=_attend_scratch(2 * hs, tkc, tq),
        compiler_params=_params(2, 1),
        name="diff_attn",
    )(slopes, lq1, lk1, lq2, lk2, gsub, qbT, qbT, kb, kpos, vbT)


MLA_HEADS_PER_STEP = 4
MLA_SUBTILES = 2


def _mla_kernel(qT_ref, qT_next_ref, k_ref, vT_ref, o_ref, *scratch, tkc, n_sub):
    S = k_ref.shape[1]
    tqs = qT_ref.shape[2] // n_sub
    nh = MLA_HEADS_PER_STEP
    q_refs = {False: qT_ref, True: qT_next_ref}

    def prep(c, nxt):
        return _chunk_start(c, tkc)

    def score(start, n, nxt):
        j, t = divmod(n, n_sub)
        k_c = k_ref[0, pl.ds(start, tkc), j * C_PAD:(j + 1) * C_PAD]
        q_pad = q_refs[nxt][0, j * C_PAD:(j + 1) * C_PAD, t * tqs:(t + 1) * tqs]
        return jnp.dot(k_c, q_pad, preferred_element_type=F32)

    def value(c, n):
        return vT_ref[0, _value_rows(n // n_sub, C_VDIM), pl.ds(_chunk_start(c, tkc), tkc)]

    outs = _attend(S // tkc, nh * n_sub, C_VDIM, tqs, pl.program_id(2) == 0,
                   prep, score, value, scratch)
    for t in range(n_sub):
        sub = jnp.concatenate([outs[j * n_sub + t] for j in range(nh)], axis=0)
        o_ref[0, t * tqs:(t + 1) * tqs, :] = sub.T.astype(o_ref.dtype)


def _mla(qT, k, vT):
    B, _, S = qT.shape
    n_sub = MLA_SUBTILES if S >= MLA_SUBTILES * Q_TILE else 1
    tq = min(Q_TILE, S) * n_sub
    tkc = _key_chunk(S)
    hp = MLA_HEADS_PER_STEP
    nxt = _next_tile(S // tq)
    return pl.pallas_call(
        functools.partial(_mla_kernel, tkc=tkc, n_sub=n_sub),
        grid=(B, C_HEADS // hp, S // tq),
        in_specs=[
            pl.BlockSpec((1, hp * C_PAD, tq), lambda b, h, i: (b, h, i)),
            pl.BlockSpec((1, hp * C_PAD, tq), lambda b, h, i: (b, h, nxt(i))),
            pl.BlockSpec((1, S, hp * C_PAD), lambda b, h, i: (b, 0, h)),
            pl.BlockSpec((1, hp * (C_VDIM + ONES_ROWS), S), lambda b, h, i: (b, h, 0)),
        ],
        out_specs=pl.BlockSpec((1, tq, hp * C_VDIM), lambda b, h, i: (b, i, h)),
        out_shape=jax.ShapeDtypeStruct((B, S, C_HEADS * C_VDIM), BF16),
        scratch_shapes=_attend_scratch(hp * n_sub, tkc, tq // n_sub),
        compiler_params=_params(2, 1),
        name="mla_attn",
    )(qT, qT, k, vT)


def _odd_in_kernel(x_ref, g_ref, win_ref, gq_ref, gkv_ref, wuqT_ref, wkn_ref, wvT_ref,
                   cosq_ref, sinq_ref, cosk_ref, sink_ref, qT_ref, k_ref, vT_ref):
    hr = C_ROPE // 2
    xn = _rms_rows(x_ref[0], g_ref[...]).astype(BF16)
    a = jnp.dot(xn, win_ref[...], preferred_element_type=F32)
    cqn = _rms_rows(a[:, :C_Q_RANK], gq_ref[...]).astype(BF16)
    ckvn = _rms_rows(a[:, C_Q_RANK:C_Q_RANK + C_KV_RANK], gkv_ref[...]).astype(BF16)

    qT = lax.dot_general(wuqT_ref[...], cqn, _NT, preferred_element_type=F32)
    qT = qT * ((C_NOPE + C_ROPE) ** -0.5 * LOG2E)
    cq, sq = cosq_ref[...], sinq_ref[...]
    for h in range(C_HEADS):
        r = h * C_PAD
        e = qT[r + C_NOPE:r + C_NOPE + hr]
        o = qT[r + C_NOPE + hr:r + C_NOPE + C_ROPE]
        head = jnp.concatenate(
            [qT[r:r + C_NOPE], e * cq - o * sq, e * sq + o * cq, qT[r + C_NOPE + C_ROPE:r + C_PAD]],
            axis=0)
        qT_ref[0, r:r + C_PAD, :] = head.astype(BF16)

    kblk = a[:, C_Q_RANK + C_KV_RANK:]
    t = kblk * cosk_ref[...] + kblk * sink_ref[...]
    lane = lax.broadcasted_iota(jnp.int32, t.shape, 1)
    kr = jnp.where(lane < C_ROPE, t + pltpu.roll(t, C_PAD - C_ROPE, axis=1), 0.0)
    kr = pltpu.roll(kr, C_NOPE, axis=1)

    kn = jnp.dot(ckvn, wkn_ref[...], preferred_element_type=F32)
    for h in range(C_HEADS):
        k_ref[0, :, h * C_PAD:(h + 1) * C_PAD] = (kn[:, h * C_PAD:(h + 1) * C_PAD] + kr).astype(BF16)

    _store_values(vT_ref, lax.dot_general(wvT_ref[...], ckvn, _NT, preferred_element_type=F32),
                  C_HEADS, C_VDIM)


def _odd_in(x, g, win, gq, gkv, wuqT, wkn, wvT, cosq, sinq, cosk, sink):
    B, S, D = x.shape
    tm = min(TOKEN_TILE, S)
    hr = C_ROPE // 2
    return pl.pallas_call(
        _odd_in_kernel,
        grid=(B, S // tm),
        in_specs=[
            pl.BlockSpec((1, tm, D), lambda b, i: (b, i, 0)),
            _const_spec((1, D)),
            _const_spec(win.shape),
            _const_spec((1, C_Q_RANK)),
            _const_spec((1, C_KV_RANK)),
            _const_spec(wuqT.shape),
            _const_spec(wkn.shape),
            _const_spec(wvT.shape),
            pl.BlockSpec((hr, tm), lambda b, i: (0, i)),
            pl.BlockSpec((hr, tm), lambda b, i: (0, i)),
            pl.BlockSpec((tm, C_PAD), lambda b, i: (i, 0)),
            pl.BlockSpec((tm, C_PAD), lambda b, i: (i, 0)),
        ],
        out_specs=[
            pl.BlockSpec((1, C_HEADS * C_PAD, tm), lambda b, i: (b, 0, i)),
            pl.BlockSpec((1, tm, C_HEADS * C_PAD), lambda b, i: (b, i, 0)),
            pl.BlockSpec((1, C_HEADS * (C_VDIM + ONES_ROWS), tm), lambda b, i: (b, 0, i)),
        ],
        out_shape=[
            jax.ShapeDtypeStruct((B, C_HEADS * C_PAD, S), BF16),
            jax.ShapeDtypeStruct((B, S, C_HEADS * C_PAD), BF16),
            jax.ShapeDtypeStruct((B, C_HEADS * (C_VDIM + ONES_ROWS), S), BF16),
        ],
        compiler_params=_params(2),
        name="odd_in",
    )(x, g, win, gq, gkv, wuqT, wkn, wvT, cosq, sinq, cosk, sink)


def _memkv_kernel(mem_ref, g_ref, w_ref, kv_ref):
    mn = _rms_rows(mem_ref[0], g_ref[...]).astype(BF16)
    kv_ref[0] = jnp.dot(mn, w_ref[...], preferred_element_type=F32).astype(BF16)


def _memkv(mem, g, w):
    B, M, D = mem.shape
    N = w.shape[1]
    return pl.pallas_call(
        _memkv_kernel,
        grid=(B,),
        in_specs=[pl.BlockSpec((1, M, D), lambda b: (b, 0, 0)), _const_spec((1, D)), _const_spec((D, N))],
        out_specs=pl.BlockSpec((1, M, N), lambda b: (b, 0, 0)),
        out_shape=jax.ShapeDtypeStruct((B, M, N), BF16),
        compiler_params=_params(1),
        name="mem_kv",
    )(mem, g, w)


def _post_mix_kernel(*refs, n_mix):
    x_ref = refs[0]
    o_refs = refs[1:1 + n_mix]
    w_refs = refs[1 + n_mix:1 + 2 * n_mix]
    gc_ref, wq_ref, kv_ref, wo_ref, out_ref = refs[1 + 2 * n_mix:]
    x = x_ref[0]
    for o_ref, w_ref in zip(o_refs, w_refs):
        x = x + jnp.dot(o_ref[0], w_ref[...], preferred_element_type=F32)

    hc = _rms_rows(x, gc_ref[...]).astype(BF16)
    q = (jnp.dot(hc, wq_ref[...], preferred_element_type=F32) * (X_DIM ** -0.5)).astype(BF16)
    heads = []
    for h in range(X_HEADS):
        k_h = kv_ref[0, :, h * X_DIM:(h + 1) * X_DIM]
        v_h = kv_ref[0, :, D_MODEL + h * X_DIM:D_MODEL + (h + 1) * X_DIM]
        s = lax.dot_general(q[:, h * X_DIM:(h + 1) * X_DIM], k_h, _NT, preferred_element_type=F32)
        p = jnp.exp(s - jnp.max(s, axis=-1, keepdims=True))
        l = jnp.sum(p, axis=-1, keepdims=True)
        heads.append((jnp.dot(p.astype(BF16), v_h, preferred_element_type=F32) / l).astype(BF16))
    o = jnp.concatenate(heads, axis=-1)
    out_ref[0] = x + jnp.dot(o, wo_ref[...], preferred_element_type=F32)


def _post_mix(x, mixes, weights, gc, wq, kv, wo):
    B, S, D = x.shape
    tm = min(WIDE_TOKEN_TILE, S)
    n = len(mixes)
    M = kv.shape[1]
    tok = lambda b, i: (b, i, 0)
    return pl.pallas_call(
        functools.partial(_post_mix_kernel, n_mix=n),
        grid=(B, S // tm),
        in_specs=([pl.BlockSpec((1, tm, D), tok)]
                  + [pl.BlockSpec((1, tm, m.shape[2]), tok) for m in mixes]
                  + [_const_spec(w.shape) for w in weights]
                  + [_const_spec((1, D)), _const_spec((D, D)),
                     pl.BlockSpec((1, M, 2 * D), lambda b, i: (b, 0, 0)),
                     _const_spec((D, D))]),
        out_specs=pl.BlockSpec((1, tm, D), tok),
        out_shape=jax.ShapeDtypeStruct((B, S, D), F32),
        compiler_params=_params(2),
        name="post_mix",
    )(x, *mixes, *weights, gc, wq, kv, wo)


def _ffn_kernel(x_ref, g_ref, wgu_ref, wd_ref, gf_ref, out_ref, *, final_norm):
    x = x_ref[0]
    xn = _rms_rows(x, g_ref[...]).astype(BF16)
    acc = x
    for c in range(D_FF // FF_CHUNK):
        lo = c * FF_CHUNK
        gate = jnp.dot(xn, wgu_ref[:, lo:lo + FF_CHUNK], preferred_element_type=F32)
        up = jnp.dot(xn, wgu_ref[:, D_FF + lo:D_FF + lo + FF_CHUNK], preferred_element_type=F32)
        hidden = (jax.nn.silu(gate) * up).astype(BF16)
        acc = acc + jnp.dot(hidden, wd_ref[lo:lo + FF_CHUNK, :], preferred_element_type=F32)
    if final_norm:
        acc = _rms_rows(acc, gf_ref[...])
    out_ref[0] = acc


def _ffn(x, g, wgu, wd, gf, final_norm):
    B, S, D = x.shape
    tm = min(WIDE_TOKEN_TILE, S)
    tok = lambda b, i: (b, i, 0)
    return pl.pallas_call(
        functools.partial(_ffn_kernel, final_norm=final_norm),
        grid=(B, S // tm),
        in_specs=[pl.BlockSpec((1, tm, D), tok), _const_spec((1, D)),
                  _const_spec(wgu.shape), _const_spec(wd.shape), _const_spec((1, D))],
        out_specs=pl.BlockSpec((1, tm, D), tok),
        out_shape=jax.ShapeDtypeStruct((B, S, D), F32),
        compiler_params=_params(2),
        name="ffn",
    )(x, g, wgu, wd, gf)


def _rope_freqs(n_pairs):
    return ROPE_THETA ** (-jnp.arange(n_pairs, dtype=F32) / n_pairs)


def _axial_angles(S):
    rows = S // GRID_W
    r = jnp.repeat(jnp.arange(rows, dtype=F32), GRID_W)
    c = jnp.tile(jnp.arange(GRID_W, dtype=F32), rows)
    f = _rope_freqs(A_DIM // 4)
    return jnp.concatenate([r[:, None] * f, c[:, None] * f], axis=-1)


def _linear_angles(S, dim):
    t = jnp.arange(S, dtype=F32)
    return t[:, None] * _rope_freqs(dim // 2)


def _deinterleave(n):
    return np.concatenate([np.arange(0, n, 2), np.arange(1, n, 2)])


def _prep_even(w_in, gq, gk):
    perm = _deinterleave(A_DIM)
    o = 0
    w_qa = w_in[:, o:o + A_Q].reshape(D_MODEL, A_HEADS, A_DIM)[:, :, perm].reshape(D_MODEL, A_Q)
    o += A_Q
    w_ka = w_in[:, o:o + A_KV].reshape(D_MODEL, A_KV_HEADS, A_DIM)[:, :, perm].reshape(D_MODEL, A_KV)
    o += A_KV
    w_va = w_in[:, o:o + A_KV]
    o += A_KV
    w_qb = w_in[:, o:o + B_QK]
    o += B_QK
    w_kb = w_in[:, o:o + B_QK]
    o += B_QK
    w_vb = w_in[:, o:o + B_V]
    wt = jnp.concatenate([w_qa, w_ka, w_va, w_qb, w_vb], axis=1).T.astype(BF16)
    return wt, w_kb.astype(BF16), gq[perm].reshape(A_DIM, 1), gk[perm].reshape(A_DIM, 1)


def _prep_odd(w_in, w_uq, w_ukv):
    hr = C_ROPE // 2
    perm = _deinterleave(C_ROPE)
    w_kr = w_in[:, C_Q_RANK + C_KV_RANK:][:, perm]
    w_kr_rot = jnp.concatenate([-w_kr[:, hr:], w_kr[:, :hr]], axis=1)
    win = jnp.concatenate(
        [w_in[:, :C_Q_RANK + C_KV_RANK], w_kr, w_kr_rot,
         jnp.zeros((D_MODEL, C_PAD - 2 * C_ROPE), F32)], axis=1).astype(BF16)
    wq = w_uq.reshape(C_Q_RANK, C_HEADS, C_NOPE + C_ROPE)
    wq = jnp.concatenate(
        [wq[:, :, :C_NOPE], wq[:, :, C_NOPE:][:, :, perm],
         jnp.zeros((C_Q_RANK, C_HEADS, C_PAD - C_NOPE - C_ROPE), F32)], axis=2)
    wuqT = wq.reshape(C_Q_RANK, C_HEADS * C_PAD).T.astype(BF16)
    wkv = w_ukv.reshape(C_KV_RANK, C_HEADS, C_NOPE + C_VDIM)
    wkn = jnp.concatenate(
        [wkv[:, :, :C_NOPE], jnp.zeros((C_KV_RANK, C_HEADS, C_PAD - C_NOPE), F32)], axis=2)
    wkn = wkn.reshape(C_KV_RANK, C_HEADS * C_PAD).astype(BF16)
    wvT = wkv[:, :, C_NOPE:].reshape(C_KV_RANK, C_HEADS * C_VDIM).T.astype(BF16)
    return win, wuqT, wkn, wvT


def _trunk(x, mem, p):
    B, S, D = x.shape
    depth = p['norm_mix'].shape[0]
    row = lambda v: v.reshape(1, -1)

    ang_a = _axial_angles(S)
    cos_a, sin_a = jnp.cos(ang_a).T, jnp.sin(ang_a).T
    ang_l = _linear_angles(S, C_ROPE)
    cos_l, sin_l = jnp.cos(ang_l), jnp.sin(ang_l)
    zpad = jnp.zeros((S, C_PAD - 2 * C_ROPE), F32)
    zrope = jnp.zeros((S, C_ROPE), F32)
    cos_k = jnp.concatenate([cos_l, cos_l, zrope, zpad], axis=1)
    sin_k = jnp.concatenate([zrope, sin_l, sin_l, zpad], axis=1)
    slopes = jnp.asarray(2.0 ** (-8.0 * np.arange(1, B_HEADS + 1) / B_HEADS), dtype=F32)

    for layer in range(depth):
        if layer % 2 == 0:
            e = layer // 2
            wt, wkb, gq, gk = _prep_even(p['e_w_in'][e], p['e_q_norm'][e], p['e_k_norm'][e])
            qaT, ka, vaT, qbT, kb, vbT = _even_in(
                x, row(p['norm_mix'][layer]), wt, wkb, gq, gk, cos_a, sin_a)
            oa = _gqa(qaT, ka, vaT)
            lam_init = 0.8 - 0.6 * math.exp(-0.3 * layer)
            ob = _diff(slopes, row(p['e_lam_q1'][e]), row(p['e_lam_k1'][e]),
                       row(p['e_lam_q2'][e]), row(p['e_lam_k2'][e]),
                       p['e_subln'][e].reshape(B_VDIM, 1), qbT, kb, vbT, lam_init)
            w_out = p['e_w_out'][e].astype(BF16)
            mixes, weights = [oa, ob], [w_out[:A_Q], w_out[A_Q:]]
        else:
            o = layer // 2
            win, wuqT, wkn, wvT = _prep_odd(p['o_w_in'][o], p['o_w_uq'][o], p['o_w_ukv'][o])
            qT, k, vT = _odd_in(x, row(p['norm_mix'][layer]), win, row(p['o_q_norm'][o]),
                                row(p['o_kv_norm'][o]), wuqT, wkn, wvT,
                                cos_l.T, sin_l.T, cos_k, sin_k)
            mixes, weights = [_mla(qT, k, vT)], [p['o_w_out'][o].astype(BF16)]
        kv = _memkv(mem, row(p['norm_mem'][layer]), p['w_ckv'][layer].astype(BF16))
        x = _post_mix(x, mixes, weights, row(p['norm_cross'][layer]),
                      p['w_cq'][layer].astype(BF16), kv, p['w_co'][layer].astype(BF16))
        x = _ffn(x, row(p['norm_ffn'][layer]), p['w_gu'][layer].astype(BF16),
                 p['w_down'][layer].astype(BF16), row(p['final_norm']),
                 final_norm=(layer == depth - 1))
    return x


def kernel(x_prompt, x_sample, mem_prompt, mem_sample, norm_mix, e_w_in, e_q_norm, e_k_norm, e_lam_q1, e_lam_k1, e_lam_q2, e_lam_k2, e_subln, e_w_out, o_w_in, o_q_norm, o_kv_norm, o_w_uq, o_w_ukv, o_w_out, norm_cross, norm_mem, w_cq, w_ckv, w_co, norm_ffn, w_gu, w_down, final_norm):
    p = dict(norm_mix=norm_mix, e_w_in=e_w_in, e_q_norm=e_q_norm, e_k_norm=e_k_norm,
             e_lam_q1=e_lam_q1, e_lam_k1=e_lam_k1, e_lam_q2=e_lam_q2, e_lam_k2=e_lam_k2,
             e_subln=e_subln, e_w_out=e_w_out, o_w_in=o_w_in, o_q_norm=o_q_norm,
             o_kv_norm=o_kv_norm, o_w_uq=o_w_uq, o_w_ukv=o_w_ukv, o_w_out=o_w_out,
             norm_cross=norm_cross, norm_mem=norm_mem, w_cq=w_cq, w_ckv=w_ckv, w_co=w_co,
             norm_ffn=norm_ffn, w_gu=w_gu, w_down=w_down, final_norm=final_norm)
    return (_trunk(x_prompt, mem_prompt, p), _trunk(x_sample, mem_sample, p))
```

```python
import functools
import math

import jax
import jax.numpy as jnp
import numpy as np
from jax import lax
from jax.experimental import pallas as pl
from jax.experimental.pallas import tpu as pltpu

F32 = jnp.float32
BF16 = jnp.bfloat16

D_MODEL = 1024
GRID_W = 64
EPS = 1e-6
ROPE_THETA = 10000.0
A_HEADS, A_KV_HEADS, A_DIM = 8, 2, 64
A_GROUP = A_HEADS // A_KV_HEADS
B_HEADS, B_DIM = 4, 64
B_VDIM = 2 * B_DIM
A_Q = A_HEADS * A_DIM
A_KV = A_KV_HEADS * A_DIM
B_QK = B_HEADS * 2 * B_DIM
B_V = B_HEADS * B_VDIM
C_HEADS, C_Q_RANK, C_KV_RANK, C_NOPE, C_ROPE, C_VDIM = 16, 384, 256, 64, 32, 64
C_PAD = 128
X_HEADS = 4
X_DIM = D_MODEL // X_HEADS
D_FF = ((-(-8 * D_MODEL // 3) + 255) // 256) * 256
FF_CHUNK = 256
NEG_BIG = float(np.finfo(np.float32).min)
LOG2E = math.log2(math.e)

GROUPED_TOKEN_TILE = 1024
WIDE_TOKEN_TILE = 512
Q_TILE = 256
KEY_CHUNK = 256
VMEM_LIMIT = 48 * 1024 * 1024

_NT = (((1,), (1,)), ((), ()))


def _params(n_parallel, n_arbitrary=0):
    return pltpu.CompilerParams(
        dimension_semantics=("parallel",) * n_parallel + ("arbitrary",) * n_arbitrary,
        vmem_limit_bytes=VMEM_LIMIT)


def _rms_rows(x, g):
    ms = jnp.mean(x * x, axis=-1, keepdims=True)
    return (x * lax.rsqrt(ms + EPS)) * g


def _rms_cols(x, g):
    ms = jnp.mean(x * x, axis=0, keepdims=True)
    return (x * lax.rsqrt(ms + EPS)) * g


def _const_spec(shape):
    nd = len(shape)
    return pl.BlockSpec(shape, lambda *_: (0,) * nd)


ONES_ROWS = 16


def _store_values(ref, vT, n_heads, dv, cols):
    ext = dv + ONES_ROWS
    ones = jnp.ones((ONES_ROWS, vT.shape[1]), BF16)
    for h in range(n_heads):
        ref[0, h * ext:h * ext + dv, cols] = vT[h * dv:(h + 1) * dv].astype(BF16)
        ref[0, h * ext + dv:(h + 1) * ext, cols] = ones


ROW_GROUPS = 2


def _row_groups(tm):
    n = ROW_GROUPS if tm % (ROW_GROUPS * 128) == 0 else 1
    return [slice(g * tm // n, (g + 1) * tm // n) for g in range(n)]


def _value_rows(h, dv):
    ext = dv + ONES_ROWS
    return slice(h * ext, (h + 1) * ext)


def _even_in_kernel(x_ref, g_ref, wt_ref, wkb_ref, gq_ref, gk_ref, cos_ref, sin_ref,
                    qaT_ref, ka_ref, vaT_ref, qbT_ref, kb_ref, vbT_ref):
    half = A_DIM // 2
    groups = _row_groups(x_ref.shape[1])
    xns = [_rms_rows(x_ref[0, r, :], g_ref[...]).astype(BF16) for r in groups]
    yTs = [lax.dot_general(wt_ref[...], xn, _NT, preferred_element_type=F32) for xn in xns]
    for r, xn in zip(groups, xns):
        kb_ref[0, r, :] = jnp.dot(xn, wkb_ref[...], preferred_element_type=F32).astype(BF16)

    scale_a = A_DIM ** -0.5 * LOG2E
    for r, yT in zip(groups, yTs):
        cos = cos_ref[:, r]
        sin = sin_ref[:, r]

        def norm_rope(xh, g, scale):
            y = _rms_cols(xh, g)
            e, o = y[:half], y[half:]
            return jnp.concatenate([e * cos - o * sin, e * sin + o * cos], axis=0) * scale

        for h in range(A_HEADS):
            qh = norm_rope(yT[h * A_DIM:(h + 1) * A_DIM], gq_ref[...], scale_a)
            qaT_ref[0, h * A_DIM:(h + 1) * A_DIM, r] = qh.astype(BF16)
        kT = jnp.concatenate(
            [norm_rope(yT[A_Q + h * A_DIM:A_Q + (h + 1) * A_DIM], gk_ref[...], 1.0)
             for h in range(A_KV_HEADS)], axis=0)
        ka_ref[0, r, :] = kT.T.astype(BF16)
        r0 = A_Q + A_KV
        _store_values(vaT_ref, yT[r0:r0 + A_KV], A_KV_HEADS, A_DIM, r)
        r0 += A_KV
        qbT_ref[0, :, r] = (yT[r0:r0 + B_QK] * (B_DIM ** -0.5 * LOG2E)).astype(BF16)
        r0 += B_QK
        _store_values(vbT_ref, yT[r0:r0 + B_V], B_HEADS, B_VDIM, r)


def _even_in(x, g, wt, wkb, gq, gk, cosT, sinT):
    B, S, D = x.shape
    tm = min(GROUPED_TOKEN_TILE, S)
    rows = wt.shape[0]
    va_rows = A_KV_HEADS * (A_DIM + ONES_ROWS)
    vb_rows = B_HEADS * (B_VDIM + ONES_ROWS)
    return pl.pallas_call(
        _even_in_kernel,
        grid=(B, S // tm),
        in_specs=[
            pl.BlockSpec((1, tm, D), lambda b, i: (b, i, 0)),
            _const_spec((1, D)),
            _const_spec((rows, D)),
            _const_spec((D, B_QK)),
            _const_spec((A_DIM, 1)),
            _const_spec((A_DIM, 1)),
            pl.BlockSpec((A_DIM // 2, tm), lambda b, i: (0, i)),
            pl.BlockSpec((A_DIM // 2, tm), lambda b, i: (0, i)),
        ],
        out_specs=[
            pl.BlockSpec((1, A_Q, tm), lambda b, i: (b, 0, i)),
            pl.BlockSpec((1, tm, A_KV), lambda b, i: (b, i, 0)),
            pl.BlockSpec((1, va_rows, tm), lambda b, i: (b, 0, i)),
            pl.BlockSpec((1, B_QK, tm), lambda b, i: (b, 0, i)),
            pl.BlockSpec((1, tm, B_QK), lambda b, i: (b, i, 0)),
            pl.BlockSpec((1, vb_rows, tm), lambda b, i: (b, 0, i)),
        ],
        out_shape=[
            jax.ShapeDtypeStruct((B, A_Q, S), BF16),
            jax.ShapeDtypeStruct((B, S, A_KV), BF16),
            jax.ShapeDtypeStruct((B, va_rows, S), BF16),
            jax.ShapeDtypeStruct((B, B_QK, S), BF16),
            jax.ShapeDtypeStruct((B, S, B_QK), BF16),
            jax.ShapeDtypeStruct((B, vb_rows, S), BF16),
        ],
        compiler_params=_params(2),
        name="even_in",
    )(x, g, wt, wkb, gq, gk, cosT, sinT)


def _chain_init(n, dv, tq):
    return tuple((jnp.full((1, tq), NEG_BIG, F32), jnp.zeros((dv + ONES_ROWS, tq), F32))
                 for _ in range(n))


def _chain_out(acc, dv):
    return acc[:dv] / acc[dv:dv + 1]


def _chunk_start(c, tkc):
    return c * tkc if isinstance(c, int) else pl.multiple_of(c * tkc, tkc)


def _key_chunk(S):
    return min(KEY_CHUNK, S // 2)


def _attend(n_chunks, n_chains, dv, tq, first_tile, prep, score, value, scratch, offset=None):
    assert n_chunks % 2 == 0
    max_ref, s_bufs = scratch[0], (scratch[1:1 + n_chains], scratch[1 + n_chains:])

    def produce(ctx, j, nxt, slot):
        s = score(ctx, j, nxt)
        s_bufs[slot][j][...] = s
        top = jnp.max(s, axis=0, keepdims=True)
        r = None if offset is None else offset(ctx, j, nxt)
        return (top, None) if r is None else (top - r, r)

    @pl.when(first_tile)
    def _():
        ctx0 = prep(0, False)
        for j in range(n_chains):
            max_ref[j] = produce(ctx0, j, False, 0)[0]

    maxes = [(max_ref[j], None) for j in range(n_chains)]
    chains = list(_chain_init(n_chains, dv, tq))
    for c in range(n_chunks):
        slot = c % 2
        nxt = c == n_chunks - 1
        ctx = prep(0 if nxt else c + 1, nxt)
        for j in range(n_chains):
            produced = produce(ctx, j, nxt, 1 - slot)
            m, acc = chains[j]
            top, r = maxes[j]
            m_new = jnp.maximum(m, top)
            alpha = jnp.exp2(m - m_new)
            shift = m_new if r is None else m_new + r
            p = jnp.exp2(s_bufs[slot][j][...] - shift).astype(BF16)
            acc = alpha * acc + jnp.dot(value(c, j), p, preferred_element_type=F32)
            chains[j] = (m_new, acc)
            maxes[j] = produced
    for j in range(n_chains):
        max_ref[j] = maxes[j][0]
    return [_chain_out(acc, dv) for _, acc in chains]


def _attend_scratch(n_chains, tkc, tq):
    return ([pltpu.VMEM((n_chains, 1, tq), F32)]
            + [pltpu.VMEM((tkc, tq), F32) for _ in range(2 * n_chains)])


def _next_tile(n_tiles):
    return lambda i: jnp.minimum(i + 1, n_tiles - 1)


def _gqa_kernel(qT_ref, qT_next_ref, k_ref, vT_ref, o_ref, *scratch, tkc):
    S = k_ref.shape[1]
    tq = qT_ref.shape[2]

    def padded_queries(ref):
        pads = []
        for h in range(A_HEADS):
            q = ref[0, h * A_DIM:(h + 1) * A_DIM, :]
            zero = jnp.zeros_like(q)
            pads.append(jnp.concatenate([q, zero] if h < A_GROUP else [zero, q], axis=0))
        return pads

    q_pads = {False: padded_queries(qT_ref), True: padded_queries(qT_next_ref)}

    def prep(c, nxt):
        return k_ref[0, pl.ds(_chunk_start(c, tkc), tkc), :]

    def score(k_c, h, nxt):
        return jnp.dot(k_c, q_pads[nxt][h], preferred_element_type=F32)

    def value(c, h):
        g = h // A_GROUP
        return vT_ref[0, _value_rows(g, A_DIM), pl.ds(_chunk_start(c, tkc), tkc)]

    outs = _attend(S // tkc, A_HEADS, A_DIM, tq, pl.program_id(1) == 0,
                   prep, score, value, scratch)
    o_ref[0] = jnp.concatenate(outs, axis=0).T.astype(o_ref.dtype)


def _gqa(qaT, ka, vaT):
    B, _, S = qaT.shape
    tq = min(Q_TILE, S)
    tkc = _key_chunk(S)
    nxt = _next_tile(S // tq)
    return pl.pallas_call(
        functools.partial(_gqa_kernel, tkc=tkc),
        grid=(B, S // tq),
        in_specs=[
            pl.BlockSpec((1, A_Q, tq), lambda b, i: (b, 0, i)),
            pl.BlockSpec((1, A_Q, tq), lambda b, i: (b, 0, nxt(i))),
            pl.BlockSpec((1, S, A_KV), lambda b, i: (b, 0, 0)),
            pl.BlockSpec((1, A_KV_HEADS * (A_DIM + ONES_ROWS), S), lambda b, i: (b, 0, 0)),
        ],
        out_specs=pl.BlockSpec((1, tq, A_Q), lambda b, i: (b, i, 0)),
        out_shape=jax.ShapeDtypeStruct((B, S, A_Q), BF16),
        scratch_shapes=_attend_scratch(A_HEADS, tkc, tq),
        compiler_params=_params(1, 1),
        name="gqa_attn",
    )(qaT, qaT, ka, vaT)


DIFF_HEADS_PER_STEP = 4


ALIBI_PIECES = 3


def _diff_kernel(slopes_ref, lq1_ref, lk1_ref, lq2_ref, lk2_ref, gsub_ref,
                 qT_ref, qT_next_ref, k_ref, kpos_ref, vT_ref, o_ref, *scratch, tkc, lam_init):
    hp = pl.program_id(1)
    i = pl.program_id(2)
    S = k_ref.shape[1]
    tq = qT_ref.shape[2]
    n_chunks = S // tkc
    n_tiles = S // tq
    pair = 2 * B_DIM
    coefs = [slopes_ref[DIFF_HEADS_PER_STEP * hp + hh] * LOG2E for hh in range(DIFF_HEADS_PER_STEP)]

    def padded_queries(ref):
        pads = []
        for hh in range(DIFF_HEADS_PER_STEP):
            q = ref[0, hh * pair:(hh + 1) * pair, :]
            rows = lax.broadcasted_iota(jnp.int32, q.shape, 0)
            zero = jnp.zeros_like(q)
            pads += [jnp.where(rows < B_DIM, q, zero), jnp.where(rows >= B_DIM, q, zero)]
        return pads

    q_pads = {False: padded_queries(qT_ref), True: padded_queries(qT_next_ref)}
    dmat = (lax.broadcasted_iota(jnp.int32, (tkc, tq), 1)
            - lax.broadcasted_iota(jnp.int32, (tkc, tq), 0)).astype(F32)
    q_local = lax.broadcasted_iota(jnp.int32, (1, tq), 1).astype(F32)
    piece_rows = lax.broadcasted_iota(jnp.int32, (pair, tq), 0) < ALIBI_PIECES

    def chunk_of(c, tile):
        home = (tile * tq) // tkc
        return home, (home if c == 0 else lax.rem(home + c, n_chunks))

    def prep(c, nxt):
        tile = jnp.minimum(i + 1, n_tiles - 1) if nxt else i
        home, chunk = chunk_of(c, tile)
        start = pl.multiple_of(chunk * tkc, tkc)
        q_start = (tile * tq).astype(F32)
        if c == 0:
            dist = jnp.abs(dmat + (q_start - start.astype(F32)))
            return dict(start=start, biases=[dist * -cf for cf in coefs])
        sign = jnp.where(chunk < home, 1.0, -1.0)
        ext = jnp.where(piece_rows, sign, 0.0).astype(BF16)
        q_pos = q_start + q_local
        return dict(start=start, ext=ext, offsets=[(sign * cf) * q_pos for cf in coefs])

    def score(ctx, n, nxt):
        hh = n // 2
        k_c = k_ref[0, pl.ds(ctx["start"], tkc), hh * pair:(hh + 1) * pair]
        q_pad = q_pads[nxt][n]
        if "biases" in ctx:
            return jnp.dot(k_c, q_pad, preferred_element_type=F32) + ctx["biases"][hh]
        keys = jnp.concatenate([k_c, kpos_ref[hh, pl.ds(ctx["start"], tkc), :]], axis=1)
        queries = jnp.concatenate([q_pad, ctx["ext"]], axis=0)
        return jnp.dot(keys, queries, preferred_element_type=F32)

    def offset(ctx, n, nxt):
        return None if "biases" in ctx else ctx["offsets"][n // 2]

    def value(c, n):
        _, chunk = chunk_of(c, i)
        start = pl.multiple_of(chunk * tkc, tkc)
        return vT_ref[0, _value_rows(n // 2, B_VDIM), pl.ds(start, tkc)]

    n_chains = 2 * DIFF_HEADS_PER_STEP
    parts = _attend(n_chunks, n_chains, B_VDIM, tq, i == 0, prep, score, value, scratch, offset)

    lam = (jnp.exp(jnp.sum(lq1_ref[...] * lk1_ref[...], axis=-1, keepdims=True))
           - jnp.exp(jnp.sum(lq2_ref[...] * lk2_ref[...], axis=-1, keepdims=True)) + lam_init)
    outs = []
    for hh in range(DIFF_HEADS_PER_STEP):
        o = parts[2 * hh] - lam * parts[2 * hh + 1]
        outs.append(_rms_cols(o, gsub_ref[...]) * (1.0 - lam_init))
    o_ref[0] = jnp.concatenate(outs, axis=0).T.astype(o_ref.dtype)


def _alibi_key_table(slopes, S):
    a = (slopes * LOG2E)[:, None] * jnp.arange(S, dtype=F32)[None, :]
    pieces, rest = [], a
    for _ in range(ALIBI_PIECES):
        piece = lax.bitcast_convert_type(
            lax.bitcast_convert_type(rest, jnp.uint32) & jnp.uint32(0xFFFF0000), F32)
        pieces.append(piece.astype(BF16))
        rest = rest - piece
    table = jnp.stack(pieces, axis=-1)
    return jnp.pad(table, ((0, 0), (0, 0), (0, 2 * B_DIM - ALIBI_PIECES)))


def _diff(slopes, lq1, lk1, lq2, lk2, gsub, qbT, kb, vbT, lam_init):
    B, _, S = qbT.shape
    tq = min(Q_TILE, S)
    tkc = _key_chunk(S)
    assert tkc % tq == 0
    hs = DIFF_HEADS_PER_STEP
    nxt = _next_tile(S // tq)
    kpos = _alibi_key_table(slopes, S)
    return pl.pallas_call(
        functools.partial(_diff_kernel, tkc=tkc, lam_init=lam_init),
        grid=(B, B_HEADS // hs, S // tq),
        in_specs=[
            pl.BlockSpec(memory_space=pltpu.SMEM),
            _const_spec((1, B_DIM)), _const_spec((1, B_DIM)),
            _const_spec((1, B_DIM)), _const_spec((1, B_DIM)),
            _const_spec((B_VDIM, 1)),
            pl.BlockSpec((1, hs * 2 * B_DIM, tq), lambda b, h, i: (b, h, i)),
            pl.BlockSpec((1, hs * 2 * B_DIM, tq), lambda b, h, i: (b, h, nxt(i))),
            pl.BlockSpec((1, S, hs * 2 * B_DIM), lambda b, h, i: (b, 0, h)),
            pl.BlockSpec((hs, S, 2 * B_DIM), lambda b, h, i: (h, 0, 0)),
            pl.BlockSpec((1, hs * (B_VDIM + ONES_ROWS), S), lambda b, h, i: (b, h, 0)),
        ],
        out_specs=pl.BlockSpec((1, tq, hs * B_VDIM), lambda b, h, i: (b, i, h)),
        out_shape=jax.ShapeDtypeStruct((B, S, B_V), BF16),
        scratch_shapes=_attend_scratch(2 * hs, tkc, tq),
        compiler_params=_params(2, 1),
        name="diff_attn",
    )(slopes, lq1, lk1, lq2, lk2, gsub, qbT, qbT, kb, kpos, vbT)


MLA_HEADS_PER_STEP = 4
MLA_SUBTILES = 2


def _mla_kernel(qT_ref, qT_next_ref, k_ref, vT_ref, o_ref, *scratch, tkc, n_sub):
    S = k_ref.shape[1]
    tqs = qT_ref.shape[2] // n_sub
    nh = MLA_HEADS_PER_STEP
    q_refs = {False: qT_ref, True: qT_next_ref}

    def prep(c, nxt):
        return _chunk_start(c, tkc)

    def score(start, n, nxt):
        j, t = divmod(n, n_sub)
        k_c = k_ref[0, pl.ds(start, tkc), j * C_PAD:(j + 1) * C_PAD]
        q_pad = q_refs[nxt][0, j * C_PAD:(j + 1) * C_PAD, t * tqs:(t + 1) * tqs]
        return jnp.dot(k_c, q_pad, preferred_element_type=F32)

    def value(c, n):
        return vT_ref[0, _value_rows(n // n_sub, C_VDIM), pl.ds(_chunk_start(c, tkc), tkc)]

    outs = _attend(S // tkc, nh * n_sub, C_VDIM, tqs, pl.program_id(2) == 0,
                   prep, score, value, scratch)
    for t in range(n_sub):
        sub = jnp.concatenate([outs[j * n_sub + t] for j in range(nh)], axis=0)
        o_ref[0, t * tqs:(t + 1) * tqs, :] = sub.T.astype(o_ref.dtype)


def _mla(qT, k, vT):
    B, _, S = qT.shape
    n_sub = MLA_SUBTILES if S >= MLA_SUBTILES * Q_TILE else 1
    tq = min(Q_TILE, S) * n_sub
    tkc = _key_chunk(S)
    hp = MLA_HEADS_PER_STEP
    nxt = _next_tile(S // tq)
    return pl.pallas_call(
        functools.partial(_mla_kernel, tkc=tkc, n_sub=n_sub),
        grid=(B, C_HEADS // hp, S // tq),
        in_specs=[
            pl.BlockSpec((1, hp * C_PAD, tq), lambda b, h, i: (b, h, i)),
            pl.BlockSpec((1, hp * C_PAD, tq), lambda b, h, i: (b, h, nxt(i))),
            pl.BlockSpec((1, S, hp * C_PAD), lambda b, h, i: (b, 0, h)),
            pl.BlockSpec((1, hp * (C_VDIM + ONES_ROWS), S), lambda b, h, i: (b, h, 0)),
        ],
        out_specs=pl.BlockSpec((1, tq, hp * C_VDIM), lambda b, h, i: (b, i, h)),
        out_shape=jax.ShapeDtypeStruct((B, S, C_HEADS * C_VDIM), BF16),
        scratch_shapes=_attend_scratch(hp * n_sub, tkc, tq // n_sub),
        compiler_params=_params(2, 1),
        name="mla_attn",
    )(qT, qT, k, vT)


def _odd_in_kernel(x_ref, g_ref, win_ref, gq_ref, gkv_ref, wuqT_ref, wkn_ref, wvT_ref,
                   cosq_ref, sinq_ref, cosk_ref, sink_ref, qT_ref, k_ref, vT_ref):
    hr = C_ROPE // 2
    groups = _row_groups(x_ref.shape[1])
    xns = [_rms_rows(x_ref[0, r, :], g_ref[...]).astype(BF16) for r in groups]
    lat = [jnp.dot(xn, win_ref[...], preferred_element_type=F32) for xn in xns]
    cqns = [_rms_rows(a[:, :C_Q_RANK], gq_ref[...]).astype(BF16) for a in lat]
    ckvns = [_rms_rows(a[:, C_Q_RANK:C_Q_RANK + C_KV_RANK], gkv_ref[...]).astype(BF16) for a in lat]

    qTs = [lax.dot_general(wuqT_ref[...], cqn, _NT, preferred_element_type=F32) for cqn in cqns]
    for rows, qT in zip(groups, qTs):
        qT = qT * ((C_NOPE + C_ROPE) ** -0.5 * LOG2E)
        cq, sq = cosq_ref[:, rows], sinq_ref[:, rows]
        for h in range(C_HEADS):
            r = h * C_PAD
            e = qT[r + C_NOPE:r + C_NOPE + hr]
            o = qT[r + C_NOPE + hr:r + C_NOPE + C_ROPE]
            head = jnp.concatenate(
                [qT[r:r + C_NOPE], e * cq - o * sq, e * sq + o * cq, qT[r + C_NOPE + C_ROPE:r + C_PAD]],
                axis=0)
            qT_ref[0, r:r + C_PAD, rows] = head.astype(BF16)

    kns = [jnp.dot(ckvn, wkn_ref[...], preferred_element_type=F32) for ckvn in ckvns]
    for rows, a, kn in zip(groups, lat, kns):
        kblk = a[:, C_Q_RANK + C_KV_RANK:]
        t = kblk * cosk_ref[rows, :] + kblk * sink_ref[rows, :]
        lane = lax.broadcasted_iota(jnp.int32, t.shape, 1)
        kr = jnp.where(lane < C_ROPE, t + pltpu.roll(t, C_PAD - C_ROPE, axis=1), 0.0)
        kr = pltpu.roll(kr, C_NOPE, axis=1)
        for h in range(C_HEADS):
            k_ref[0, rows, h * C_PAD:(h + 1) * C_PAD] = (
                kn[:, h * C_PAD:(h + 1) * C_PAD] + kr).astype(BF16)

    for rows, ckvn in zip(groups, ckvns):
        _store_values(vT_ref, lax.dot_general(wvT_ref[...], ckvn, _NT, preferred_element_type=F32),
                      C_HEADS, C_VDIM, rows)


def _odd_in(x, g, win, gq, gkv, wuqT, wkn, wvT, cosq, sinq, cosk, sink):
    B, S, D = x.shape
    tm = min(GROUPED_TOKEN_TILE, S)
    hr = C_ROPE // 2
    return pl.pallas_call(
        _odd_in_kernel,
        grid=(B, S // tm),
        in_specs=[
            pl.BlockSpec((1, tm, D), lambda b, i: (b, i, 0)),
            _const_spec((1, D)),
            _const_spec(win.shape),
            _const_spec((1, C_Q_RANK)),
            _const_spec((1, C_KV_RANK)),
            _const_spec(wuqT.shape),
            _const_spec(wkn.shape),
            _const_spec(wvT.shape),
            pl.BlockSpec((hr, tm), lambda b, i: (0, i)),
            pl.BlockSpec((hr, tm), lambda b, i: (0, i)),
            pl.BlockSpec((tm, C_PAD), lambda b, i: (i, 0)),
            pl.BlockSpec((tm, C_PAD), lambda b, i: (i, 0)),
        ],
        out_specs=[
            pl.BlockSpec((1, C_HEADS * C_PAD, tm), lambda b, i: (b, 0, i)),
            pl.BlockSpec((1, tm, C_HEADS * C_PAD), lambda b, i: (b, i, 0)),
            pl.BlockSpec((1, C_HEADS * (C_VDIM + ONES_ROWS), tm), lambda b, i: (b, 0, i)),
        ],
        out_shape=[
            jax.ShapeDtypeStruct((B, C_HEADS * C_PAD, S), BF16),
            jax.ShapeDtypeStruct((B, S, C_HEADS * C_PAD), BF16),
            jax.ShapeDtypeStruct((B, C_HEADS * (C_VDIM + ONES_ROWS), S), BF16),
        ],
        compiler_params=_params(2),
        name="odd_in",
    )(x, g, win, gq, gkv, wuqT, wkn, wvT, cosq, sinq, cosk, sink)


def _memkv_kernel(mem_ref, g_ref, w_ref, kv_ref):
    mn = _rms_rows(mem_ref[0], g_ref[...]).astype(BF16)
    kv_ref[0] = jnp.dot(mn, w_ref[...], preferred_element_type=F32).astype(BF16)


def _memkv(mem, g, w):
    B, M, D = mem.shape
    N = w.shape[1]
    return pl.pallas_call(
        _memkv_kernel,
        grid=(B,),
        in_specs=[pl.BlockSpec((1, M, D), lambda b: (b, 0, 0)), _const_spec((1, D)), _const_spec((D, N))],
        out_specs=pl.BlockSpec((1, M, N), lambda b: (b, 0, 0)),
        out_shape=jax.ShapeDtypeStruct((B, M, N), BF16),
        compiler_params=_params(1),
        name="mem_kv",
    )(mem, g, w)


def _post_mix_kernel(*refs, n_mix):
    x_ref = refs[0]
    o_refs = refs[1:1 + n_mix]
    w_refs = refs[1 + n_mix:1 + 2 * n_mix]
    gc_ref, wq_ref, kv_ref, wo_ref, out_ref = refs[1 + 2 * n_mix:]
    rows = _row_groups(x_ref.shape[1])
    xs = []
    for r in rows:
        x = x_ref[0, r, :]
        for o_ref, w_ref in zip(o_refs, w_refs):
            x = x + jnp.dot(o_ref[0, r, :], w_ref[...], preferred_element_type=F32)
        xs.append(x)
    qs = []
    for x in xs:
        hc = _rms_rows(x, gc_ref[...]).astype(BF16)
        qs.append((jnp.dot(hc, wq_ref[...], preferred_element_type=F32) * (X_DIM ** -0.5)).astype(BF16))
    heads = [[] for _ in rows]
    for h in range(X_HEADS):
        k_h = kv_ref[0, :, h * X_DIM:(h + 1) * X_DIM]
        v_h = kv_ref[0, :, D_MODEL + h * X_DIM:D_MODEL + (h + 1) * X_DIM]
        for g, q in enumerate(qs):
            s = lax.dot_general(q[:, h * X_DIM:(h + 1) * X_DIM], k_h, _NT, preferred_element_type=F32)
            p = jnp.exp(s - jnp.max(s, axis=-1, keepdims=True))
            l = jnp.sum(p, axis=-1, keepdims=True)
            heads[g].append((jnp.dot(p.astype(BF16), v_h, preferred_element_type=F32) / l).astype(BF16))
    for r, x, hs in zip(rows, xs, heads):
        o = jnp.concatenate(hs, axis=-1)
        out_ref[0, r, :] = x + jnp.dot(o, wo_ref[...], preferred_element_type=F32)


def _post_mix(x, mixes, weights, gc, wq, kv, wo):
    B, S, D = x.shape
    tm = min(GROUPED_TOKEN_TILE, S)
    n = len(mixes)
    M = kv.shape[1]
    tok = lambda b, i: (b, i, 0)
    return pl.pallas_call(
        functools.partial(_post_mix_kernel, n_mix=n),
        grid=(B, S // tm),
        in_specs=([pl.BlockSpec((1, tm, D), tok)]
                  + [pl.BlockSpec((1, tm, m.shape[2]), tok) for m in mixes]
                  + [_const_spec(w.shape) for w in weights]
                  + [_const_spec((1, D)), _const_spec((D, D)),
                     pl.BlockSpec((1, M, 2 * D), lambda b, i: (b, 0, 0)),
                     _const_spec((D, D))]),
        out_specs=pl.BlockSpec((1, tm, D), tok),
        out_shape=jax.ShapeDtypeStruct((B, S, D), F32),
        compiler_params=_params(2),
        name="post_mix",
    )(x, *mixes, *weights, gc, wq, kv, wo)


def _ffn_kernel(x_ref, g_ref, wgu_ref, wd_ref, gf_ref, out_ref, *, final_norm):
    x = x_ref[0]
    xn = _rms_rows(x, g_ref[...]).astype(BF16)
    acc = x
    for c in range(D_FF // FF_CHUNK):
        lo = c * FF_CHUNK
        gate = jnp.dot(xn, wgu_ref[:, lo:lo + FF_CHUNK], preferred_element_type=F32)
        up = jnp.dot(xn, wgu_ref[:, D_FF + lo:D_FF + lo + FF_CHUNK], preferred_element_type=F32)
        hidden = (jax.nn.silu(gate) * up).astype(BF16)
        acc = acc + jnp.dot(hidden, wd_ref[lo:lo + FF_CHUNK, :], preferred_element_type=F32)
    if final_norm:
        acc = _rms_rows(acc, gf_ref[...])
    out_ref[0] = acc


def _ffn(x, g, wgu, wd, gf, final_norm):
    B, S, D = x.shape
    tm = min(WIDE_TOKEN_TILE, S)
    tok = lambda b, i: (b, i, 0)
    return pl.pallas_call(
        functools.partial(_ffn_kernel, final_norm=final_norm),
        grid=(B, S // tm),
        in_specs=[pl.BlockSpec((1, tm, D), tok), _const_spec((1, D)),
                  _const_spec(wgu.shape), _const_spec(wd.shape), _const_spec((1, D))],
        out_specs=pl.BlockSpec((1, tm, D), tok),
        out_shape=jax.ShapeDtypeStruct((B, S, D), F32),
        compiler_params=_params(2),
        name="ffn",
    )(x, g, wgu, wd, gf)


def _rope_freqs(n_pairs):
    return ROPE_THETA ** (-jnp.arange(n_pairs, dtype=F32) / n_pairs)


def _axial_angles(S):
    rows = S // GRID_W
    r = jnp.repeat(jnp.arange(rows, dtype=F32), GRID_W)
    c = jnp.tile(jnp.arange(GRID_W, dtype=F32), rows)
    f = _rope_freqs(A_DIM // 4)
    return jnp.concatenate([r[:, None] * f, c[:, None] * f], axis=-1)


def _linear_angles(S, dim):
    t = jnp.arange(S, dtype=F32)
    return t[:, None] * _rope_freqs(dim // 2)


def _deinterleave(n):
    return np.concatenate([np.arange(0, n, 2), np.arange(1, n, 2)])


def _prep_even(w_in, gq, gk):
    perm = _deinterleave(A_DIM)
    o = 0
    w_qa = w_in[:, o:o + A_Q].reshape(D_MODEL, A_HEADS, A_DIM)[:, :, perm].reshape(D_MODEL, A_Q)
    o += A_Q
    w_ka = w_in[:, o:o + A_KV].reshape(D_MODEL, A_KV_HEADS, A_DIM)[:, :, perm].reshape(D_MODEL, A_KV)
    o += A_KV
    w_va = w_in[:, o:o + A_KV]
    o += A_KV
    w_qb = w_in[:, o:o + B_QK]
    o += B_QK
    w_kb = w_in[:, o:o + B_QK]
    o += B_QK
    w_vb = w_in[:, o:o + B_V]
    wt = jnp.concatenate([w_qa, w_ka, w_va, w_qb, w_vb], axis=1).T.astype(BF16)
    return wt, w_kb.astype(BF16), gq[perm].reshape(A_DIM, 1), gk[perm].reshape(A_DIM, 1)


def _prep_odd(w_in, w_uq, w_ukv):
    hr = C_ROPE // 2
    perm = _deinterleave(C_ROPE)
    w_kr = w_in[:, C_Q_RANK + C_KV_RANK:][:, perm]
    w_kr_rot = jnp.concatenate([-w_kr[:, hr:], w_kr[:, :hr]], axis=1)
    win = jnp.concatenate(
        [w_in[:, :C_Q_RANK + C_KV_RANK], w_kr, w_kr_rot,
         jnp.zeros((D_MODEL, C_PAD - 2 * C_ROPE), F32)], axis=1).astype(BF16)
    wq = w_uq.reshape(C_Q_RANK, C_HEADS, C_NOPE + C_ROPE)
    wq = jnp.concatenate(
        [wq[:, :, :C_NOPE], wq[:, :, C_NOPE:][:, :, perm],
         jnp.zeros((C_Q_RANK, C_HEADS, C_PAD - C_NOPE - C_ROPE), F32)], axis=2)
    wuqT = wq.reshape(C_Q_RANK, C_HEADS * C_PAD).T.astype(BF16)
    wkv = w_ukv.reshape(C_KV_RANK, C_HEADS, C_NOPE + C_VDIM)
    wkn = jnp.concatenate(
        [wkv[:, :, :C_NOPE], jnp.zeros((C_KV_RANK, C_HEADS, C_PAD - C_NOPE), F32)], axis=2)
    wkn = wkn.reshape(C_KV_RANK, C_HEADS * C_PAD).astype(BF16)
    wvT = wkv[:, :, C_NOPE:].reshape(C_KV_RANK, C_HEADS * C_VDIM).T.astype(BF16)
    return win, wuqT, wkn, wvT


def _trunk(x, mem, p):
    B, S, D = x.shape
    depth = p['norm_mix'].shape[0]
    row = lambda v: v.reshape(1, -1)

    ang_a = _axial_angles(S)
    cos_a, sin_a = jnp.cos(ang_a).T, jnp.sin(ang_a).T
    ang_l = _linear_angles(S, C_ROPE)
    cos_l, sin_l = jnp.cos(ang_l), jnp.sin(ang_l)
    zpad = jnp.zeros((S, C_PAD - 2 * C_ROPE), F32)
    zrope = jnp.zeros((S, C_ROPE), F32)
    cos_k = jnp.concatenate([cos_l, cos_l, zrope, zpad], axis=1)
    sin_k = jnp.concatenate([zrope, sin_l, sin_l, zpad], axis=1)
    slopes = jnp.asarray(2.0 ** (-8.0 * np.arange(1, B_HEADS + 1) / B_HEADS), dtype=F32)

    for layer in range(depth):
        if layer % 2 == 0:
            e = layer // 2
            wt, wkb, gq, gk = _prep_even(p['e_w_in'][e], p['e_q_norm'][e], p['e_k_norm'][e])
            qaT, ka, vaT, qbT, kb, vbT = _even_in(
                x, row(p['norm_mix'][layer]), wt, wkb, gq, gk, cos_a, sin_a)
            oa = _gqa(qaT, ka, vaT)
            lam_init = 0.8 - 0.6 * math.exp(-0.3 * layer)
            ob = _diff(slopes, row(p['e_lam_q1'][e]), row(p['e_lam_k1'][e]),
                       row(p['e_lam_q2'][e]), row(p['e_lam_k2'][e]),
                       p['e_subln'][e].reshape(B_VDIM, 1), qbT, kb, vbT, lam_init)
            w_out = p['e_w_out'][e].astype(BF16)
            mixes, weights = [oa, ob], [w_out[:A_Q], w_out[A_Q:]]
        else:
            o = layer // 2
            win, wuqT, wkn, wvT = _prep_odd(p['o_w_in'][o], p['o_w_uq'][o], p['o_w_ukv'][o])
            qT, k, vT = _odd_in(x, row(p['norm_mix'][layer]), win, row(p['o_q_norm'][o]),
                                row(p['o_kv_norm'][o]), wuqT, wkn, wvT,
                                cos_l.T, sin_l.T, cos_k, sin_k)
            mixes, weights = [_mla(qT, k, vT)], [p['o_w_out'][o].astype(BF16)]
        kv = _memkv(mem, row(p['norm_mem'][layer]), p['w_ckv'][layer].astype(BF16))
        x = _post_mix(x, mixes, weights, row(p['norm_cross'][layer]),
                      p['w_cq'][layer].astype(BF16), kv, p['w_co'][layer].astype(BF16))
        x = _ffn(x, row(p['norm_ffn'][layer]), p['w_gu'][layer].astype(BF16),
                 p['w_down'][layer].astype(BF16), row(p['final_norm']),
                 final_norm=(layer == depth - 1))
    return x


def kernel(x_prompt, x_sample, mem_prompt, mem_sample, norm_mix, e_w_in, e_q_norm, e_k_norm, e_lam_q1, e_lam_k1, e_lam_q2, e_lam_k2, e_subln, e_w_out, o_w_in, o_q_norm, o_kv_norm, o_w_uq, o_w_ukv, o_w_out, norm_cross, norm_mem, w_cq, w_ckv, w_co, norm_ffn, w_gu, w_down, final_norm):
    p = dict(norm_mix=norm_mix, e_w_in=e_w_in, e_q_norm=e_q_norm, e_k_norm=e_k_norm,
             e_lam_q1=e_lam_q1, e_lam_k1=e_lam_k1, e_lam_q2=e_lam_q2, e_lam_k2=e_lam_k2,
             e_subln=e_subln, e_w_out=e_w_out, o_w_in=o_w_in, o_q_norm=o_q_norm,
             o_kv_norm=o_kv_norm, o_w_uq=o_w_uq, o_w_ukv=o_w_ukv, o_w_out=o_w_out,
             norm_cross=norm_cross, norm_mem=norm_mem, w_cq=w_cq, w_ckv=w_ckv, w_co=w_co,
             norm_ffn=norm_ffn, w_gu=w_gu, w_down=w_down, final_norm=final_norm)
    return (_trunk(x_prompt, mem_prompt, p), _trunk(x_sample, mem_sample, p))
```

```python
import functools
import math

import jax
import jax.numpy as jnp
import numpy as np
from jax import lax
from jax.experimental import pallas as pl
from jax.experimental.pallas import tpu as pltpu

F32 = jnp.float32
BF16 = jnp.bfloat16

D_MODEL = 1024
GRID_W = 64
EPS = 1e-6
ROPE_THETA = 10000.0
A_HEADS, A_KV_HEADS, A_DIM = 8, 2, 64
A_GROUP = A_HEADS // A_KV_HEADS
B_HEADS, B_DIM = 4, 64
B_VDIM = 2 * B_DIM
A_Q = A_HEADS * A_DIM
A_KV = A_KV_HEADS * A_DIM
B_QK = B_HEADS * 2 * B_DIM
B_V = B_HEADS * B_VDIM
C_HEADS, C_Q_RANK, C_KV_RANK, C_NOPE, C_ROPE, C_VDIM = 16, 384, 256, 64, 32, 64
C_PAD = 128
X_HEADS = 4
X_DIM = D_MODEL // X_HEADS
D_FF = ((-(-8 * D_MODEL // 3) + 255) // 256) * 256
FF_CHUNK = 256
NEG_BIG = float(np.finfo(np.float32).min)
LOG2E = math.log2(math.e)

GROUPED_TOKEN_TILE = 1024
WIDE_TOKEN_TILE = 512
Q_TILE = 256
KEY_CHUNK = 256
VMEM_LIMIT = 48 * 1024 * 1024

_NT = (((1,), (1,)), ((), ()))


def _params(n_parallel, n_arbitrary=0):
    return pltpu.CompilerParams(
        dimension_semantics=("parallel",) * n_parallel + ("arbitrary",) * n_arbitrary,
        vmem_limit_bytes=VMEM_LIMIT)


def _rms_rows(x, g):
    ms = jnp.mean(x * x, axis=-1, keepdims=True)
    return (x * lax.rsqrt(ms + EPS)) * g


def _rms_cols(x, g):
    ms = jnp.mean(x * x, axis=0, keepdims=True)
    return (x * lax.rsqrt(ms + EPS)) * g


def _const_spec(shape):
    nd = len(shape)
    return pl.BlockSpec(shape, lambda *_: (0,) * nd)


ONES_ROWS = 16


def _store_values(ref, vT, n_heads, dv, cols):
    ext = dv + ONES_ROWS
    ones = jnp.ones((ONES_ROWS, vT.shape[1]), BF16)
    for h in range(n_heads):
        ref[0, h * ext:h * ext + dv, cols] = vT[h * dv:(h + 1) * dv].astype(BF16)
        ref[0, h * ext + dv:(h + 1) * ext, cols] = ones


ROW_GROUPS = 2


def _row_groups(tm):
    n = ROW_GROUPS if tm % (ROW_GROUPS * 128) == 0 else 1
    return [slice(g * tm // n, (g + 1) * tm // n) for g in range(n)]


def _value_rows(h, dv):
    ext = dv + ONES_ROWS
    return slice(h * ext, (h + 1) * ext)


def _even_in_kernel(x_ref, g_ref, wt_ref, wkb_ref, gq_ref, gk_ref, cos_ref, sin_ref,
                    qaT_ref, ka_ref, vaT_ref, qbT_ref, kb_ref, vbT_ref):
    half = A_DIM // 2
    groups = _row_groups(x_ref.shape[1])
    xns = [_rms_rows(x_ref[0, r, :], g_ref[...]).astype(BF16) for r in groups]
    yTs = [lax.dot_general(wt_ref[...], xn, _NT, preferred_element_type=F32) for xn in xns]
    for r, xn in zip(groups, xns):
        kb_ref[0, r, :] = jnp.dot(xn, wkb_ref[...], preferred_element_type=F32).astype(BF16)

    scale_a = A_DIM ** -0.5 * LOG2E
    for r, yT in zip(groups, yTs):
        cos = cos_ref[:, r]
        sin = sin_ref[:, r]

        def norm_rope(xh, g, scale):
            y = _rms_cols(xh, g)
            e, o = y[:half], y[half:]
            return jnp.concatenate([e * cos - o * sin, e * sin + o * cos], axis=0) * scale

        for h in range(A_HEADS):
            qh = norm_rope(yT[h * A_DIM:(h + 1) * A_DIM], gq_ref[...], scale_a)
            qaT_ref[0, h * A_DIM:(h + 1) * A_DIM, r] = qh.astype(BF16)
        kT = jnp.concatenate(
            [norm_rope(yT[A_Q + h * A_DIM:A_Q + (h + 1) * A_DIM], gk_ref[...], 1.0)
             for h in range(A_KV_HEADS)], axis=0)
        ka_ref[0, r, :] = kT.T.astype(BF16)
        r0 = A_Q + A_KV
        _store_values(vaT_ref, yT[r0:r0 + A_KV], A_KV_HEADS, A_DIM, r)
        r0 += A_KV
        qbT_ref[0, :, r] = (yT[r0:r0 + B_QK] * (B_DIM ** -0.5 * LOG2E)).astype(BF16)
        r0 += B_QK
        _store_values(vbT_ref, yT[r0:r0 + B_V], B_HEADS, B_VDIM, r)


def _even_in(x, g, wt, wkb, gq, gk, cosT, sinT):
    B, S, D = x.shape
    tm = min(GROUPED_TOKEN_TILE, S)
    rows = wt.shape[0]
    va_rows = A_KV_HEADS * (A_DIM + ONES_ROWS)
    vb_rows = B_HEADS * (B_VDIM + ONES_ROWS)
    return pl.pallas_call(
        _even_in_kernel,
        grid=(B, S // tm),
        in_specs=[
            pl.BlockSpec((1, tm, D), lambda b, i: (b, i, 0)),
            _const_spec((1, D)),
            _const_spec((rows, D)),
            _const_spec((D, B_QK)),
            _const_spec((A_DIM, 1)),
            _const_spec((A_DIM, 1)),
            pl.BlockSpec((A_DIM // 2, tm), lambda b, i: (0, i)),
            pl.BlockSpec((A_DIM // 2, tm), lambda b, i: (0, i)),
        ],
        out_specs=[
            pl.BlockSpec((1, A_Q, tm), lambda b, i: (b, 0, i)),
            pl.BlockSpec((1, tm, A_KV), lambda b, i: (b, i, 0)),
            pl.BlockSpec((1, va_rows, tm), lambda b, i: (b, 0, i)),
            pl.BlockSpec((1, B_QK, tm), lambda b, i: (b, 0, i)),
            pl.BlockSpec((1, tm, B_QK), lambda b, i: (b, i, 0)),
            pl.BlockSpec((1, vb_rows, tm), lambda b, i: (b, 0, i)),
        ],
        out_shape=[
            jax.ShapeDtypeStruct((B, A_Q, S), BF16),
            jax.ShapeDtypeStruct((B, S, A_KV), BF16),
            jax.ShapeDtypeStruct((B, va_rows, S), BF16),
            jax.ShapeDtypeStruct((B, B_QK, S), BF16),
            jax.ShapeDtypeStruct((B, S, B_QK), BF16),
            jax.ShapeDtypeStruct((B, vb_rows, S), BF16),
        ],
        compiler_params=_params(2),
        name="even_in",
    )(x, g, wt, wkb, gq, gk, cosT, sinT)


def _chain_init(n, dv, tq):
    return tuple((jnp.full((1, tq), NEG_BIG, F32), jnp.zeros((dv + ONES_ROWS, tq), F32))
                 for _ in range(n))


def _chain_out(acc, dv):
    return acc[:dv] / acc[dv:dv + 1]


def _chunk_start(c, tkc):
    return c * tkc if isinstance(c, int) else pl.multiple_of(c * tkc, tkc)


def _key_chunk(S):
    return min(KEY_CHUNK, S // 2)


def _attend(n_chunks, n_chains, dv, tq, first_tile, prep, score, value, scratch, offset=None):
    assert n_chunks % 2 == 0
    max_ref, s_bufs = scratch[0], (scratch[1:1 + n_chains], scratch[1 + n_chains:])

    def produce(ctx, j, nxt, slot):
        s = score(ctx, j, nxt)
        s_bufs[slot][j][...] = s
        top = jnp.max(s, axis=0, keepdims=True)
        r = None if offset is None else offset(ctx, j, nxt)
        return (top, None) if r is None else (top - r, r)

    @pl.when(first_tile)
    def _():
        ctx0 = prep(0, False)
        for j in range(n_chains):
            max_ref[j] = produce(ctx0, j, False, 0)[0]

    maxes = [(max_ref[j], None) for j in range(n_chains)]
    chains = list(_chain_init(n_chains, dv, tq))
    for c in range(n_chunks):
        slot = c % 2
        nxt = c == n_chunks - 1
        ctx = prep(0 if nxt else c + 1, nxt)
        for j in range(n_chains):
            produced = produce(ctx, j, nxt, 1 - slot)
            m, acc = chains[j]
            top, r = maxes[j]
            m_new = jnp.maximum(m, top)
            alpha = jnp.exp2(m - m_new)
            shift = m_new if r is None else m_new + r
            p = jnp.exp2(s_bufs[slot][j][...] - shift).astype(BF16)
            acc = alpha * acc + jnp.dot(value(c, j), p, preferred_element_type=F32)
            chains[j] = (m_new, acc)
            maxes[j] = produced
    for j in range(n_chains):
        max_ref[j] = maxes[j][0]
    return [_chain_out(acc, dv) for _, acc in chains]


def _attend_scratch(n_chains, tkc, tq):
    return ([pltpu.VMEM((n_chains, 1, tq), F32)]
            + [pltpu.VMEM((tkc, tq), F32) for _ in range(2 * n_chains)])


def _next_tile(n_tiles):
    return lambda i: jnp.minimum(i + 1, n_tiles - 1)


def _gqa_kernel(qT_ref, qT_next_ref, k_ref, vT_ref, o_ref, *scratch, tkc):
    S = k_ref.shape[1]
    tq = qT_ref.shape[2]

    def padded_queries(ref):
        pads = []
        for h in range(A_HEADS):
            q = ref[0, h * A_DIM:(h + 1) * A_DIM, :]
            zero = jnp.zeros_like(q)
            pads.append(jnp.concatenate([q, zero] if h < A_GROUP else [zero, q], axis=0))
        return pads

    q_pads = {False: padded_queries(qT_ref), True: padded_queries(qT_next_ref)}

    def prep(c, nxt):
        return k_ref[0, pl.ds(_chunk_start(c, tkc), tkc), :]

    def score(k_c, h, nxt):
        return jnp.dot(k_c, q_pads[nxt][h], preferred_element_type=F32)

    def value(c, h):
        g = h // A_GROUP
        return vT_ref[0, _value_rows(g, A_DIM), pl.ds(_chunk_start(c, tkc), tkc)]

    outs = _attend(S // tkc, A_HEADS, A_DIM, tq, pl.program_id(1) == 0,
                   prep, score, value, scratch)
    o_ref[0] = jnp.concatenate(outs, axis=0).T.astype(o_ref.dtype)


def _gqa(qaT, ka, vaT):
    B, _, S = qaT.shape
    tq = min(Q_TILE, S)
    tkc = _key_chunk(S)
    nxt = _next_tile(S // tq)
    return pl.pallas_call(
        functools.partial(_gqa_kernel, tkc=tkc),
        grid=(B, S // tq),
        in_specs=[
            pl.BlockSpec((1, A_Q, tq), lambda b, i: (b, 0, i)),
            pl.BlockSpec((1, A_Q, tq), lambda b, i: (b, 0, nxt(i))),
            pl.BlockSpec((1, S, A_KV), lambda b, i: (b, 0, 0)),
            pl.BlockSpec((1, A_KV_HEADS * (A_DIM + ONES_ROWS), S), lambda b, i: (b, 0, 0)),
        ],
        out_specs=pl.BlockSpec((1, tq, A_Q), lambda b, i: (b, i, 0)),
        out_shape=jax.ShapeDtypeStruct((B, S, A_Q), BF16),
        scratch_shapes=_attend_scratch(A_HEADS, tkc, tq),
        compiler_params=_params(1, 1),
        name="gqa_attn",
    )(qaT, qaT, ka, vaT)


DIFF_HEADS_PER_STEP = 4


ALIBI_PIECES = 3


def _diff_kernel(slopes_ref, lq1_ref, lk1_ref, lq2_ref, lk2_ref, gsub_ref,
                 qT_ref, qT_next_ref, k_ref, kpos_ref, vT_ref, o_ref, *scratch, tkc, lam_init):
    hp = pl.program_id(1)
    i = pl.program_id(2)
    S = k_ref.shape[1]
    tq = qT_ref.shape[2]
    n_chunks = S // tkc
    n_tiles = S // tq
    pair = 2 * B_DIM
    coefs = [slopes_ref[DIFF_HEADS_PER_STEP * hp + hh] * LOG2E for hh in range(DIFF_HEADS_PER_STEP)]

    def padded_queries(ref):
        pads = []
        for hh in range(DIFF_HEADS_PER_STEP):
            q = ref[0, hh * pair:(hh + 1) * pair, :]
            rows = lax.broadcasted_iota(jnp.int32, q.shape, 0)
            zero = jnp.zeros_like(q)
            pads += [jnp.where(rows < B_DIM, q, zero), jnp.where(rows >= B_DIM, q, zero)]
        return pads

    q_pads = {False: padded_queries(qT_ref), True: padded_queries(qT_next_ref)}
    dmat = (lax.broadcasted_iota(jnp.int32, (tkc, tq), 1)
            - lax.broadcasted_iota(jnp.int32, (tkc, tq), 0)).astype(F32)
    q_local = lax.broadcasted_iota(jnp.int32, (1, tq), 1).astype(F32)
    piece_rows = lax.broadcasted_iota(jnp.int32, (pair, tq), 0) < ALIBI_PIECES

    def chunk_of(c, tile):
        home = (tile * tq) // tkc
        return home, (home if c == 0 else lax.rem(home + c, n_chunks))

    def prep(c, nxt):
        tile = jnp.minimum(i + 1, n_tiles - 1) if nxt else i
        home, chunk = chunk_of(c, tile)
        start = pl.multiple_of(chunk * tkc, tkc)
        q_start = (tile * tq).astype(F32)
        if c == 0:
            dist = jnp.abs(dmat + (q_start - start.astype(F32)))
            return dict(start=start, biases=[dist * -cf for cf in coefs])
        sign = jnp.where(chunk < home, 1.0, -1.0)
        ext = jnp.where(piece_rows, sign, 0.0).astype(BF16)
        q_pos = q_start + q_local
        return dict(start=start, ext=ext, offsets=[(sign * cf) * q_pos for cf in coefs])

    def score(ctx, n, nxt):
        hh = n // 2
        k_c = k_ref[0, pl.ds(ctx["start"], tkc), hh * pair:(hh + 1) * pair]
        q_pad = q_pads[nxt][n]
        if "biases" in ctx:
            return jnp.dot(k_c, q_pad, preferred_element_type=F32) + ctx["biases"][hh]
        keys = jnp.concatenate([k_c, kpos_ref[hh, pl.ds(ctx["start"], tkc), :]], axis=1)
        queries = jnp.concatenate([q_pad, ctx["ext"]], axis=0)
        return jnp.dot(keys, queries, preferred_element_type=F32)

    def offset(ctx, n, nxt):
        return None if "biases" in ctx else ctx["offsets"][n // 2]

    def value(c, n):
        _, chunk = chunk_of(c, i)
        start = pl.multiple_of(chunk * tkc, tkc)
        return vT_ref[0, _value_rows(n // 2, B_VDIM), pl.ds(start, tkc)]

    n_chains = 2 * DIFF_HEADS_PER_STEP
    parts = _attend(n_chunks, n_chains, B_VDIM, tq, i == 0, prep, score, value, scratch, offset)

    lam = (jnp.exp(jnp.sum(lq1_ref[...] * lk1_ref[...], axis=-1, keepdims=True))
           - jnp.exp(jnp.sum(lq2_ref[...] * lk2_ref[...], axis=-1, keepdims=True)) + lam_init)
    outs = []
    for hh in range(DIFF_HEADS_PER_STEP):
        o = parts[2 * hh] - lam * parts[2 * hh + 1]
        outs.append(_rms_cols(o, gsub_ref[...]) * (1.0 - lam_init))
    o_ref[0] = jnp.concatenate(outs, axis=0).T.astype(o_ref.dtype)


def _alibi_key_table(slopes, S):
    a = (slopes * LOG2E)[:, None] * jnp.arange(S, dtype=F32)[None, :]
    pieces, rest = [], a
    for _ in range(ALIBI_PIECES):
        piece = lax.bitcast_convert_type(
            lax.bitcast_convert_type(rest, jnp.uint32) & jnp.uint32(0xFFFF0000), F32)
        pieces.append(piece.astype(BF16))
        rest = rest - piece
    table = jnp.stack(pieces, axis=-1)
    return jnp.pad(table, ((0, 0), (0, 0), (0, 2 * B_DIM - ALIBI_PIECES)))


def _diff(slopes, lq1, lk1, lq2, lk2, gsub, qbT, kb, vbT, lam_init):
    B, _, S = qbT.shape
    tq = min(Q_TILE, S)
    tkc = _key_chunk(S)
    assert tkc % tq == 0
    hs = DIFF_HEADS_PER_STEP
    nxt = _next_tile(S // tq)
    kpos = _alibi_key_table(slopes, S)
    return pl.pallas_call(
        functools.partial(_diff_kernel, tkc=tkc, lam_init=lam_init),
        grid=(B, B_HEADS // hs, S // tq),
        in_specs=[
            pl.BlockSpec(memory_space=pltpu.SMEM),
            _const_spec((1, B_DIM)), _const_spec((1, B_DIM)),
            _const_spec((1, B_DIM)), _const_spec((1, B_DIM)),
            _const_spec((B_VDIM, 1)),
            pl.BlockSpec((1, hs * 2 * B_DIM, tq), lambda b, h, i: (b, h, i)),
            pl.BlockSpec((1, hs * 2 * B_DIM, tq), lambda b, h, i: (b, h, nxt(i))),
            pl.BlockSpec((1, S, hs * 2 * B_DIM), lambda b, h, i: (b, 0, h)),
            pl.BlockSpec((hs, S, 2 * B_DIM), lambda b, h, i: (h, 0, 0)),
            pl.BlockSpec((1, hs * (B_VDIM + ONES_ROWS), S), lambda b, h, i: (b, h, 0)),
        ],
        out_specs=pl.BlockSpec((1, tq, hs * B_VDIM), lambda b, h, i: (b, i, h)),
        out_shape=jax.ShapeDtypeStruct((B, S, B_V), BF16),
        scratch_shapes=_attend_scratch(2 * hs, tkc, tq),
        compiler_params=_params(2, 1),
        name="diff_attn",
    )(slopes, lq1, lk1, lq2, lk2, gsub, qbT, qbT, kb, kpos, vbT)


MLA_HEADS_PER_STEP = 4
MLA_SUBTILES = 2


def _mla_kernel(qT_ref, qT_next_ref, k_ref, vT_ref, o_ref, *scratch, tkc, n_sub):
    S = k_ref.shape[1]
    tqs = qT_ref.shape[2] // n_sub
    nh = MLA_HEADS_PER_STEP
    q_refs = {False: qT_ref, True: qT_next_ref}

    def prep(c, nxt):
        return _chunk_start(c, tkc)

    def score(start, n, nxt):
        j, t = divmod(n, n_sub)
        k_c = k_ref[0, pl.ds(start, tkc), j * C_PAD:(j + 1) * C_PAD]
        q_pad = q_refs[nxt][0, j * C_PAD:(j + 1) * C_PAD, t * tqs:(t + 1) * tqs]
        return jnp.dot(k_c, q_pad, preferred_element_type=F32)

    def value(c, n):
        return vT_ref[0, _value_rows(n // n_sub, C_VDIM), pl.ds(_chunk_start(c, tkc), tkc)]

    outs = _attend(S // tkc, nh * n_sub, C_VDIM, tqs, pl.program_id(2) == 0,
                   prep, score, value, scratch)
    for t in range(n_sub):
        sub = jnp.concatenate([outs[j * n_sub + t] for j in range(nh)], axis=0)
        o_ref[0, t * tqs:(t + 1) * tqs, :] = sub.T.astype(o_ref.dtype)


def _mla(qT, k, vT):
    B, _, S = qT.shape
    n_sub = MLA_SUBTILES if S >= MLA_SUBTILES * Q_TILE else 1
    tq = min(Q_TILE, S) * n_sub
    tkc = _key_chunk(S)
    hp = MLA_HEADS_PER_STEP
    nxt = _next_tile(S // tq)
    return pl.pallas_call(
        functools.partial(_mla_kernel, tkc=tkc, n_sub=n_sub),
        grid=(B, C_HEADS // hp, S // tq),
        in_specs=[
            pl.BlockSpec((1, hp * C_PAD, tq), lambda b, h, i: (b, h, i)),
            pl.BlockSpec((1, hp * C_PAD, tq), lambda b, h, i: (b, h, nxt(i))),
            pl.BlockSpec((1, S, hp * C_PAD), lambda b, h, i: (b, 0, h)),
            pl.BlockSpec((1, hp * (C_VDIM + ONES_ROWS), S), lambda b, h, i: (b, h, 0)),
        ],
        out_specs=pl.BlockSpec((1, tq, hp * C_VDIM), lambda b, h, i: (b, i, h)),
        out_shape=jax.ShapeDtypeStruct((B, S, C_HEADS * C_VDIM), BF16),
        scratch_shapes=_attend_scratch(hp * n_sub, tkc, tq // n_sub),
        compiler_params=_params(2, 1),
        name="mla_attn",
    )(qT, qT, k, vT)


def _odd_in_kernel(x_ref, g_ref, win_ref, gq_ref, gkv_ref, wuqT_ref, wkn_ref, wvT_ref,
                   cosq_ref, sinq_ref, cosk_ref, sink_ref, qT_ref, k_ref, vT_ref):
    hr = C_ROPE // 2
    groups = _row_groups(x_ref.shape[1])
    xns = [_rms_rows(x_ref[0, r, :], g_ref[...]).astype(BF16) for r in groups]
    lat = [jnp.dot(xn, win_ref[...], preferred_element_type=F32) for xn in xns]
    cqns = [_rms_rows(a[:, :C_Q_RANK], gq_ref[...]).astype(BF16) for a in lat]
    ckvns = [_rms_rows(a[:, C_Q_RANK:C_Q_RANK + C_KV_RANK], gkv_ref[...]).astype(BF16) for a in lat]

    qTs = [lax.dot_general(wuqT_ref[...], cqn, _NT, preferred_element_type=F32) for cqn in cqns]
    for rows, qT in zip(groups, qTs):
        qT = qT * ((C_NOPE + C_ROPE) ** -0.5 * LOG2E)
        cq, sq = cosq_ref[:, rows], sinq_ref[:, rows]
        for h in range(C_HEADS):
            r = h * C_PAD
            e = qT[r + C_NOPE:r + C_NOPE + hr]
            o = qT[r + C_NOPE + hr:r + C_NOPE + C_ROPE]
            head = jnp.concatenate(
                [qT[r:r + C_NOPE], e * cq - o * sq, e * sq + o * cq, qT[r + C_NOPE + C_ROPE:r + C_PAD]],
                axis=0)
            qT_ref[0, r:r + C_PAD, rows] = head.astype(BF16)

    kns = [jnp.dot(ckvn, wkn_ref[...], preferred_element_type=F32) for ckvn in ckvns]
    for rows, a, kn in zip(groups, lat, kns):
        kblk = a[:, C_Q_RANK + C_KV_RANK:]
        t = kblk * cosk_ref[rows, :] + kblk * sink_ref[rows, :]
        lane = lax.broadcasted_iota(jnp.int32, t.shape, 1)
        kr = jnp.where(lane < C_ROPE, t + pltpu.roll(t, C_PAD - C_ROPE, axis=1), 0.0)
        kr = pltpu.roll(kr, C_NOPE, axis=1)
        for h in range(C_HEADS):
            k_ref[0, rows, h * C_PAD:(h + 1) * C_PAD] = (
                kn[:, h * C_PAD:(h + 1) * C_PAD] + kr).astype(BF16)

    for rows, ckvn in zip(groups, ckvns):
        _store_values(vT_ref, lax.dot_general(wvT_ref[...], ckvn, _NT, preferred_element_type=F32),
                      C_HEADS, C_VDIM, rows)


def _odd_in(x, g, win, gq, gkv, wuqT, wkn, wvT, cosq, sinq, cosk, sink):
    B, S, D = x.shape
    tm = min(GROUPED_TOKEN_TILE, S)
    hr = C_ROPE // 2
    return pl.pallas_call(
        _odd_in_kernel,
        grid=(B, S // tm),
        in_specs=[
            pl.BlockSpec((1, tm, D), lambda b, i: (b, i, 0)),
            _const_spec((1, D)),
            _const_spec(win.shape),
            _const_spec((1, C_Q_RANK)),
            _const_spec((1, C_KV_RANK)),
            _const_spec(wuqT.shape),
            _const_spec(wkn.shape),
            _const_spec(wvT.shape),
            pl.BlockSpec((hr, tm), lambda b, i: (0, i)),
            pl.BlockSpec((hr, tm), lambda b, i: (0, i)),
            pl.BlockSpec((tm, C_PAD), lambda b, i: (i, 0)),
            pl.BlockSpec((tm, C_PAD), lambda b, i: (i, 0)),
        ],
        out_specs=[
            pl.BlockSpec((1, C_HEADS * C_PAD, tm), lambda b, i: (b, 0, i)),
            pl.BlockSpec((1, tm, C_HEADS * C_PAD), lambda b, i: (b, i, 0)),
            pl.BlockSpec((1, C_HEADS * (C_VDIM + ONES_ROWS), tm), lambda b, i: (b, 0, i)),
        ],
        out_shape=[
            jax.ShapeDtypeStruct((B, C_HEADS * C_PAD, S), BF16),
            jax.ShapeDtypeStruct((B, S, C_HEADS * C_PAD), BF16),
            jax.ShapeDtypeStruct((B, C_HEADS * (C_VDIM + ONES_ROWS), S), BF16),
        ],
        compiler_params=_params(2),
        name="odd_in",
    )(x, g, win, gq, gkv, wuqT, wkn, wvT, cosq, sinq, cosk, sink)


def _memkv_kernel(mem_ref, g_ref, w_ref, kv_ref):
    mn = _rms_rows(mem_ref[0], g_ref[...]).astype(BF16)
    kv_ref[0] = jnp.dot(mn, w_ref[...], preferred_element_type=F32).astype(BF16)


def _memkv(mem, g, w):
    B, M, D = mem.shape
    N = w.shape[1]
    return pl.pallas_call(
        _memkv_kernel,
        grid=(B,),
        in_specs=[pl.BlockSpec((1, M, D), lambda b: (b, 0, 0)), _const_spec((1, D)), _const_spec((D, N))],
        out_specs=pl.BlockSpec((1, M, N), lambda b: (b, 0, 0)),
        out_shape=jax.ShapeDtypeStruct((B, M, N), BF16),
        compiler_params=_params(1),
        name="mem_kv",
    )(mem, g, w)


def _post_mix_kernel(*refs, n_mix):
    x_ref = refs[0]
    o_refs = refs[1:1 + n_mix]
    w_refs = refs[1 + n_mix:1 + 2 * n_mix]
    gc_ref, wq_ref, kv_ref, wo_ref, out_ref = refs[1 + 2 * n_mix:]
    rows = _row_groups(x_ref.shape[1])
    xs = []
    for r in rows:
        x = x_ref[0, r, :]
        for o_ref, w_ref in zip(o_refs, w_refs):
            x = x + jnp.dot(o_ref[0, r, :], w_ref[...], preferred_element_type=F32)
        xs.append(x)
    qs = []
    for x in xs:
        hc = _rms_rows(x, gc_ref[...]).astype(BF16)
        qs.append((jnp.dot(hc, wq_ref[...], preferred_element_type=F32) * (X_DIM ** -0.5)).astype(BF16))
    heads = [[] for _ in rows]
    for h in range(X_HEADS):
        k_h = kv_ref[0, :, h * X_DIM:(h + 1) * X_DIM]
        v_h = kv_ref[0, :, D_MODEL + h * X_DIM:D_MODEL + (h + 1) * X_DIM]
        for g, q in enumerate(qs):
            s = lax.dot_general(q[:, h * X_DIM:(h + 1) * X_DIM], k_h, _NT, preferred_element_type=F32)
            p = jnp.exp(s - jnp.max(s, axis=-1, keepdims=True))
            l = jnp.sum(p, axis=-1, keepdims=True)
            heads[g].append((jnp.dot(p.astype(BF16), v_h, preferred_element_type=F32) / l).astype(BF16))
    for r, x, hs in zip(rows, xs, heads):
        o = jnp.concatenate(hs, axis=-1)
        out_ref[0, r, :] = x + jnp.dot(o, wo_ref[...], preferred_element_type=F32)


def _post_mix(x, mixes, weights, gc, wq, kv, wo):
    B, S, D = x.shape
    tm = min(GROUPED_TOKEN_TILE, S)
    n = len(mixes)
    M = kv.shape[1]
    tok = lambda b, i: (b, i, 0)
    return pl.pallas_call(
        functools.partial(_post_mix_kernel, n_mix=n),
        grid=(B, S // tm),
        in_specs=([pl.BlockSpec((1, tm, D), tok)]
                  + [pl.BlockSpec((1, tm, m.shape[2]), tok) for m in mixes]
                  + [_const_spec(w.shape) for w in weights]
                  + [_const_spec((1, D)), _const_spec((D, D)),
                     pl.BlockSpec((1, M, 2 * D), lambda b, i: (b, 0, 0)),
                     _const_spec((D, D))]),
        out_specs=pl.BlockSpec((1, tm, D), tok),
        out_shape=jax.ShapeDtypeStruct((B, S, D), F32),
        compiler_params=_params(2),
        name="post_mix",
    )(x, *mixes, *weights, gc, wq, kv, wo)


def _ffn_kernel(x_ref, g_ref, wgu_ref, wd_ref, gf_ref, out_ref, *, final_norm):
    groups = _row_groups(x_ref.shape[1])
    accs = [x_ref[0, r, :] for r in groups]
    xns = [_rms_rows(x, g_ref[...]).astype(BF16) for x in accs]
    for c in range(D_FF // FF_CHUNK):
        lo = c * FF_CHUNK
        hidden = []
        for xn in xns:
            gate = jnp.dot(xn, wgu_ref[:, lo:lo + FF_CHUNK], preferred_element_type=F32)
            up = jnp.dot(xn, wgu_ref[:, D_FF + lo:D_FF + lo + FF_CHUNK], preferred_element_type=F32)
            hidden.append((jax.nn.silu(gate) * up).astype(BF16))
        for g, h in enumerate(hidden):
            accs[g] = accs[g] + jnp.dot(h, wd_ref[lo:lo + FF_CHUNK, :], preferred_element_type=F32)
    for r, acc in zip(groups, accs):
        if final_norm:
            acc = _rms_rows(acc, gf_ref[...])
        out_ref[0, r, :] = acc


def _ffn(x, g, wgu, wd, gf, final_norm):
    B, S, D = x.shape
    tm = min(GROUPED_TOKEN_TILE, S)
    tok = lambda b, i: (b, i, 0)
    resident = lambda shape: pl.BlockSpec(shape, lambda *_: (0,) * len(shape),
                                          pipeline_mode=pl.Buffered(1))
    return pl.pallas_call(
        functools.partial(_ffn_kernel, final_norm=final_norm),
        grid=(B, S // tm),
        in_specs=[pl.BlockSpec((1, tm, D), tok), _const_spec((1, D)),
                  resident(wgu.shape), resident(wd.shape), _const_spec((1, D))],
        out_specs=pl.BlockSpec((1, tm, D), tok),
        out_shape=jax.ShapeDtypeStruct((B, S, D), F32),
        compiler_params=_params(2),
        name="ffn",
    )(x, g, wgu, wd, gf)


def _rope_freqs(n_pairs):
    return ROPE_THETA ** (-jnp.arange(n_pairs, dtype=F32) / n_pairs)


def _axial_angles(S):
    rows = S // GRID_W
    r = jnp.repeat(jnp.arange(rows, dtype=F32), GRID_W)
    c = jnp.tile(jnp.arange(GRID_W, dtype=F32), rows)
    f = _rope_freqs(A_DIM // 4)
    return jnp.concatenate([r[:, None] * f, c[:, None] * f], axis=-1)


def _linear_angles(S, dim):
    t = jnp.arange(S, dtype=F32)
    return t[:, None] * _rope_freqs(dim // 2)


def _deinterleave(n):
    return np.concatenate([np.arange(0, n, 2), np.arange(1, n, 2)])


def _prep_even(w_in, gq, gk):
    perm = _deinterleave(A_DIM)
    o = 0
    w_qa = w_in[:, o:o + A_Q].reshape(D_MODEL, A_HEADS, A_DIM)[:, :, perm].reshape(D_MODEL, A_Q)
    o += A_Q
    w_ka = w_in[:, o:o + A_KV].reshape(D_MODEL, A_KV_HEADS, A_DIM)[:, :, perm].reshape(D_MODEL, A_KV)
    o += A_KV
    w_va = w_in[:, o:o + A_KV]
    o += A_KV
    w_qb = w_in[:, o:o + B_QK]
    o += B_QK
    w_kb = w_in[:, o:o + B_QK]
    o += B_QK
    w_vb = w_in[:, o:o + B_V]
    wt = jnp.concatenate([w_qa, w_ka, w_va, w_qb, w_vb], axis=1).T.astype(BF16)
    return wt, w_kb.astype(BF16), gq[perm].reshape(A_DIM, 1), gk[perm].reshape(A_DIM, 1)


def _prep_odd(w_in, w_uq, w_ukv):
    hr = C_ROPE // 2
    perm = _deinterleave(C_ROPE)
    w_kr = w_in[:, C_Q_RANK + C_KV_RANK:][:, perm]
    w_kr_rot = jnp.concatenate([-w_kr[:, hr:], w_kr[:, :hr]], axis=1)
    win = jnp.concatenate(
        [w_in[:, :C_Q_RANK + C_KV_RANK], w_kr, w_kr_rot,
         jnp.zeros((D_MODEL, C_PAD - 2 * C_ROPE), F32)], axis=1).astype(BF16)
    wq = w_uq.reshape(C_Q_RANK, C_HEADS, C_NOPE + C_ROPE)
    wq = jnp.concatenate(
        [wq[:, :, :C_NOPE], wq[:, :, C_NOPE:][:, :, perm],
         jnp.zeros((C_Q_RANK, C_HEADS, C_PAD - C_NOPE - C_ROPE), F32)], axis=2)
    wuqT = wq.reshape(C_Q_RANK, C_HEADS * C_PAD).T.astype(BF16)
    wkv = w_ukv.reshape(C_KV_RANK, C_HEADS, C_NOPE + C_VDIM)
    wkn = jnp.concatenate(
        [wkv[:, :, :C_NOPE], jnp.zeros((C_KV_RANK, C_HEADS, C_PAD - C_NOPE), F32)], axis=2)
    wkn = wkn.reshape(C_KV_RANK, C_HEADS * C_PAD).astype(BF16)
    wvT = wkv[:, :, C_NOPE:].reshape(C_KV_RANK, C_HEADS * C_VDIM).T.astype(BF16)
    return win, wuqT, wkn, wvT


def _trunk(x, mem, p):
    B, S, D = x.shape
    depth = p['norm_mix'].shape[0]
    row = lambda v: v.reshape(1, -1)

    ang_a = _axial_angles(S)
    cos_a, sin_a = jnp.cos(ang_a).T, jnp.sin(ang_a).T
    ang_l = _linear_angles(S, C_ROPE)
    cos_l, sin_l = jnp.cos(ang_l), jnp.sin(ang_l)
    zpad = jnp.zeros((S, C_PAD - 2 * C_ROPE), F32)
    zrope = jnp.zeros((S, C_ROPE), F32)
    cos_k = jnp.concatenate([cos_l, cos_l, zrope, zpad], axis=1)
    sin_k = jnp.concatenate([zrope, sin_l, sin_l, zpad], axis=1)
    slopes = jnp.asarray(2.0 ** (-8.0 * np.arange(1, B_HEADS + 1) / B_HEADS), dtype=F32)

    for layer in range(depth):
        if layer % 2 == 0:
            e = layer // 2
            wt, wkb, gq, gk = _prep_even(p['e_w_in'][e], p['e_q_norm'][e], p['e_k_norm'][e])
            qaT, ka, vaT, qbT, kb, vbT = _even_in(
                x, row(p['norm_mix'][layer]), wt, wkb, gq, gk, cos_a, sin_a)
            oa = _gqa(qaT, ka, vaT)
            lam_init = 0.8 - 0.6 * math.exp(-0.3 * layer)
            ob = _diff(slopes, row(p['e_lam_q1'][e]), row(p['e_lam_k1'][e]),
                       row(p['e_lam_q2'][e]), row(p['e_lam_k2'][e]),
                       p['e_subln'][e].reshape(B_VDIM, 1), qbT, kb, vbT, lam_init)
            w_out = p['e_w_out'][e].astype(BF16)
            mixes, weights = [oa, ob], [w_out[:A_Q], w_out[A_Q:]]
        else:
            o = layer // 2
            win, wuqT, wkn, wvT = _prep_odd(p['o_w_in'][o], p['o_w_uq'][o], p['o_w_ukv'][o])
            qT, k, vT = _odd_in(x, row(p['norm_mix'][layer]), win, row(p['o_q_norm'][o]),
                                row(p['o_kv_norm'][o]), wuqT, wkn, wvT,
                                cos_l.T, sin_l.T, cos_k, sin_k)
            mixes, weights = [_mla(qT, k, vT)], [p['o_w_out'][o].astype(BF16)]
        kv = _memkv(mem, row(p['norm_mem'][layer]), p['w_ckv'][layer].astype(BF16))
        x = _post_mix(x, mixes, weights, row(p['norm_cross'][layer]),
                      p['w_cq'][layer].astype(BF16), kv, p['w_co'][layer].astype(BF16))
        x = _ffn(x, row(p['norm_ffn'][layer]), p['w_gu'][layer].astype(BF16),
                 p['w_down'][layer].astype(BF16), row(p['final_norm']),
                 final_norm=(layer == depth - 1))
    return x


def kernel(x_prompt, x_sample, mem_prompt, mem_sample, norm_mix, e_w_in, e_q_norm, e_k_norm, e_lam_q1, e_lam_k1, e_lam_q2, e_lam_k2, e_subln, e_w_out, o_w_in, o_q_norm, o_kv_norm, o_w_uq, o_w_ukv, o_w_out, norm_cross, norm_mem, w_cq, w_ckv, w_co, norm_ffn, w_gu, w_down, final_norm):
    p = dict(norm_mix=norm_mix, e_w_in=e_w_in, e_q_norm=e_q_norm, e_k_norm=e_k_norm,
             e_lam_q1=e_lam_q1, e_lam_k1=e_lam_k1, e_lam_q2=e_lam_q2, e_lam_k2=e_lam_k2,
             e_subln=e_subln, e_w_out=e_w_out, o_w_in=o_w_in, o_q_norm=o_q_norm,
             o_kv_norm=o_kv_norm, o_w_uq=o_w_uq, o_w_ukv=o_w_ukv, o_w_out=o_w_out,
             norm_cross=norm_cross, norm_mem=norm_mem, w_cq=w_cq, w_ckv=w_ckv, w_co=w_co,
             norm_ffn=norm_ffn, w_gu=w_gu, w_down=w_down, final_norm=final_norm)
    return (_trunk(x_prompt, mem_prompt, p), _trunk(x_sample, mem_sample, p))
```

```python
import functools
import math

import jax
import jax.numpy as jnp
import numpy as np
from jax import lax
from jax.experimental import pallas as pl
from jax.experimental.pallas import tpu as pltpu

F32 = jnp.float32
BF16 = jnp.bfloat16

D_MODEL = 1024
GRID_W = 64
EPS = 1e-6
ROPE_THETA = 10000.0
A_HEADS, A_KV_HEADS, A_DIM = 8, 2, 64
A_GROUP = A_HEADS // A_KV_HEADS
B_HEADS, B_DIM = 4, 64
B_VDIM = 2 * B_DIM
A_Q = A_HEADS * A_DIM
A_KV = A_KV_HEADS * A_DIM
B_QK = B_HEADS * 2 * B_DIM
B_V = B_HEADS * B_VDIM
C_HEADS, C_Q_RANK, C_KV_RANK, C_NOPE, C_ROPE, C_VDIM = 16, 384, 256, 64, 32, 64
C_PAD = 128
X_HEADS = 4
X_DIM = D_MODEL // X_HEADS
D_FF = ((-(-8 * D_MODEL // 3) + 255) // 256) * 256
FF_CHUNK = 256
NEG_BIG = float(np.finfo(np.float32).min)
LOG2E = math.log2(math.e)

GROUPED_TOKEN_TILE = 1024
WIDE_TOKEN_TILE = 512
Q_TILE = 256
KEY_CHUNK = 256
VMEM_LIMIT = 48 * 1024 * 1024

_NT = (((1,), (1,)), ((), ()))


def _params(n_parallel, n_arbitrary=0):
    return pltpu.CompilerParams(
        dimension_semantics=("parallel",) * n_parallel + ("arbitrary",) * n_arbitrary,
        vmem_limit_bytes=VMEM_LIMIT)


def _rms_rows(x, g):
    ms = jnp.mean(x * x, axis=-1, keepdims=True)
    return (x * lax.rsqrt(ms + EPS)) * g


def _rms_cols(x, g):
    ms = jnp.mean(x * x, axis=0, keepdims=True)
    return (x * lax.rsqrt(ms + EPS)) * g


def _const_spec(shape):
    nd = len(shape)
    return pl.BlockSpec(shape, lambda *_: (0,) * nd)


ONES_ROWS = 16


def _store_values(ref, vT, n_heads, dv, cols):
    ext = dv + ONES_ROWS
    ones = jnp.ones((ONES_ROWS, vT.shape[1]), BF16)
    for h in range(n_heads):
        ref[0, h * ext:h * ext + dv, cols] = vT[h * dv:(h + 1) * dv].astype(BF16)
        ref[0, h * ext + dv:(h + 1) * ext, cols] = ones


ROW_GROUPS = 2


def _row_groups(tm):
    n = ROW_GROUPS if tm % (ROW_GROUPS * 128) == 0 else 1
    return [slice(g * tm // n, (g + 1) * tm // n) for g in range(n)]


def _value_rows(h, dv):
    ext = dv + ONES_ROWS
    return slice(h * ext, (h + 1) * ext)


def _even_in_kernel(x_ref, g_ref, wt_ref, wkb_ref, gq_ref, gk_ref, cos_ref, sin_ref,
                    qaT_ref, ka_ref, vaT_ref, qbT_ref, kb_ref, vbT_ref):
    half = A_DIM // 2
    groups = _row_groups(x_ref.shape[1])
    xns = [_rms_rows(x_ref[0, r, :], g_ref[...]).astype(BF16) for r in groups]
    yTs = [lax.dot_general(wt_ref[...], xn, _NT, preferred_element_type=F32) for xn in xns]
    for r, xn in zip(groups, xns):
        kb_ref[0, r, :] = jnp.dot(xn, wkb_ref[...], preferred_element_type=F32).astype(BF16)

    scale_a = A_DIM ** -0.5 * LOG2E
    for r, yT in zip(groups, yTs):
        cos = cos_ref[:, r]
        sin = sin_ref[:, r]

        def norm_rope(xh, g, scale):
            y = _rms_cols(xh, g)
            e, o = y[:half], y[half:]
            return jnp.concatenate([e * cos - o * sin, e * sin + o * cos], axis=0) * scale

        for h in range(A_HEADS):
            qh = norm_rope(yT[h * A_DIM:(h + 1) * A_DIM], gq_ref[...], scale_a)
            qaT_ref[0, h * A_DIM:(h + 1) * A_DIM, r] = qh.astype(BF16)
        kT = jnp.concatenate(
            [norm_rope(yT[A_Q + h * A_DIM:A_Q + (h + 1) * A_DIM], gk_ref[...], 1.0)
             for h in range(A_KV_HEADS)], axis=0)
        ka_ref[0, r, :] = kT.T.astype(BF16)
        r0 = A_Q + A_KV
        _store_values(vaT_ref, yT[r0:r0 + A_KV], A_KV_HEADS, A_DIM, r)
        r0 += A_KV
        qbT_ref[0, :, r] = (yT[r0:r0 + B_QK] * (B_DIM ** -0.5 * LOG2E)).astype(BF16)
        r0 += B_QK
        _store_values(vbT_ref, yT[r0:r0 + B_V], B_HEADS, B_VDIM, r)


def _even_in(x, g, wt, wkb, gq, gk, cosT, sinT):
    B, S, D = x.shape
    tm = min(GROUPED_TOKEN_TILE, S)
    rows = wt.shape[0]
    va_rows = A_KV_HEADS * (A_DIM + ONES_ROWS)
    vb_rows = B_HEADS * (B_VDIM + ONES_ROWS)
    return pl.pallas_call(
        _even_in_kernel,
        grid=(B, S // tm),
        in_specs=[
            pl.BlockSpec((1, tm, D), lambda b, i: (b, i, 0)),
            _const_spec((1, D)),
            _const_spec((rows, D)),
            _const_spec((D, B_QK)),
            _const_spec((A_DIM, 1)),
            _const_spec((A_DIM, 1)),
            pl.BlockSpec((A_DIM // 2, tm), lambda b, i: (0, i)),
            pl.BlockSpec((A_DIM // 2, tm), lambda b, i: (0, i)),
        ],
        out_specs=[
            pl.BlockSpec((1, A_Q, tm), lambda b, i: (b, 0, i)),
            pl.BlockSpec((1, tm, A_KV), lambda b, i: (b, i, 0)),
            pl.BlockSpec((1, va_rows, tm), lambda b, i: (b, 0, i)),
            pl.BlockSpec((1, B_QK, tm), lambda b, i: (b, 0, i)),
            pl.BlockSpec((1, tm, B_QK), lambda b, i: (b, i, 0)),
            pl.BlockSpec((1, vb_rows, tm), lambda b, i: (b, 0, i)),
        ],
        out_shape=[
            jax.ShapeDtypeStruct((B, A_Q, S), BF16),
            jax.ShapeDtypeStruct((B, S, A_KV), BF16),
            jax.ShapeDtypeStruct((B, va_rows, S), BF16),
            jax.ShapeDtypeStruct((B, B_QK, S), BF16),
            jax.ShapeDtypeStruct((B, S, B_QK), BF16),
            jax.ShapeDtypeStruct((B, vb_rows, S), BF16),
        ],
        compiler_params=_params(2),
        name="even_in",
    )(x, g, wt, wkb, gq, gk, cosT, sinT)


def _chain_init(n, dv, tq):
    return tuple((jnp.full((1, tq), NEG_BIG, F32), jnp.zeros((dv + ONES_ROWS, tq), F32))
                 for _ in range(n))


def _chain_out(acc, dv):
    return acc[:dv] / acc[dv:dv + 1]


def _chunk_start(c, tkc):
    return c * tkc if isinstance(c, int) else pl.multiple_of(c * tkc, tkc)


def _key_chunk(S):
    return min(KEY_CHUNK, S // 2)


def _attend(n_chunks, n_chains, dv, tq, first_tile, prep, score, value, scratch, offset=None):
    assert n_chunks % 2 == 0
    max_ref, s_bufs = scratch[0], (scratch[1:1 + n_chains], scratch[1 + n_chains:])

    def produce(ctx, j, nxt, slot):
        s = score(ctx, j, nxt)
        s_bufs[slot][j][...] = s
        top = jnp.max(s, axis=0, keepdims=True)
        r = None if offset is None else offset(ctx, j, nxt)
        return (top, None) if r is None else (top - r, r)

    @pl.when(first_tile)
    def _():
        ctx0 = prep(0, False)
        for j in range(n_chains):
            max_ref[j] = produce(ctx0, j, False, 0)[0]

    maxes = [(max_ref[j], None) for j in range(n_chains)]
    chains = list(_chain_init(n_chains, dv, tq))
    for c in range(n_chunks):
        slot = c % 2
        nxt = c == n_chunks - 1
        ctx = prep(0 if nxt else c + 1, nxt)
        for j in range(n_chains):
            m, acc = chains[j]
            top, r = maxes[j]
            m_new = jnp.maximum(m, top)
            alpha = jnp.exp2(m - m_new)
            shift = m_new if r is None else m_new + r
            p = jnp.exp2(s_bufs[slot][j][...] - shift).astype(BF16)
            acc = alpha * acc + jnp.dot(value(c, j), p, preferred_element_type=F32)
            chains[j] = (m_new, acc)
            maxes[j] = produce(ctx, j, nxt, 1 - slot)
    for j in range(n_chains):
        max_ref[j] = maxes[j][0]
    return [_chain_out(acc, dv) for _, acc in chains]


def _attend_scratch(n_chains, tkc, tq):
    return ([pltpu.VMEM((n_chains, 1, tq), F32)]
            + [pltpu.VMEM((tkc, tq), F32) for _ in range(2 * n_chains)])


def _next_tile(n_tiles):
    return lambda i: jnp.minimum(i + 1, n_tiles - 1)


def _gqa_kernel(qT_ref, qT_next_ref, k_ref, vT_ref, o_ref, *scratch, tkc):
    S = k_ref.shape[1]
    tq = qT_ref.shape[2]

    def padded_queries(ref):
        pads = []
        for h in range(A_HEADS):
            q = ref[0, h * A_DIM:(h + 1) * A_DIM, :]
            zero = jnp.zeros_like(q)
            pads.append(jnp.concatenate([q, zero] if h < A_GROUP else [zero, q], axis=0))
        return pads

    q_pads = {False: padded_queries(qT_ref), True: padded_queries(qT_next_ref)}

    def prep(c, nxt):
        return k_ref[0, pl.ds(_chunk_start(c, tkc), tkc), :]

    def score(k_c, h, nxt):
        return jnp.dot(k_c, q_pads[nxt][h], preferred_element_type=F32)

    def value(c, h):
        g = h // A_GROUP
        return vT_ref[0, _value_rows(g, A_DIM), pl.ds(_chunk_start(c, tkc), tkc)]

    outs = _attend(S // tkc, A_HEADS, A_DIM, tq, pl.program_id(1) == 0,
                   prep, score, value, scratch)
    o_ref[0] = jnp.concatenate(outs, axis=0).T.astype(o_ref.dtype)


def _gqa(qaT, ka, vaT):
    B, _, S = qaT.shape
    tq = min(Q_TILE, S)
    tkc = _key_chunk(S)
    nxt = _next_tile(S // tq)
    return pl.pallas_call(
        functools.partial(_gqa_kernel, tkc=tkc),
        grid=(B, S // tq),
        in_specs=[
            pl.BlockSpec((1, A_Q, tq), lambda b, i: (b, 0, i)),
            pl.BlockSpec((1, A_Q, tq), lambda b, i: (b, 0, nxt(i))),
            pl.BlockSpec((1, S, A_KV), lambda b, i: (b, 0, 0)),
            pl.BlockSpec((1, A_KV_HEADS * (A_DIM + ONES_ROWS), S), lambda b, i: (b, 0, 0)),
        ],
        out_specs=pl.BlockSpec((1, tq, A_Q), lambda b, i: (b, i, 0)),
        out_shape=jax.ShapeDtypeStruct((B, S, A_Q), BF16),
        scratch_shapes=_attend_scratch(A_HEADS, tkc, tq),
        compiler_params=_params(1, 1),
        name="gqa_attn",
    )(qaT, qaT, ka, vaT)


DIFF_HEADS_PER_STEP = 4


ALIBI_PIECES = 3


def _diff_kernel(slopes_ref, lq1_ref, lk1_ref, lq2_ref, lk2_ref, gsub_ref,
                 qT_ref, qT_next_ref, k_ref, kpos_ref, vT_ref, o_ref, *scratch, tkc, lam_init):
    hp = pl.program_id(1)
    i = pl.program_id(2)
    S = k_ref.shape[1]
    tq = qT_ref.shape[2]
    n_chunks = S // tkc
    n_tiles = S // tq
    pair = 2 * B_DIM
    coefs = [slopes_ref[DIFF_HEADS_PER_STEP * hp + hh] * LOG2E for hh in range(DIFF_HEADS_PER_STEP)]

    def padded_queries(ref):
        pads = []
        for hh in range(DIFF_HEADS_PER_STEP):
            q = ref[0, hh * pair:(hh + 1) * pair, :]
            rows = lax.broadcasted_iota(jnp.int32, q.shape, 0)
            zero = jnp.zeros_like(q)
            pads += [jnp.where(rows < B_DIM, q, zero), jnp.where(rows >= B_DIM, q, zero)]
        return pads

    q_pads = {False: padded_queries(qT_ref), True: padded_queries(qT_next_ref)}
    dmat = (lax.broadcasted_iota(jnp.int32, (tkc, tq), 1)
            - lax.broadcasted_iota(jnp.int32, (tkc, tq), 0)).astype(F32)
    q_local = lax.broadcasted_iota(jnp.int32, (1, tq), 1).astype(F32)
    piece_rows = lax.broadcasted_iota(jnp.int32, (pair, tq), 0) < ALIBI_PIECES

    def chunk_of(c, tile):
        home = (tile * tq) // tkc
        return home, (home if c == 0 else lax.rem(home + c, n_chunks))

    def prep(c, nxt):
        tile = jnp.minimum(i + 1, n_tiles - 1) if nxt else i
        home, chunk = chunk_of(c, tile)
        start = pl.multiple_of(chunk * tkc, tkc)
        q_start = (tile * tq).astype(F32)
        if c == 0:
            dist = jnp.abs(dmat + (q_start - start.astype(F32)))
            return dict(start=start, biases=[dist * -cf for cf in coefs])
        sign = jnp.where(chunk < home, 1.0, -1.0)
        ext = jnp.where(piece_rows, sign, 0.0).astype(BF16)
        q_pos = q_start + q_local
        return dict(start=start, ext=ext, offsets=[(sign * cf) * q_pos for cf in coefs])

    def score(ctx, n, nxt):
        hh = n // 2
        k_c = k_ref[0, pl.ds(ctx["start"], tkc), hh * pair:(hh + 1) * pair]
        q_pad = q_pads[nxt][n]
        if "biases" in ctx:
            return jnp.dot(k_c, q_pad, preferred_element_type=F32) + ctx["biases"][hh]
        keys = jnp.concatenate([k_c, kpos_ref[hh, pl.ds(ctx["start"], tkc), :]], axis=1)
        queries = jnp.concatenate([q_pad, ctx["ext"]], axis=0)
        return jnp.dot(keys, queries, preferred_element_type=F32)

    def offset(ctx, n, nxt):
        return None if "biases" in ctx else ctx["offsets"][n // 2]

    def value(c, n):
        _, chunk = chunk_of(c, i)
        start = pl.multiple_of(chunk * tkc, tkc)
        return vT_ref[0, _value_rows(n // 2, B_VDIM), pl.ds(start, tkc)]

    n_chains = 2 * DIFF_HEADS_PER_STEP
    parts = _attend(n_chunks, n_chains, B_VDIM, tq, i == 0, prep, score, value, scratch, offset)

    lam = (jnp.exp(jnp.sum(lq1_ref[...] * lk1_ref[...], axis=-1, keepdims=True))
           - jnp.exp(jnp.sum(lq2_ref[...] * lk2_ref[...], axis=-1, keepdims=True)) + lam_init)
    outs = []
    for hh in range(DIFF_HEADS_PER_STEP):
        o = parts[2 * hh] - lam * parts[2 * hh + 1]
        outs.append(_rms_cols(o, gsub_ref[...]) * (1.0 - lam_init))
    o_ref[0] = jnp.concatenate(outs, axis=0).T.astype(o_ref.dtype)


def _alibi_key_table(slopes, S):
    a = (slopes * LOG2E)[:, None] * jnp.arange(S, dtype=F32)[None, :]
    pieces, rest = [], a
    for _ in range(ALIBI_PIECES):
        piece = lax.bitcast_convert_type(
            lax.bitcast_convert_type(rest, jnp.uint32) & jnp.uint32(0xFFFF0000), F32)
        pieces.append(piece.astype(BF16))
        rest = rest - piece
    table = jnp.stack(pieces, axis=-1)
    return jnp.pad(table, ((0, 0), (0, 0), (0, 2 * B_DIM - ALIBI_PIECES)))


def _diff(slopes, lq1, lk1, lq2, lk2, gsub, qbT, kb, vbT, lam_init):
    B, _, S = qbT.shape
    tq = min(Q_TILE, S)
    tkc = _key_chunk(S)
    assert tkc % tq == 0
    hs = DIFF_HEADS_PER_STEP
    nxt = _next_tile(S // tq)
    kpos = _alibi_key_table(slopes, S)
    return pl.pallas_call(
        functools.partial(_diff_kernel, tkc=tkc, lam_init=lam_init),
        grid=(B, B_HEADS // hs, S // tq),
        in_specs=[
            pl.BlockSpec(memory_space=pltpu.SMEM),
            _const_spec((1, B_DIM)), _const_spec((1, B_DIM)),
            _const_spec((1, B_DIM)), _const_spec((1, B_DIM)),
            _const_spec((B_VDIM, 1)),
            pl.BlockSpec((1, hs * 2 * B_DIM, tq), lambda b, h, i: (b, h, i)),
            pl.BlockSpec((1, hs * 2 * B_DIM, tq), lambda b, h, i: (b, h, nxt(i))),
            pl.BlockSpec((1, S, hs * 2 * B_DIM), lambda b, h, i: (b, 0, h)),
            pl.BlockSpec((hs, S, 2 * B_DIM), lambda b, h, i: (h, 0, 0)),
            pl.BlockSpec((1, hs * (B_VDIM + ONES_ROWS), S), lambda b, h, i: (b, h, 0)),
        ],
        out_specs=pl.BlockSpec((1, tq, hs * B_VDIM), lambda b, h, i: (b, i, h)),
        out_shape=jax.ShapeDtypeStruct((B, S, B_V), BF16),
        scratch_shapes=_attend_scratch(2 * hs, tkc, tq),
        compiler_params=_params(2, 1),
        name="diff_attn",
    )(slopes, lq1, lk1, lq2, lk2, gsub, qbT, qbT, kb, kpos, vbT)


MLA_HEADS_PER_STEP = 4
MLA_SUBTILES = 2


def _mla_kernel(qT_ref, qT_next_ref, k_ref, vT_ref, o_ref, *scratch, tkc, n_sub):
    S = k_ref.shape[1]
    tqs = qT_ref.shape[2] // n_sub
    nh = MLA_HEADS_PER_STEP
    q_refs = {False: qT_ref, True: qT_next_ref}

    def prep(c, nxt):
        return _chunk_start(c, tkc)

    def score(start, n, nxt):
        j, t = divmod(n, n_sub)
        k_c = k_ref[0, pl.ds(start, tkc), j * C_PAD:(j + 1) * C_PAD]
        q_pad = q_refs[nxt][0, j * C_PAD:(j + 1) * C_PAD, t * tqs:(t + 1) * tqs]
        return jnp.dot(k_c, q_pad, preferred_element_type=F32)

    def value(c, n):
        return vT_ref[0, _value_rows(n // n_sub, C_VDIM), pl.ds(_chunk_start(c, tkc), tkc)]

    outs = _attend(S // tkc, nh * n_sub, C_VDIM, tqs, pl.program_id(2) == 0,
                   prep, score, value, scratch)
    for t in range(n_sub):
        sub = jnp.concatenate([outs[j * n_sub + t] for j in range(nh)], axis=0)
        o_ref[0, t * tqs:(t + 1) * tqs, :] = sub.T.astype(o_ref.dtype)


def _mla(qT, k, vT):
    B, _, S = qT.shape
    n_sub = MLA_SUBTILES if S >= MLA_SUBTILES * Q_TILE else 1
    tq = min(Q_TILE, S) * n_sub
    tkc = _key_chunk(S)
    hp = MLA_HEADS_PER_STEP
    nxt = _next_tile(S // tq)
    return pl.pallas_call(
        functools.partial(_mla_kernel, tkc=tkc, n_sub=n_sub),
        grid=(B, C_HEADS // hp, S // tq),
        in_specs=[
            pl.BlockSpec((1, hp * C_PAD, tq), lambda b, h, i: (b, h, i)),
            pl.BlockSpec((1, hp * C_PAD, tq), lambda b, h, i: (b, h, nxt(i))),
            pl.BlockSpec((1, S, hp * C_PAD), lambda b, h, i: (b, 0, h)),
            pl.BlockSpec((1, hp * (C_VDIM + ONES_ROWS), S), lambda b, h, i: (b, h, 0)),
        ],
        out_specs=pl.BlockSpec((1, tq, hp * C_VDIM), lambda b, h, i: (b, i, h)),
        out_shape=jax.ShapeDtypeStruct((B, S, C_HEADS * C_VDIM), BF16),
        scratch_shapes=_attend_scratch(hp * n_sub, tkc, tq // n_sub),
        compiler_params=_params(2, 1),
        name="mla_attn",
    )(qT, qT, k, vT)


def _odd_in_kernel(x_ref, g_ref, win_ref, gq_ref, gkv_ref, wuqT_ref, wkn_ref, wvT_ref,
                   cosq_ref, sinq_ref, cosk_ref, sink_ref, qT_ref, k_ref, vT_ref):
    hr = C_ROPE // 2
    groups = _row_groups(x_ref.shape[1])
    xns = [_rms_rows(x_ref[0, r, :], g_ref[...]).astype(BF16) for r in groups]
    lat = [jnp.dot(xn, win_ref[...], preferred_element_type=F32) for xn in xns]
    cqns = [_rms_rows(a[:, :C_Q_RANK], gq_ref[...]).astype(BF16) for a in lat]
    ckvns = [_rms_rows(a[:, C_Q_RANK:C_Q_RANK + C_KV_RANK], gkv_ref[...]).astype(BF16) for a in lat]

    qTs = [lax.dot_general(wuqT_ref[...], cqn, _NT, preferred_element_type=F32) for cqn in cqns]
    for rows, qT in zip(groups, qTs):
        qT = qT * ((C_NOPE + C_ROPE) ** -0.5 * LOG2E)
        cq, sq = cosq_ref[:, rows], sinq_ref[:, rows]
        for h in range(C_HEADS):
            r = h * C_PAD
            e = qT[r + C_NOPE:r + C_NOPE + hr]
            o = qT[r + C_NOPE + hr:r + C_NOPE + C_ROPE]
            head = jnp.concatenate(
                [qT[r:r + C_NOPE], e * cq - o * sq, e * sq + o * cq, qT[r + C_NOPE + C_ROPE:r + C_PAD]],
                axis=0)
            qT_ref[0, r:r + C_PAD, rows] = head.astype(BF16)

    kns = [jnp.dot(ckvn, wkn_ref[...], preferred_element_type=F32) for ckvn in ckvns]
    for rows, a, kn in zip(groups, lat, kns):
        kblk = a[:, C_Q_RANK + C_KV_RANK:]
        t = kblk * cosk_ref[rows, :] + kblk * sink_ref[rows, :]
        lane = lax.broadcasted_iota(jnp.int32, t.shape, 1)
        kr = jnp.where(lane < C_ROPE, t + pltpu.roll(t, C_PAD - C_ROPE, axis=1), 0.0)
        kr = pltpu.roll(kr, C_NOPE, axis=1)
        for h in range(C_HEADS):
            k_ref[0, rows, h * C_PAD:(h + 1) * C_PAD] = (
                kn[:, h * C_PAD:(h + 1) * C_PAD] + kr).astype(BF16)

    for rows, ckvn in zip(groups, ckvns):
        _store_values(vT_ref, lax.dot_general(wvT_ref[...], ckvn, _NT, preferred_element_type=F32),
                      C_HEADS, C_VDIM, rows)


def _odd_in(x, g, win, gq, gkv, wuqT, wkn, wvT, cosq, sinq, cosk, sink):
    B, S, D = x.shape
    tm = min(GROUPED_TOKEN_TILE, S)
    hr = C_ROPE // 2
    return pl.pallas_call(
        _odd_in_kernel,
        grid=(B, S // tm),
        in_specs=[
            pl.BlockSpec((1, tm, D), lambda b, i: (b, i, 0)),
            _const_spec((1, D)),
            _const_spec(win.shape),
            _const_spec((1, C_Q_RANK)),
            _const_spec((1, C_KV_RANK)),
            _const_spec(wuqT.shape),
            _const_spec(wkn.shape),
            _const_spec(wvT.shape),
            pl.BlockSpec((hr, tm), lambda b, i: (0, i)),
            pl.BlockSpec((hr, tm), lambda b, i: (0, i)),
            pl.BlockSpec((tm, C_PAD), lambda b, i: (i, 0)),
            pl.BlockSpec((tm, C_PAD), lambda b, i: (i, 0)),
        ],
        out_specs=[
            pl.BlockSpec((1, C_HEADS * C_PAD, tm), lambda b, i: (b, 0, i)),
            pl.BlockSpec((1, tm, C_HEADS * C_PAD), lambda b, i: (b, i, 0)),
            pl.BlockSpec((1, C_HEADS * (C_VDIM + ONES_ROWS), tm), lambda b, i: (b, 0, i)),
        ],
        out_shape=[
            jax.ShapeDtypeStruct((B, C_HEADS * C_PAD, S), BF16),
            jax.ShapeDtypeStruct((B, S, C_HEADS * C_PAD), BF16),
            jax.ShapeDtypeStruct((B, C_HEADS * (C_VDIM + ONES_ROWS), S), BF16),
        ],
        compiler_params=_params(2),
        name="odd_in",
    )(x, g, win, gq, gkv, wuqT, wkn, wvT, cosq, sinq, cosk, sink)


def _memkv_kernel(mem_ref, g_ref, w_ref, kv_ref):
    mn = _rms_rows(mem_ref[0], g_ref[...]).astype(BF16)
    kv_ref[0] = jnp.dot(mn, w_ref[...], preferred_element_type=F32).astype(BF16)


def _memkv(mem, g, w):
    B, M, D = mem.shape
    N = w.shape[1]
    return pl.pallas_call(
        _memkv_kernel,
        grid=(B,),
        in_specs=[pl.BlockSpec((1, M, D), lambda b: (b, 0, 0)), _const_spec((1, D)), _const_spec((D, N))],
        out_specs=pl.BlockSpec((1, M, N), lambda b: (b, 0, 0)),
        out_shape=jax.ShapeDtypeStruct((B, M, N), BF16),
        compiler_params=_params(1),
        name="mem_kv",
    )(mem, g, w)


def _post_mix_kernel(*refs, n_mix):
    x_ref = refs[0]
    o_refs = refs[1:1 + n_mix]
    w_refs = refs[1 + n_mix:1 + 2 * n_mix]
    gc_ref, wq_ref, kv_ref, wo_ref, out_ref = refs[1 + 2 * n_mix:]
    rows = _row_groups(x_ref.shape[1])
    xs = []
    for r in rows:
        x = x_ref[0, r, :]
        for o_ref, w_ref in zip(o_refs, w_refs):
            x = x + jnp.dot(o_ref[0, r, :], w_ref[...], preferred_element_type=F32)
        xs.append(x)
    qs = []
    for x in xs:
        hc = _rms_rows(x, gc_ref[...]).astype(BF16)
        qs.append((jnp.dot(hc, wq_ref[...], preferred_element_type=F32) * (X_DIM ** -0.5)).astype(BF16))
    heads = [[] for _ in rows]
    for h in range(X_HEADS):
        k_h = kv_ref[0, :, h * X_DIM:(h + 1) * X_DIM]
        v_h = kv_ref[0, :, D_MODEL + h * X_DIM:D_MODEL + (h + 1) * X_DIM]
        for g, q in enumerate(qs):
            s = lax.dot_general(q[:, h * X_DIM:(h + 1) * X_DIM], k_h, _NT, preferred_element_type=F32)
            p = jnp.exp(s - jnp.max(s, axis=-1, keepdims=True))
            l = jnp.sum(p, axis=-1, keepdims=True)
            heads[g].append((jnp.dot(p.astype(BF16), v_h, preferred_element_type=F32) / l).astype(BF16))
    for r, x, hs in zip(rows, xs, heads):
        o = jnp.concatenate(hs, axis=-1)
        out_ref[0, r, :] = x + jnp.dot(o, wo_ref[...], preferred_element_type=F32)


def _post_mix(x, mixes, weights, gc, wq, kv, wo):
    B, S, D = x.shape
    tm = min(GROUPED_TOKEN_TILE, S)
    n = len(mixes)
    M = kv.shape[1]
    tok = lambda b, i: (b, i, 0)
    return pl.pallas_call(
        functools.partial(_post_mix_kernel, n_mix=n),
        grid=(B, S // tm),
        in_specs=([pl.BlockSpec((1, tm, D), tok)]
                  + [pl.BlockSpec((1, tm, m.shape[2]), tok) for m in mixes]
                  + [_const_spec(w.shape) for w in weights]
                  + [_const_spec((1, D)), _const_spec((D, D)),
                     pl.BlockSpec((1, M, 2 * D), lambda b, i: (b, 0, 0)),
                     _const_spec((D, D))]),
        out_specs=pl.BlockSpec((1, tm, D), tok),
        out_shape=jax.ShapeDtypeStruct((B, S, D), F32),
        compiler_params=_params(2),
        name="post_mix",
    )(x, *mixes, *weights, gc, wq, kv, wo)


def _ffn_kernel(x_ref, g_ref, wgu_ref, wd_ref, gf_ref, out_ref, *, final_norm):
    groups = _row_groups(x_ref.shape[1])
    accs = [x_ref[0, r, :] for r in groups]
    xns = [_rms_rows(x, g_ref[...]).astype(BF16) for x in accs]
    for c in range(D_FF // FF_CHUNK):
        lo = c * FF_CHUNK
        hidden = []
        for xn in xns:
            gate = jnp.dot(xn, wgu_ref[:, lo:lo + FF_CHUNK], preferred_element_type=F32)
            up = jnp.dot(xn, wgu_ref[:, D_FF + lo:D_FF + lo + FF_CHUNK], preferred_element_type=F32)
            hidden.append((jax.nn.silu(gate) * up).astype(BF16))
        for g, h in enumerate(hidden):
            accs[g] = accs[g] + jnp.dot(h, wd_ref[lo:lo + FF_CHUNK, :], preferred_element_type=F32)
    for r, acc in zip(groups, accs):
        if final_norm:
            acc = _rms_rows(acc, gf_ref[...])
        out_ref[0, r, :] = acc


def _ffn(x, g, wgu, wd, gf, final_norm):
    B, S, D = x.shape
    tm = min(GROUPED_TOKEN_TILE, S)
    tok = lambda b, i: (b, i, 0)
    resident = lambda shape: pl.BlockSpec(shape, lambda *_: (0,) * len(shape),
                                          pipeline_mode=pl.Buffered(1))
    return pl.pallas_call(
        functools.partial(_ffn_kernel, final_norm=final_norm),
        grid=(B, S // tm),
        in_specs=[pl.BlockSpec((1, tm, D), tok), _const_spec((1, D)),
                  resident(wgu.shape), resident(wd.shape), _const_spec((1, D))],
        out_specs=pl.BlockSpec((1, tm, D), tok),
        out_shape=jax.ShapeDtypeStruct((B, S, D), F32),
        compiler_params=_params(2),
        name="ffn",
    )(x, g, wgu, wd, gf)


def _rope_freqs(n_pairs):
    return ROPE_THETA ** (-jnp.arange(n_pairs, dtype=F32) / n_pairs)


def _axial_angles(S):
    rows = S // GRID_W
    r = jnp.repeat(jnp.arange(rows, dtype=F32), GRID_W)
    c = jnp.tile(jnp.arange(GRID_W, dtype=F32), rows)
    f = _rope_freqs(A_DIM // 4)
    return jnp.concatenate([r[:, None] * f, c[:, None] * f], axis=-1)


def _linear_angles(S, dim):
    t = jnp.arange(S, dtype=F32)
    return t[:, None] * _rope_freqs(dim // 2)


def _deinterleave(n):
    return np.concatenate([np.arange(0, n, 2), np.arange(1, n, 2)])


def _prep_even(w_in, gq, gk):
    perm = _deinterleave(A_DIM)
    o = 0
    w_qa = w_in[:, o:o + A_Q].reshape(D_MODEL, A_HEADS, A_DIM)[:, :, perm].reshape(D_MODEL, A_Q)
    o += A_Q
    w_ka = w_in[:, o:o + A_KV].reshape(D_MODEL, A_KV_HEADS, A_DIM)[:, :, perm].reshape(D_MODEL, A_KV)
    o += A_KV
    w_va = w_in[:, o:o + A_KV]
    o += A_KV
    w_qb = w_in[:, o:o + B_QK]
    o += B_QK
    w_kb = w_in[:, o:o + B_QK]
    o += B_QK
    w_vb = w_in[:, o:o + B_V]
    wt = jnp.concatenate([w_qa, w_ka, w_va, w_qb, w_vb], axis=1).T.astype(BF16)
    return wt, w_kb.astype(BF16), gq[perm].reshape(A_DIM, 1), gk[perm].reshape(A_DIM, 1)


def _prep_odd(w_in, w_uq, w_ukv):
    hr = C_ROPE // 2
    perm = _deinterleave(C_ROPE)
    w_kr = w_in[:, C_Q_RANK + C_KV_RANK:][:, perm]
    w_kr_rot = jnp.concatenate([-w_kr[:, hr:], w_kr[:, :hr]], axis=1)
    win = jnp.concatenate(
        [w_in[:, :C_Q_RANK + C_KV_RANK], w_kr, w_kr_rot,
         jnp.zeros((D_MODEL, C_PAD - 2 * C_ROPE), F32)], axis=1).astype(BF16)
    wq = w_uq.reshape(C_Q_RANK, C_HEADS, C_NOPE + C_ROPE)
    wq = jnp.concatenate(
        [wq[:, :, :C_NOPE], wq[:, :, C_NOPE:][:, :, perm],
         jnp.zeros((C_Q_RANK, C_HEADS, C_PAD - C_NOPE - C_ROPE), F32)], axis=2)
    wuqT = wq.reshape(C_Q_RANK, C_HEADS * C_PAD).T.astype(BF16)
    wkv = w_ukv.reshape(C_KV_RANK, C_HEADS, C_NOPE + C_VDIM)
    wkn = jnp.concatenate(
        [wkv[:, :, :C_NOPE], jnp.zeros((C_KV_RANK, C_HEADS, C_PAD - C_NOPE), F32)], axis=2)
    wkn = wkn.reshape(C_KV_RANK, C_HEADS * C_PAD).astype(BF16)
    wvT = wkv[:, :, C_NOPE:].reshape(C_KV_RANK, C_HEADS * C_VDIM).T.astype(BF16)
    return win, wuqT, wkn, wvT


def _trunk(x, mem, p):
    B, S, D = x.shape
    depth = p['norm_mix'].shape[0]
    row = lambda v: v.reshape(1, -1)

    ang_a = _axial_angles(S)
    cos_a, sin_a = jnp.cos(ang_a).T, jnp.sin(ang_a).T
    ang_l = _linear_angles(S, C_ROPE)
    cos_l, sin_l = jnp.cos(ang_l), jnp.sin(ang_l)
    zpad = jnp.zeros((S, C_PAD - 2 * C_ROPE), F32)
    zrope = jnp.zeros((S, C_ROPE), F32)
    cos_k = jnp.concatenate([cos_l, cos_l, zrope, zpad], axis=1)
    sin_k = jnp.concatenate([zrope, sin_l, sin_l, zpad], axis=1)
    slopes = jnp.asarray(2.0 ** (-8.0 * np.arange(1, B_HEADS + 1) / B_HEADS), dtype=F32)

    for layer in range(depth):
        if layer % 2 == 0:
            e = layer // 2
            wt, wkb, gq, gk = _prep_even(p['e_w_in'][e], p['e_q_norm'][e], p['e_k_norm'][e])
            qaT, ka, vaT, qbT, kb, vbT = _even_in(
                x, row(p['norm_mix'][layer]), wt, wkb, gq, gk, cos_a, sin_a)
            oa = _gqa(qaT, ka, vaT)
            lam_init = 0.8 - 0.6 * math.exp(-0.3 * layer)
            ob = _diff(slopes, row(p['e_lam_q1'][e]), row(p['e_lam_k1'][e]),
                       row(p['e_lam_q2'][e]), row(p['e_lam_k2'][e]),
                       p['e_subln'][e].reshape(B_VDIM, 1), qbT, kb, vbT, lam_init)
            w_out = p['e_w_out'][e].astype(BF16)
            mixes, weights = [oa, ob], [w_out[:A_Q], w_out[A_Q:]]
        else:
            o = layer // 2
            win, wuqT, wkn, wvT = _prep_odd(p['o_w_in'][o], p['o_w_uq'][o], p['o_w_ukv'][o])
            qT, k, vT = _odd_in(x, row(p['norm_mix'][layer]), win, row(p['o_q_norm'][o]),
                                row(p['o_kv_norm'][o]), wuqT, wkn, wvT,
                                cos_l.T, sin_l.T, cos_k, sin_k)
            mixes, weights = [_mla(qT, k, vT)], [p['o_w_out'][o].astype(BF16)]
        kv = _memkv(mem, row(p['norm_mem'][layer]), p['w_ckv'][layer].astype(BF16))
        x = _post_mix(x, mixes, weights, row(p['norm_cross'][layer]),
                      p['w_cq'][layer].astype(BF16), kv, p['w_co'][layer].astype(BF16))
        x = _ffn(x, row(p['norm_ffn'][layer]), p['w_gu'][layer].astype(BF16),
                 p['w_down'][layer].astype(BF16), row(p['final_norm']),
                 final_norm=(layer == depth - 1))
    return x


def kernel(x_prompt, x_sample, mem_prompt, mem_sample, norm_mix, e_w_in, e_q_norm, e_k_norm, e_lam_q1, e_lam_k1, e_lam_q2, e_lam_k2, e_subln, e_w_out, o_w_in, o_q_norm, o_kv_norm, o_w_uq, o_w_ukv, o_w_out, norm_cross, norm_mem, w_cq, w_ckv, w_co, norm_ffn, w_gu, w_down, final_norm):
    p = dict(norm_mix=norm_mix, e_w_in=e_w_in, e_q_norm=e_q_norm, e_k_norm=e_k_norm,
             e_lam_q1=e_lam_q1, e_lam_k1=e_lam_k1, e_lam_q2=e_lam_q2, e_lam_k2=e_lam_k2,
             e_subln=e_subln, e_w_out=e_w_out, o_w_in=o_w_in, o_q_norm=o_q_norm,
             o_kv_norm=o_kv_norm, o_w_uq=o_w_uq, o_w_ukv=o_w_ukv, o_w_out=o_w_out,
             norm_cross=norm_cross, norm_mem=norm_mem, w_cq=w_cq, w_ckv=w_ckv, w_co=w_co,
             norm_ffn=norm_ffn, w_gu=w_gu, w_down=w_down, final_norm=final_norm)
    return (_trunk(x_prompt, mem_prompt, p), _trunk(x_sample, mem_sample, p))
```

```python
import functools
import math

import jax
import jax.numpy as jnp
import numpy as np
from jax import lax
from jax.experimental import pallas as pl
from jax.experimental.pallas import tpu as pltpu

F32 = jnp.float32
BF16 = jnp.bfloat16

D_MODEL = 1024
GRID_W = 64
EPS = 1e-6
ROPE_THETA = 10000.0
A_HEADS, A_KV_HEADS, A_DIM = 8, 2, 64
A_GROUP = A_HEADS // A_KV_HEADS
B_HEADS, B_DIM = 4, 64
B_VDIM = 2 * B_DIM
A_Q = A_HEADS * A_DIM
A_KV = A_KV_HEADS * A_DIM
B_QK = B_HEADS * 2 * B_DIM
B_V = B_HEADS * B_VDIM
C_HEADS, C_Q_RANK, C_KV_RANK, C_NOPE, C_ROPE, C_VDIM = 16, 384, 256, 64, 32, 64
C_PAD = 128
X_HEADS = 4
X_DIM = D_MODEL // X_HEADS
D_FF = ((-(-8 * D_MODEL // 3) + 255) // 256) * 256
FF_CHUNK = 256
NEG_BIG = float(np.finfo(np.float32).min)
LOG2E = math.log2(math.e)

GROUPED_TOKEN_TILE = 1024
WIDE_TOKEN_TILE = 512
Q_TILE = 256
KEY_CHUNK = 256
VMEM_LIMIT = 48 * 1024 * 1024

_NT = (((1,), (1,)), ((), ()))


def _params(n_parallel, n_arbitrary=0):
    return pltpu.CompilerParams(
        dimension_semantics=("parallel",) * n_parallel + ("arbitrary",) * n_arbitrary,
        vmem_limit_bytes=VMEM_LIMIT)


def _rms_rows(x, g):
    ms = jnp.mean(x * x, axis=-1, keepdims=True)
    return (x * lax.rsqrt(ms + EPS)) * g


def _rms_cols(x, g):
    ms = jnp.mean(x * x, axis=0, keepdims=True)
    return (x * lax.rsqrt(ms + EPS)) * g


def _const_spec(shape):
    nd = len(shape)
    return pl.BlockSpec(shape, lambda *_: (0,) * nd)


ONES_ROWS = 16


def _store_values(ref, vT, n_heads, dv, cols):
    ext = dv + ONES_ROWS
    ones = jnp.ones((ONES_ROWS, vT.shape[1]), BF16)
    for h in range(n_heads):
        ref[0, h * ext:h * ext + dv, cols] = vT[h * dv:(h + 1) * dv].astype(BF16)
        ref[0, h * ext + dv:(h + 1) * ext, cols] = ones


ROW_GROUPS = 2


def _row_groups(tm):
    n = ROW_GROUPS if tm % (ROW_GROUPS * 128) == 0 else 1
    return [slice(g * tm // n, (g + 1) * tm // n) for g in range(n)]


def _value_rows(h, dv):
    ext = dv + ONES_ROWS
    return slice(h * ext, (h + 1) * ext)


def _even_in_kernel(x_ref, g_ref, wt_ref, wkb_ref, gq_ref, gk_ref, cos_ref, sin_ref,
                    qaT_ref, ka_ref, vaT_ref, qbT_ref, kb_ref, vbT_ref):
    half = A_DIM // 2
    groups = _row_groups(x_ref.shape[1])
    xns = [_rms_rows(x_ref[0, r, :], g_ref[...]).astype(BF16) for r in groups]
    yTs = [lax.dot_general(wt_ref[...], xn, _NT, preferred_element_type=F32) for xn in xns]
    for r, xn in zip(groups, xns):
        kb_ref[0, r, :] = jnp.dot(xn, wkb_ref[...], preferred_element_type=F32).astype(BF16)

    scale_a = A_DIM ** -0.5 * LOG2E
    for r, yT in zip(groups, yTs):
        cos = cos_ref[:, r]
        sin = sin_ref[:, r]

        def norm_rope(xh, g, scale):
            y = _rms_cols(xh, g)
            e, o = y[:half], y[half:]
            return jnp.concatenate([e * cos - o * sin, e * sin + o * cos], axis=0) * scale

        for h in range(A_HEADS):
            qh = norm_rope(yT[h * A_DIM:(h + 1) * A_DIM], gq_ref[...], scale_a)
            qaT_ref[0, h * A_DIM:(h + 1) * A_DIM, r] = qh.astype(BF16)
        kT = jnp.concatenate(
            [norm_rope(yT[A_Q + h * A_DIM:A_Q + (h + 1) * A_DIM], gk_ref[...], 1.0)
             for h in range(A_KV_HEADS)], axis=0)
        ka_ref[0, r, :] = kT.T.astype(BF16)
        r0 = A_Q + A_KV
        _store_values(vaT_ref, yT[r0:r0 + A_KV], A_KV_HEADS, A_DIM, r)
        r0 += A_KV
        qbT_ref[0, :, r] = (yT[r0:r0 + B_QK] * (B_DIM ** -0.5 * LOG2E)).astype(BF16)
        r0 += B_QK
        _store_values(vbT_ref, yT[r0:r0 + B_V], B_HEADS, B_VDIM, r)


def _even_in(x, g, wt, wkb, gq, gk, cosT, sinT):
    B, S, D = x.shape
    tm = min(GROUPED_TOKEN_TILE, S)
    rows = wt.shape[0]
    va_rows = A_KV_HEADS * (A_DIM + ONES_ROWS)
    vb_rows = B_HEADS * (B_VDIM + ONES_ROWS)
    return pl.pallas_call(
        _even_in_kernel,
        grid=(B, S // tm),
        in_specs=[
            pl.BlockSpec((1, tm, D), lambda b, i: (b, i, 0)),
            _const_spec((1, D)),
            _const_spec((rows, D)),
            _const_spec((D, B_QK)),
            _const_spec((A_DIM, 1)),
            _const_spec((A_DIM, 1)),
            pl.BlockSpec((A_DIM // 2, tm), lambda b, i: (0, i)),
            pl.BlockSpec((A_DIM // 2, tm), lambda b, i: (0, i)),
        ],
        out_specs=[
            pl.BlockSpec((1, A_Q, tm), lambda b, i: (b, 0, i)),
            pl.BlockSpec((1, tm, A_KV), lambda b, i: (b, i, 0)),
            pl.BlockSpec((1, va_rows, tm), lambda b, i: (b, 0, i)),
            pl.BlockSpec((1, B_QK, tm), lambda b, i: (b, 0, i)),
            pl.BlockSpec((1, tm, B_QK), lambda b, i: (b, i, 0)),
            pl.BlockSpec((1, vb_rows, tm), lambda b, i: (b, 0, i)),
        ],
        out_shape=[
            jax.ShapeDtypeStruct((B, A_Q, S), BF16),
            jax.ShapeDtypeStruct((B, S, A_KV), BF16),
            jax.ShapeDtypeStruct((B, va_rows, S), BF16),
            jax.ShapeDtypeStruct((B, B_QK, S), BF16),
            jax.ShapeDtypeStruct((B, S, B_QK), BF16),
            jax.ShapeDtypeStruct((B, vb_rows, S), BF16),
        ],
        compiler_params=_params(2),
        name="even_in",
    )(x, g, wt, wkb, gq, gk, cosT, sinT)


def _chain_init(n, dv, tq):
    return tuple((jnp.full((1, tq), NEG_BIG, F32), jnp.zeros((dv + ONES_ROWS, tq), F32))
                 for _ in range(n))


def _chain_out(acc, dv):
    return acc[:dv] / acc[dv:dv + 1]


def _chunk_start(c, tkc):
    return c * tkc if isinstance(c, int) else pl.multiple_of(c * tkc, tkc)


def _key_chunk(S):
    return min(KEY_CHUNK, S // 2)


def _attend(n_chunks, n_chains, dv, tq, first_tile, prep, score, value, scratch, offset=None):
    assert n_chunks % 2 == 0
    max_ref, s_bufs = scratch[0], (scratch[1:1 + n_chains], scratch[1 + n_chains:])

    def produce(ctx, j, nxt, slot):
        s = score(ctx, j, nxt)
        s_bufs[slot][j][...] = s
        top = jnp.max(s, axis=0, keepdims=True)
        r = None if offset is None else offset(ctx, j, nxt)
        return (top, None) if r is None else (top - r, r)

    @pl.when(first_tile)
    def _():
        ctx0 = prep(0, False)
        for j in range(n_chains):
            max_ref[j] = produce(ctx0, j, False, 0)[0]

    maxes = [(max_ref[j], None) for j in range(n_chains)]
    chains = list(_chain_init(n_chains, dv, tq))
    for c in range(n_chunks):
        slot = c % 2
        nxt = c == n_chunks - 1
        ctx = prep(0 if nxt else c + 1, nxt)
        for j in range(n_chains):
            m, acc = chains[j]
            top, r = maxes[j]
            m_new = jnp.maximum(m, top)
            alpha = jnp.exp2(m - m_new)
            shift = m_new if r is None else m_new + r
            p = jnp.exp2(s_bufs[slot][j][...] - shift).astype(BF16)
            acc = alpha * acc + jnp.dot(value(c, j), p, preferred_element_type=F32)
            chains[j] = (m_new, acc)
            maxes[j] = produce(ctx, j, nxt, 1 - slot)
    for j in range(n_chains):
        max_ref[j] = maxes[j][0]
    return [_chain_out(acc, dv) for _, acc in chains]


def _attend_scratch(n_chains, tkc, tq):
    return ([pltpu.VMEM((n_chains, 1, tq), F32)]
            + [pltpu.VMEM((tkc, tq), F32) for _ in range(2 * n_chains)])


def _next_tile(n_tiles):
    return lambda i: jnp.minimum(i + 1, n_tiles - 1)


def _gqa_kernel(qT_ref, qT_next_ref, k_ref, vT_ref, o_ref, *scratch, tkc):
    S = k_ref.shape[1]
    tq = qT_ref.shape[2]

    q_refs = {False: qT_ref, True: qT_next_ref}

    def prep(c, nxt):
        return k_ref[0, pl.ds(_chunk_start(c, tkc), tkc), :]

    def score(k_c, h, nxt):
        q = q_refs[nxt][0, h * A_DIM:(h + 1) * A_DIM, :]
        zero = jnp.zeros_like(q)
        q_pad = jnp.concatenate([q, zero] if h < A_GROUP else [zero, q], axis=0)
        return jnp.dot(k_c, q_pad, preferred_element_type=F32)

    def value(c, h):
        g = h // A_GROUP
        return vT_ref[0, _value_rows(g, A_DIM), pl.ds(_chunk_start(c, tkc), tkc)]

    outs = _attend(S // tkc, A_HEADS, A_DIM, tq, pl.program_id(1) == 0,
                   prep, score, value, scratch)
    o_ref[0] = jnp.concatenate(outs, axis=0).T.astype(o_ref.dtype)


def _gqa(qaT, ka, vaT):
    B, _, S = qaT.shape
    tq = min(Q_TILE, S)
    tkc = _key_chunk(S)
    nxt = _next_tile(S // tq)
    return pl.pallas_call(
        functools.partial(_gqa_kernel, tkc=tkc),
        grid=(B, S // tq),
        in_specs=[
            pl.BlockSpec((1, A_Q, tq), lambda b, i: (b, 0, i)),
            pl.BlockSpec((1, A_Q, tq), lambda b, i: (b, 0, nxt(i))),
            pl.BlockSpec((1, S, A_KV), lambda b, i: (b, 0, 0)),
            pl.BlockSpec((1, A_KV_HEADS * (A_DIM + ONES_ROWS), S), lambda b, i: (b, 0, 0)),
        ],
        out_specs=pl.BlockSpec((1, tq, A_Q), lambda b, i: (b, i, 0)),
        out_shape=jax.ShapeDtypeStruct((B, S, A_Q), BF16),
        scratch_shapes=_attend_scratch(A_HEADS, tkc, tq),
        compiler_params=_params(1, 1),
        name="gqa_attn",
    )(qaT, qaT, ka, vaT)


DIFF_HEADS_PER_STEP = 4


ALIBI_PIECES = 3


def _diff_kernel(slopes_ref, lq1_ref, lk1_ref, lq2_ref, lk2_ref, gsub_ref,
                 qT_ref, qT_next_ref, k_ref, kpos_ref, vT_ref, o_ref, *scratch, tkc, lam_init):
    hp = pl.program_id(1)
    i = pl.program_id(2)
    S = k_ref.shape[1]
    tq = qT_ref.shape[2]
    n_chunks = S // tkc
    n_tiles = S // tq
    pair = 2 * B_DIM
    coefs = [slopes_ref[DIFF_HEADS_PER_STEP * hp + hh] * LOG2E for hh in range(DIFF_HEADS_PER_STEP)]

    q_refs = {False: qT_ref, True: qT_next_ref}

    def padded_query(n, nxt):
        hh, comp = divmod(n, 2)
        q = q_refs[nxt][0, hh * pair + comp * B_DIM:hh * pair + (comp + 1) * B_DIM, :]
        zero = jnp.zeros_like(q)
        return jnp.concatenate([q, zero] if comp == 0 else [zero, q], axis=0)

    dmat = (lax.broadcasted_iota(jnp.int32, (tkc, tq), 1)
            - lax.broadcasted_iota(jnp.int32, (tkc, tq), 0)).astype(F32)
    q_local = lax.broadcasted_iota(jnp.int32, (1, tq), 1).astype(F32)
    piece_rows = lax.broadcasted_iota(jnp.int32, (pair, tq), 0) < ALIBI_PIECES

    def chunk_of(c, tile):
        home = (tile * tq) // tkc
        return home, (home if c == 0 else lax.rem(home + c, n_chunks))

    def prep(c, nxt):
        tile = jnp.minimum(i + 1, n_tiles - 1) if nxt else i
        home, chunk = chunk_of(c, tile)
        start = pl.multiple_of(chunk * tkc, tkc)
        q_start = (tile * tq).astype(F32)
        if c == 0:
            dist = jnp.abs(dmat + (q_start - start.astype(F32)))
            return dict(start=start, biases=[dist * -cf for cf in coefs])
        sign = jnp.where(chunk < home, 1.0, -1.0)
        ext = jnp.where(piece_rows, sign, 0.0).astype(BF16)
        q_pos = q_start + q_local
        return dict(start=start, ext=ext, offsets=[(sign * cf) * q_pos for cf in coefs])

    def score(ctx, n, nxt):
        hh = n // 2
        k_c = k_ref[0, pl.ds(ctx["start"], tkc), hh * pair:(hh + 1) * pair]
        q_pad = padded_query(n, nxt)
        if "biases" in ctx:
            return jnp.dot(k_c, q_pad, preferred_element_type=F32) + ctx["biases"][hh]
        keys = jnp.concatenate([k_c, kpos_ref[hh, pl.ds(ctx["start"], tkc), :]], axis=1)
        queries = jnp.concatenate([q_pad, ctx["ext"]], axis=0)
        return jnp.dot(keys, queries, preferred_element_type=F32)

    def offset(ctx, n, nxt):
        return None if "biases" in ctx else ctx["offsets"][n // 2]

    def value(c, n):
        _, chunk = chunk_of(c, i)
        start = pl.multiple_of(chunk * tkc, tkc)
        return vT_ref[0, _value_rows(n // 2, B_VDIM), pl.ds(start, tkc)]

    n_chains = 2 * DIFF_HEADS_PER_STEP
    parts = _attend(n_chunks, n_chains, B_VDIM, tq, i == 0, prep, score, value, scratch, offset)

    lam = (jnp.exp(jnp.sum(lq1_ref[...] * lk1_ref[...], axis=-1, keepdims=True))
           - jnp.exp(jnp.sum(lq2_ref[...] * lk2_ref[...], axis=-1, keepdims=True)) + lam_init)
    outs = []
    for hh in range(DIFF_HEADS_PER_STEP):
        o = parts[2 * hh] - lam * parts[2 * hh + 1]
        outs.append(_rms_cols(o, gsub_ref[...]) * (1.0 - lam_init))
    o_ref[0] = jnp.concatenate(outs, axis=0).T.astype(o_ref.dtype)


def _alibi_key_table(slopes, S):
    a = (slopes * LOG2E)[:, None] * jnp.arange(S, dtype=F32)[None, :]
    pieces, rest = [], a
    for _ in range(ALIBI_PIECES):
        piece = lax.bitcast_convert_type(
            lax.bitcast_convert_type(rest, jnp.uint32) & jnp.uint32(0xFFFF0000), F32)
        pieces.append(piece.astype(BF16))
        rest = rest - piece
    table = jnp.stack(pieces, axis=-1)
    return jnp.pad(table, ((0, 0), (0, 0), (0, 2 * B_DIM - ALIBI_PIECES)))


def _diff(slopes, lq1, lk1, lq2, lk2, gsub, qbT, kb, vbT, lam_init):
    B, _, S = qbT.shape
    tq = min(Q_TILE, S)
    tkc = _key_chunk(S)
    assert tkc % tq == 0
    hs = DIFF_HEADS_PER_STEP
    nxt = _next_tile(S // tq)
    kpos = _alibi_key_table(slopes, S)
    return pl.pallas_call(
        functools.partial(_diff_kernel, tkc=tkc, lam_init=lam_init),
        grid=(B, B_HEADS // hs, S // tq),
        in_specs=[
            pl.BlockSpec(memory_space=pltpu.SMEM),
            _const_spec((1, B_DIM)), _const_spec((1, B_DIM)),
            _const_spec((1, B_DIM)), _const_spec((1, B_DIM)),
            _const_spec((B_VDIM, 1)),
            pl.BlockSpec((1, hs * 2 * B_DIM, tq), lambda b, h, i: (b, h, i)),
            pl.BlockSpec((1, hs * 2 * B_DIM, tq), lambda b, h, i: (b, h, nxt(i))),
            pl.BlockSpec((1, S, hs * 2 * B_DIM), lambda b, h, i: (b, 0, h)),
            pl.BlockSpec((hs, S, 2 * B_DIM), lambda b, h, i: (h, 0, 0)),
            pl.BlockSpec((1, hs * (B_VDIM + ONES_ROWS), S), lambda b, h, i: (b, h, 0)),
        ],
        out_specs=pl.BlockSpec((1, tq, hs * B_VDIM), lambda b, h, i: (b, i, h)),
        out_shape=jax.ShapeDtypeStruct((B, S, B_V), BF16),
        scratch_shapes=_attend_scratch(2 * hs, tkc, tq),
        compiler_params=_params(2, 1),
        name="diff_attn",
    )(slopes, lq1, lk1, lq2, lk2, gsub, qbT, qbT, kb, kpos, vbT)


MLA_HEADS_PER_STEP = 4
MLA_SUBTILES = 2


def _mla_kernel(qT_ref, qT_next_ref, k_ref, vT_ref, o_ref, *scratch, tkc, n_sub):
    S = k_ref.shape[1]
    tqs = qT_ref.shape[2] // n_sub
    nh = MLA_HEADS_PER_STEP
    q_refs = {False: qT_ref, True: qT_next_ref}

    def prep(c, nxt):
        return _chunk_start(c, tkc)

    def score(start, n, nxt):
        j, t = divmod(n, n_sub)
        k_c = k_ref[0, pl.ds(start, tkc), j * C_PAD:(j + 1) * C_PAD]
        q_pad = q_refs[nxt][0, j * C_PAD:(j + 1) * C_PAD, t * tqs:(t + 1) * tqs]
        return jnp.dot(k_c, q_pad, preferred_element_type=F32)

    def value(c, n):
        return vT_ref[0, _value_rows(n // n_sub, C_VDIM), pl.ds(_chunk_start(c, tkc), tkc)]

    outs = _attend(S // tkc, nh * n_sub, C_VDIM, tqs, pl.program_id(2) == 0,
                   prep, score, value, scratch)
    for t in range(n_sub):
        sub = jnp.concatenate([outs[j * n_sub + t] for j in range(nh)], axis=0)
        o_ref[0, t * tqs:(t + 1) * tqs, :] = sub.T.astype(o_ref.dtype)


def _mla(qT, k, vT):
    B, _, S = qT.shape
    n_sub = MLA_SUBTILES if S >= MLA_SUBTILES * Q_TILE else 1
    tq = min(Q_TILE, S) * n_sub
    tkc = _key_chunk(S)
    hp = MLA_HEADS_PER_STEP
    nxt = _next_tile(S // tq)
    return pl.pallas_call(
        functools.partial(_mla_kernel, tkc=tkc, n_sub=n_sub),
        grid=(B, C_HEADS // hp, S // tq),
        in_specs=[
            pl.BlockSpec((1, hp * C_PAD, tq), lambda b, h, i: (b, h, i)),
            pl.BlockSpec((1, hp * C_PAD, tq), lambda b, h, i: (b, h, nxt(i))),
            pl.BlockSpec((1, S, hp * C_PAD), lambda b, h, i: (b, 0, h)),
            pl.BlockSpec((1, hp * (C_VDIM + ONES_ROWS), S), lambda b, h, i: (b, h, 0)),
        ],
        out_specs=pl.BlockSpec((1, tq, hp * C_VDIM), lambda b, h, i: (b, i, h)),
        out_shape=jax.ShapeDtypeStruct((B, S, C_HEADS * C_VDIM), BF16),
        scratch_shapes=_attend_scratch(hp * n_sub, tkc, tq // n_sub),
        compiler_params=_params(2, 1),
        name="mla_attn",
    )(qT, qT, k, vT)


def _odd_in_kernel(x_ref, g_ref, win_ref, gq_ref, gkv_ref, wuqT_ref, wkn_ref, wvT_ref,
                   cosq_ref, sinq_ref, cosk_ref, sink_ref, qT_ref, k_ref, vT_ref):
    hr = C_ROPE // 2
    groups = _row_groups(x_ref.shape[1])
    xns = [_rms_rows(x_ref[0, r, :], g_ref[...]).astype(BF16) for r in groups]
    lat = [jnp.dot(xn, win_ref[...], preferred_element_type=F32) for xn in xns]
    cqns = [_rms_rows(a[:, :C_Q_RANK], gq_ref[...]).astype(BF16) for a in lat]
    ckvns = [_rms_rows(a[:, C_Q_RANK:C_Q_RANK + C_KV_RANK], gkv_ref[...]).astype(BF16) for a in lat]

    qTs = [lax.dot_general(wuqT_ref[...], cqn, _NT, preferred_element_type=F32) for cqn in cqns]
    for rows, qT in zip(groups, qTs):
        qT = qT * ((C_NOPE + C_ROPE) ** -0.5 * LOG2E)
        cq, sq = cosq_ref[:, rows], sinq_ref[:, rows]
        for h in range(C_HEADS):
            r = h * C_PAD
            e = qT[r + C_NOPE:r + C_NOPE + hr]
            o = qT[r + C_NOPE + hr:r + C_NOPE + C_ROPE]
            head = jnp.concatenate(
                [qT[r:r + C_NOPE], e * cq - o * sq, e * sq + o * cq, qT[r + C_NOPE + C_ROPE:r + C_PAD]],
                axis=0)
            qT_ref[0, r:r + C_PAD, rows] = head.astype(BF16)

    kns = [jnp.dot(ckvn, wkn_ref[...], preferred_element_type=F32) for ckvn in ckvns]
    for rows, a, kn in zip(groups, lat, kns):
        kblk = a[:, C_Q_RANK + C_KV_RANK:]
        t = kblk * cosk_ref[rows, :] + kblk * sink_ref[rows, :]
        lane = lax.broadcasted_iota(jnp.int32, t.shape, 1)
        kr = jnp.where(lane < C_ROPE, t + pltpu.roll(t, C_PAD - C_ROPE, axis=1), 0.0)
        kr = pltpu.roll(kr, C_NOPE, axis=1)
        for h in range(C_HEADS):
            k_ref[0, rows, h * C_PAD:(h + 1) * C_PAD] = (
                kn[:, h * C_PAD:(h + 1) * C_PAD] + kr).astype(BF16)

    for rows, ckvn in zip(groups, ckvns):
        _store_values(vT_ref, lax.dot_general(wvT_ref[...], ckvn, _NT, preferred_element_type=F32),
                      C_HEADS, C_VDIM, rows)


def _odd_in(x, g, win, gq, gkv, wuqT, wkn, wvT, cosq, sinq, cosk, sink):
    B, S, D = x.shape
    tm = min(GROUPED_TOKEN_TILE, S)
    hr = C_ROPE // 2
    return pl.pallas_call(
        _odd_in_kernel,
        grid=(B, S // tm),
        in_specs=[
            pl.BlockSpec((1, tm, D), lambda b, i: (b, i, 0)),
            _const_spec((1, D)),
            _const_spec(win.shape),
            _const_spec((1, C_Q_RANK)),
            _const_spec((1, C_KV_RANK)),
            _const_spec(wuqT.shape),
            _const_spec(wkn.shape),
            _const_spec(wvT.shape),
            pl.BlockSpec((hr, tm), lambda b, i: (0, i)),
            pl.BlockSpec((hr, tm), lambda b, i: (0, i)),
            pl.BlockSpec((tm, C_PAD), lambda b, i: (i, 0)),
            pl.BlockSpec((tm, C_PAD), lambda b, i: (i, 0)),
        ],
        out_specs=[
            pl.BlockSpec((1, C_HEADS * C_PAD, tm), lambda b, i: (b, 0, i)),
            pl.BlockSpec((1, tm, C_HEADS * C_PAD), lambda b, i: (b, i, 0)),
            pl.BlockSpec((1, C_HEADS * (C_VDIM + ONES_ROWS), tm), lambda b, i: (b, 0, i)),
        ],
        out_shape=[
            jax.ShapeDtypeStruct((B, C_HEADS * C_PAD, S), BF16),
            jax.ShapeDtypeStruct((B, S, C_HEADS * C_PAD), BF16),
            jax.ShapeDtypeStruct((B, C_HEADS * (C_VDIM + ONES_ROWS), S), BF16),
        ],
        compiler_params=_params(2),
        name="odd_in",
    )(x, g, win, gq, gkv, wuqT, wkn, wvT, cosq, sinq, cosk, sink)


def _memkv_kernel(mem_ref, g_ref, w_ref, kv_ref):
    mn = _rms_rows(mem_ref[0], g_ref[...]).astype(BF16)
    kv_ref[0] = jnp.dot(mn, w_ref[...], preferred_element_type=F32).astype(BF16)


def _memkv(mem, g, w):
    B, M, D = mem.shape
    N = w.shape[1]
    return pl.pallas_call(
        _memkv_kernel,
        grid=(B,),
        in_specs=[pl.BlockSpec((1, M, D), lambda b: (b, 0, 0)), _const_spec((1, D)), _const_spec((D, N))],
        out_specs=pl.BlockSpec((1, M, N), lambda b: (b, 0, 0)),
        out_shape=jax.ShapeDtypeStruct((B, M, N), BF16),
        compiler_params=_params(1),
        name="mem_kv",
    )(mem, g, w)


def _post_mix_kernel(*refs, n_mix):
    x_ref = refs[0]
    o_refs = refs[1:1 + n_mix]
    w_refs = refs[1 + n_mix:1 + 2 * n_mix]
    gc_ref, wq_ref, kv_ref, wo_ref, out_ref = refs[1 + 2 * n_mix:]
    rows = _row_groups(x_ref.shape[1])
    xs = []
    for r in rows:
        x = x_ref[0, r, :]
        for o_ref, w_ref in zip(o_refs, w_refs):
            x = x + jnp.dot(o_ref[0, r, :], w_ref[...], preferred_element_type=F32)
        xs.append(x)
    qs = []
    for x in xs:
        hc = _rms_rows(x, gc_ref[...]).astype(BF16)
        qs.append((jnp.dot(hc, wq_ref[...], preferred_element_type=F32) * (X_DIM ** -0.5)).astype(BF16))
    heads = [[] for _ in rows]
    for h in range(X_HEADS):
        k_h = kv_ref[0, :, h * X_DIM:(h + 1) * X_DIM]
        v_h = kv_ref[0, :, D_MODEL + h * X_DIM:D_MODEL + (h + 1) * X_DIM]
        for g, q in enumerate(qs):
            s = lax.dot_general(q[:, h * X_DIM:(h + 1) * X_DIM], k_h, _NT, preferred_element_type=F32)
            p = jnp.exp(s - jnp.max(s, axis=-1, keepdims=True))
            l = jnp.sum(p, axis=-1, keepdims=True)
            heads[g].append((jnp.dot(p.astype(BF16), v_h, preferred_element_type=F32) / l).astype(BF16))
    for r, x, hs in zip(rows, xs, heads):
        o = jnp.concatenate(hs, axis=-1)
        out_ref[0, r, :] = x + jnp.dot(o, wo_ref[...], preferred_element_type=F32)


def _post_mix(x, mixes, weights, gc, wq, kv, wo):
    B, S, D = x.shape
    tm = min(GROUPED_TOKEN_TILE, S)
    n = len(mixes)
    M = kv.shape[1]
    tok = lambda b, i: (b, i, 0)
    return pl.pallas_call(
        functools.partial(_post_mix_kernel, n_mix=n),
        grid=(B, S // tm),
        in_specs=([pl.BlockSpec((1, tm, D), tok)]
                  + [pl.BlockSpec((1, tm, m.shape[2]), tok) for m in mixes]
                  + [_const_spec(w.shape) for w in weights]
                  + [_const_spec((1, D)), _const_spec((D, D)),
                     pl.BlockSpec((1, M, 2 * D), lambda b, i: (b, 0, 0)),
                     _const_spec((D, D))]),
        out_specs=pl.BlockSpec((1, tm, D), tok),
        out_shape=jax.ShapeDtypeStruct((B, S, D), F32),
        compiler_params=_params(2),
        name="post_mix",
    )(x, *mixes, *weights, gc, wq, kv, wo)


def _ffn_kernel(x_ref, g_ref, wgu_ref, wd_ref, gf_ref, out_ref, *, final_norm):
    groups = _row_groups(x_ref.shape[1])
    accs = [x_ref[0, r, :] for r in groups]
    xns = [_rms_rows(x, g_ref[...]).astype(BF16) for x in accs]
    for c in range(D_FF // FF_CHUNK):
        lo = c * FF_CHUNK
        hidden = []
        for xn in xns:
            gate = jnp.dot(xn, wgu_ref[:, lo:lo + FF_CHUNK], preferred_element_type=F32)
            up = jnp.dot(xn, wgu_ref[:, D_FF + lo:D_FF + lo + FF_CHUNK], preferred_element_type=F32)
            hidden.append((jax.nn.silu(gate) * up).astype(BF16))
        for g, h in enumerate(hidden):
            accs[g] = accs[g] + jnp.dot(h, wd_ref[lo:lo + FF_CHUNK, :], preferred_element_type=F32)
    for r, acc in zip(groups, accs):
        if final_norm:
            acc = _rms_rows(acc, gf_ref[...])
        out_ref[0, r, :] = acc


def _ffn(x, g, wgu, wd, gf, final_norm):
    B, S, D = x.shape
    tm = min(GROUPED_TOKEN_TILE, S)
    tok = lambda b, i: (b, i, 0)
    resident = lambda shape: pl.BlockSpec(shape, lambda *_: (0,) * len(shape),
                                          pipeline_mode=pl.Buffered(1))
    return pl.pallas_call(
        functools.partial(_ffn_kernel, final_norm=final_norm),
        grid=(B, S // tm),
        in_specs=[pl.BlockSpec((1, tm, D), tok), _const_spec((1, D)),
                  resident(wgu.shape), resident(wd.shape), _const_spec((1, D))],
        out_specs=pl.BlockSpec((1, tm, D), tok),
        out_shape=jax.ShapeDtypeStruct((B, S, D), F32),
        compiler_params=_params(2),
        name="ffn",
    )(x, g, wgu, wd, gf)


def _rope_freqs(n_pairs):
    return ROPE_THETA ** (-jnp.arange(n_pairs, dtype=F32) / n_pairs)


def _axial_angles(S):
    rows = S // GRID_W
    r = jnp.repeat(jnp.arange(rows, dtype=F32), GRID_W)
    c = jnp.tile(jnp.arange(GRID_W, dtype=F32), rows)
    f = _rope_freqs(A_DIM // 4)
    return jnp.concatenate([r[:, None] * f, c[:, None] * f], axis=-1)


def _linear_angles(S, dim):
    t = jnp.arange(S, dtype=F32)
    return t[:, None] * _rope_freqs(dim // 2)


def _deinterleave(n):
    return np.concatenate([np.arange(0, n, 2), np.arange(1, n, 2)])


def _prep_even(w_in, gq, gk):
    perm = _deinterleave(A_DIM)
    o = 0
    w_qa = w_in[:, o:o + A_Q].reshape(D_MODEL, A_HEADS, A_DIM)[:, :, perm].reshape(D_MODEL, A_Q)
    o += A_Q
    w_ka = w_in[:, o:o + A_KV].reshape(D_MODEL, A_KV_HEADS, A_DIM)[:, :, perm].reshape(D_MODEL, A_KV)
    o += A_KV
    w_va = w_in[:, o:o + A_KV]
    o += A_KV
    w_qb = w_in[:, o:o + B_QK]
    o += B_QK
    w_kb = w_in[:, o:o + B_QK]
    o += B_QK
    w_vb = w_in[:, o:o + B_V]
    wt = jnp.concatenate([w_qa, w_ka, w_va, w_qb, w_vb], axis=1).T.astype(BF16)
    return wt, w_kb.astype(BF16), gq[perm].reshape(A_DIM, 1), gk[perm].reshape(A_DIM, 1)


def _prep_odd(w_in, w_uq, w_ukv):
    hr = C_ROPE // 2
    perm = _deinterleave(C_ROPE)
    w_kr = w_in[:, C_Q_RANK + C_KV_RANK:][:, perm]
    w_kr_rot = jnp.concatenate([-w_kr[:, hr:], w_kr[:, :hr]], axis=1)
    win = jnp.concatenate(
        [w_in[:, :C_Q_RANK + C_KV_RANK], w_kr, w_kr_rot,
         jnp.zeros((D_MODEL, C_PAD - 2 * C_ROPE), F32)], axis=1).astype(BF16)
    wq = w_uq.reshape(C_Q_RANK, C_HEADS, C_NOPE + C_ROPE)
    wq = jnp.concatenate(
        [wq[:, :, :C_NOPE], wq[:, :, C_NOPE:][:, :, perm],
         jnp.zeros((C_Q_RANK, C_HEADS, C_PAD - C_NOPE - C_ROPE), F32)], axis=2)
    wuqT = wq.reshape(C_Q_RANK, C_HEADS * C_PAD).T.astype(BF16)
    wkv = w_ukv.reshape(C_KV_RANK, C_HEADS, C_NOPE + C_VDIM)
    wkn = jnp.concatenate(
        [wkv[:, :, :C_NOPE], jnp.zeros((C_KV_RANK, C_HEADS, C_PAD - C_NOPE), F32)], axis=2)
    wkn = wkn.reshape(C_KV_RANK, C_HEADS * C_PAD).astype(BF16)
    wvT = wkv[:, :, C_NOPE:].reshape(C_KV_RANK, C_HEADS * C_VDIM).T.astype(BF16)
    return win, wuqT, wkn, wvT


def _trunk(x, mem, p):
    B, S, D = x.shape
    depth = p['norm_mix'].shape[0]
    row = lambda v: v.reshape(1, -1)

    ang_a = _axial_angles(S)
    cos_a, sin_a = jnp.cos(ang_a).T, jnp.sin(ang_a).T
    ang_l = _linear_angles(S, C_ROPE)
    cos_l, sin_l = jnp.cos(ang_l), jnp.sin(ang_l)
    zpad = jnp.zeros((S, C_PAD - 2 * C_ROPE), F32)
    zrope = jnp.zeros((S, C_ROPE), F32)
    cos_k = jnp.concatenate([cos_l, cos_l, zrope, zpad], axis=1)
    sin_k = jnp.concatenate([zrope, sin_l, sin_l, zpad], axis=1)
    slopes = jnp.asarray(2.0 ** (-8.0 * np.arange(1, B_HEADS + 1) / B_HEADS), dtype=F32)

    for layer in range(depth):
        if layer % 2 == 0:
            e = layer // 2
            wt, wkb, gq, gk = _prep_even(p['e_w_in'][e], p['e_q_norm'][e], p['e_k_norm'][e])
            qaT, ka, vaT, qbT, kb, vbT = _even_in(
                x, row(p['norm_mix'][layer]), wt, wkb, gq, gk, cos_a, sin_a)
            oa = _gqa(qaT, ka, vaT)
            lam_init = 0.8 - 0.6 * math.exp(-0.3 * layer)
            ob = _diff(slopes, row(p['e_lam_q1'][e]), row(p['e_lam_k1'][e]),
                       row(p['e_lam_q2'][e]), row(p['e_lam_k2'][e]),
                       p['e_subln'][e].reshape(B_VDIM, 1), qbT, kb, vbT, lam_init)
            w_out = p['e_w_out'][e].astype(BF16)
            mixes, weights = [oa, ob], [w_out[:A_Q], w_out[A_Q:]]
        else:
            o = layer // 2
            win, wuqT, wkn, wvT = _prep_odd(p['o_w_in'][o], p['o_w_uq'][o], p['o_w_ukv'][o])
            qT, k, vT = _odd_in(x, row(p['norm_mix'][layer]), win, row(p['o_q_norm'][o]),
                                row(p['o_kv_norm'][o]), wuqT, wkn, wvT,
                                cos_l.T, sin_l.T, cos_k, sin_k)
            mixes, weights = [_mla(qT, k, vT)], [p['o_w_out'][o].astype(BF16)]
        kv = _memkv(mem, row(p['norm_mem'][layer]), p['w_ckv'][layer].astype(BF16))
        x = _post_mix(x, mixes, weights, row(p['norm_cross'][layer]),
                      p['w_cq'][layer].astype(BF16), kv, p['w_co'][layer].astype(BF16))
        x = _ffn(x, row(p['norm_ffn'][layer]), p['w_gu'][layer].astype(BF16),
                 p['w_down'][layer].astype(BF16), row(p['final_norm']),
                 final_norm=(layer == depth - 1))
    return x


def kernel(x_prompt, x_sample, mem_prompt, mem_sample, norm_mix, e_w_in, e_q_norm, e_k_norm, e_lam_q1, e_lam_k1, e_lam_q2, e_lam_k2, e_subln, e_w_out, o_w_in, o_q_norm, o_kv_norm, o_w_uq, o_w_ukv, o_w_out, norm_cross, norm_mem, w_cq, w_ckv, w_co, norm_ffn, w_gu, w_down, final_norm):
    p = dict(norm_mix=norm_mix, e_w_in=e_w_in, e_q_norm=e_q_norm, e_k_norm=e_k_norm,
             e_lam_q1=e_lam_q1, e_lam_k1=e_lam_k1, e_lam_q2=e_lam_q2, e_lam_k2=e_lam_k2,
             e_subln=e_subln, e_w_out=e_w_out, o_w_in=o_w_in, o_q_norm=o_q_norm,
             o_kv_norm=o_kv_norm, o_w_uq=o_w_uq, o_w_ukv=o_w_ukv, o_w_out=o_w_out,
             norm_cross=norm_cross, norm_mem=norm_mem, w_cq=w_cq, w_ckv=w_ckv, w_co=w_co,
             norm_ffn=norm_ffn, w_gu=w_gu, w_down=w_down, final_norm=final_norm)
    return (_trunk(x_prompt, mem_prompt, p), _trunk(x_sample, mem_sample, p))
```
